```python
import jax, jax.numpy as jnp
from jax import lax
import numpy as np

D_MODEL = 2048
BATCH = 2
SEQ = 8192
DEPTH = 1

GRID_W = 64
HEAD_DIM = 128
N_HEADS = D_MODEL // HEAD_DIM
N_HEADS_NA = N_HEADS // 4
N_HEADS_DIL = N_HEADS - N_HEADS_NA
W_NA = N_HEADS_NA * HEAD_DIM
W_DIL = N_HEADS_DIL * HEAD_DIM
NA_ROWS = 8
NA_COLS = 16
NA_QCOLS = 16
NA_KCOLS = NA_QCOLS + NA_COLS
DIL_PAIRS = ((128, 1), (512, 4), (2048, 16))
D_FF = 5632
PLE_DIM = 256
ROPE_THETA = 10000.0
EPS = 1e-6
NEG = -1e30

kernel_name = "hymba_style_na_dilated_macaron_encoder"


def _rmsnorm(x, g):
    xf = x.astype(jnp.float32)
    y = xf * lax.rsqrt(jnp.mean(xf * xf, axis=-1, keepdims=True) + EPS) * g.astype(jnp.float32)
    return y.astype(x.dtype)


def _swiglu(u, w_gate, w_up, w_down):
    return (jax.nn.silu(u @ w_gate) * (u @ w_up)) @ w_down


def _rope(t):
    S, hd = t.shape[2], t.shape[3]
    inv = jnp.float32(ROPE_THETA) ** (-jnp.arange(0, hd, 2, dtype=jnp.float32) / hd)
    ang = jnp.arange(S, dtype=jnp.float32)[:, None] * inv[None, :]
    cos, sin = jnp.cos(ang), jnp.sin(ang)
    tf = t.astype(jnp.float32)
    t1, t2 = tf[..., : hd // 2], tf[..., hd // 2:]
    return jnp.concatenate([t1 * cos - t2 * sin, t2 * cos + t1 * sin], axis=-1).astype(t.dtype)


def _na_indices(rows):
    kr = min(NA_ROWS, rows)
    n_cb = GRID_W // NA_QCOLS
    r = np.arange(rows)
    rs = np.clip(r - kr // 2, 0, rows - kr)
    krow = rs[:, None] + np.arange(kr)[None, :]
    cb0 = np.arange(n_cb) * NA_QCOLS
    kcs = np.clip(cb0 - NA_COLS // 2, 0, GRID_W - NA_KCOLS)
    kcol = kcs[:, None] + np.arange(NA_KCOLS)[None, :]
    qcol = cb0[:, None] + np.arange(NA_QCOLS)[None, :]
    qs = np.clip(qcol - NA_COLS // 2, 0, GRID_W - NA_COLS)
    tok = krow[:, None, :, None] * GRID_W + kcol[None, :, None, :]
    col_ok = (kcol[:, None, :] >= qs[:, :, None]) & (kcol[:, None, :] < qs[:, :, None] + NA_COLS)
    col_ok = np.broadcast_to(col_ok[:, :, None, :], (n_cb, NA_QCOLS, kr, NA_KCOLS)).reshape(n_cb, NA_QCOLS, kr * NA_KCOLS)
    dr = krow - r[:, None] + NA_ROWS - 1
    dc = np.clip(kcol[:, None, :] - qcol[:, :, None] + NA_COLS - 1, 0, 2 * NA_COLS - 2)
    return kr, n_cb, tok, col_ok, dr, dc


def _neighborhood_attention(q, k, v, rpb):
    B, H, S, hd = q.shape
    rows = S // GRID_W
    kr, n_cb, tok, col_ok, dr, dc = _na_indices(rows)
    nk = kr * NA_KCOLS
    qb = q.reshape(B, H, rows, n_cb, NA_QCOLS, hd)
    flat = tok.reshape(-1)
    kb = jnp.take(k, flat, axis=2).reshape(B, H, rows, n_cb, nk, hd)
    vb = jnp.take(v, flat, axis=2).reshape(B, H, rows, n_cb, nk, hd)
    bias = rpb[:, dr[:, None, None, :, None], dc[None, :, :, None, :]]
    bias = bias.reshape(H, rows, n_cb, NA_QCOLS, nk).astype(jnp.float32)
    s = jnp.einsum('bhrcqd,bhrckd->bhrcqk', qb, kb).astype(jnp.float32) * (hd ** -0.5) + bias[None]
    s = jnp.where(col_ok, s, NEG)
    pr = jax.nn.softmax(s, axis=-1)
    o = jnp.einsum('bhrcqk,bhrckd->bhrcqd', pr.astype(v.dtype), vb)
    return o.reshape(B, H, S, hd)


def _dilated_branch(q, k, v, window, dil):
    B, H, S, hd = q.shape
    half = window // (2 * dil)
    blk = half
    L = S // dil
    nb = -(-L // blk)
    Lp = nb * blk

    def fold(t):
        return t.reshape(B, H, L, dil, hd).transpose(0, 1, 3, 2, 4)

    def kwin(t):
        tp = jnp.pad(t, ((0, 0), (0, 0), (0, 0), (blk, Lp - L + blk), (0, 0))).reshape(B, H, dil, nb + 2, blk, hd)
        return jnp.concatenate([tp[:, :, :, :-2], tp[:, :, :, 1:-1], tp[:, :, :, 2:]], axis=4)

    qb = jnp.pad(fold(q), ((0, 0), (0, 0), (0, 0), (0, Lp - L), (0, 0))).reshape(B, H, dil, nb, blk, hd)
    kb, vb = kwin(fold(k)), kwin(fold(v))
    m_q = np.arange(nb)[:, None] * blk + np.arange(blk)[None, :]
    m_k = np.arange(nb)[:, None] * blk - blk + np.arange(3 * blk)[None, :]
    mk = m_k[:, None, :]
    valid = (np.abs(mk - m_q[:, :, None]) <= half) & (mk >= 0) & (mk < L)
    s = jnp.einsum('bhrnqd,bhrnkd->bhrnqk', qb, kb).astype(jnp.float32) * (hd ** -0.5)
    s = jnp.where(valid, s, NEG)
    mx = jnp.max(s, axis=-1, keepdims=True)
    e = jnp.exp(s - mx)
    den = jnp.sum(e, axis=-1, keepdims=True)
    o = jnp.einsum('bhrnqk,bhrnkd->bhrnqd', e, vb.astype(jnp.float32)) / den
    lse = (mx + jnp.log(den))[..., 0]
    o = o.reshape(B, H, dil, Lp, hd)[:, :, :, :L].transpose(0, 1, 3, 2, 4).reshape(B, H, S, hd)
    lse = lse.reshape(B, H, dil, Lp)[:, :, :, :L].transpose(0, 1, 3, 2).reshape(B, H, S)
    return o, lse


def _dilated_mixture(q, k, v):
    res = [_dilated_branch(q, k, v, w, d) for (w, d) in DIL_PAIRS]
    o_all = jnp.stack([r[0] for r in res], axis=0)
    wts = jax.nn.softmax(jnp.stack([r[1] for r in res], axis=0), axis=0)
    return jnp.einsum('pbhs,pbhsd->bhsd', wts, o_all).astype(q.dtype)


def _mixer(u, w_qkv, na_rpb, out_g, w_o):
    B, S, _ = u.shape
    qkv = u @ w_qkv
    cuts = np.cumsum([W_NA, W_NA, W_NA, W_DIL, W_DIL])
    qa, ka, va, qd, kd, vd = jnp.split(qkv, cuts, axis=-1)

    def heads(t):
        return t.reshape(B, S, -1, HEAD_DIM).transpose(0, 2, 1, 3)

    o_na = _neighborhood_attention(heads(qa), heads(ka), heads(va), na_rpb)
    o_dil = _dilated_mixture(_rope(heads(qd)), _rope(heads(kd)), heads(vd))
    o = jnp.concatenate([o_na, o_dil], axis=1).astype(jnp.float32)
    o = o * lax.rsqrt(jnp.mean(o * o, axis=-1, keepdims=True) + EPS)
    o = o.transpose(0, 2, 1, 3).reshape(B, S, D_MODEL) * out_g.astype(jnp.float32)
    return o.astype(u.dtype) @ w_o


def setup_inputs(seed: int = 0) -> dict:
    key = jax.random.key(seed)
    ks = jax.random.split(key, 24)
    D = D_MODEL
    f32 = jnp.float32

    def nrm(k, shape, scale):
        return jax.random.normal(k, shape, f32) * scale

    def gain(k):
        return 1.0 + 0.05 * jax.random.normal(k, (DEPTH, D), f32)

    return {
        "x": nrm(ks[0], (BATCH, SEQ, D), 1.0),
        "p": nrm(ks[1], (DEPTH, BATCH, SEQ, PLE_DIM), 1.0),
        "ffn1_pre_g": gain(ks[2]),
        "ffn1_w_gate": nrm(ks[3], (DEPTH, D, D_FF), D ** -0.5),
        "ffn1_w_up": nrm(ks[4], (DEPTH, D, D_FF), D ** -0.5),
        "ffn1_w_down": nrm(ks[5], (DEPTH, D_FF, D), D_FF ** -0.5),
        "ffn1_post_g": gain(ks[6]),
        "mix_pre_g": gain(ks[7]),
        "w_qkv": nrm(ks[8], (DEPTH, D, 3 * D), D ** -0.5),
        "na_rpb": nrm(ks[9], (DEPTH, N_HEADS_NA, 2 * NA_ROWS - 1, 2 * NA_COLS - 1), 0.1),
        "out_g": gain(ks[10]),
        "w_o": nrm(ks[11], (DEPTH, D, D), D ** -0.5),
        "mix_post_g": gain(ks[12]),
        "ffn2_pre_g": gain(ks[13]),
        "ffn2_w_gate": nrm(ks[14], (DEPTH, D, D_FF), D ** -0.5),
        "ffn2_w_up": nrm(ks[15], (DEPTH, D, D_FF), D ** -0.5),
        "ffn2_w_down": nrm(ks[16], (DEPTH, D_FF, D), D_FF ** -0.5),
        "ffn2_post_g": gain(ks[17]),
        "ple_pre_g": gain(ks[18]),
        "w_ple_gate": nrm(ks[19], (DEPTH, D, D), D ** -0.5),
        "w_ple_proj": nrm(ks[20], (DEPTH, PLE_DIM, D), PLE_DIM ** -0.5),
        "ple_post_g": gain(ks[21]),
    }


def reference(x, p, ffn1_pre_g, ffn1_w_gate, ffn1_w_up, ffn1_w_down, ffn1_post_g,
              mix_pre_g, w_qkv, na_rpb, out_g, w_o, mix_post_g,
              ffn2_pre_g, ffn2_w_gate, ffn2_w_up, ffn2_w_down, ffn2_post_g,
              ple_pre_g, w_ple_gate, w_ple_proj, ple_post_g):
    h = x
    for i in range(DEPTH):
        f = _swiglu(_rmsnorm(h, ffn1_pre_g[i]), ffn1_w_gate[i], ffn1_w_up[i], ffn1_w_down[i])
        h = h + 0.5 * _rmsnorm(f, ffn1_post_g[i])
        m = _mixer(_rmsnorm(h, mix_pre_g[i]), w_qkv[i], na_rpb[i], out_g[i], w_o[i])
        h = h + _rmsnorm(m, mix_post_g[i])
        f = _swiglu(_rmsnorm(h, ffn2_pre_g[i]), ffn2_w_gate[i], ffn2_w_up[i], ffn2_w_down[i])
        h = h + 0.5 * _rmsnorm(f, ffn2_post_g[i])
        g = jax.nn.sigmoid(_rmsnorm(h, ple_pre_g[i]) @ w_ple_gate[i])
        h = h + _rmsnorm(g * (p[i] @ w_ple_proj[i]), ple_post_g[i])
    return h
```

```python
import functools

import jax
import jax.numpy as jnp
import numpy as np
from jax import lax
from jax.experimental import pallas as pl
from jax.experimental.pallas import tpu as pltpu

D_MODEL = 2048
D_FF = 5632
HEAD_DIM = 128
N_HEADS = 16
N_HEADS_NA = 4
N_HEADS_DIL = 12
W_NA = N_HEADS_NA * HEAD_DIM
W_DIL = N_HEADS_DIL * HEAD_DIM
GRID_W = 64
NA_ROWS = 8
NA_COLS = 16
DIL_PAIRS = ((128, 1), (512, 4), (2048, 16))
PLE_DIM = 256
ROPE_THETA = 10000.0
EPS = 1e-6
NEG = -1e30
SCALE = HEAD_DIM ** -0.5

F32 = jnp.float32
BF16 = jnp.bfloat16

VMEM_LIMIT_BYTES = 56 * 1024 * 1024

FFN_TM = 512
FFN_TF = 512
QKV_TM = 1024
QKV_TN = 768
NA_QROWS = 8
NA_KROWS = 16
DIL_TQ = 256
DIL_HALF = 64
MIX_TM = 256
PLE_TM = 512


def _rms(x, g):
    return x * lax.rsqrt(jnp.mean(x * x, axis=-1, keepdims=True) + EPS) * g


def _params(*sem):
    return pltpu.CompilerParams(dimension_semantics=sem, vmem_limit_bytes=VMEM_LIMIT_BYTES)


def _ffn_kernel(x_ref, pre_g_ref, wg_ref, wu_ref, wd_ref, post_g_ref, o_ref, u_ref, acc_ref):
    j = pl.program_id(1)

    @pl.when(j == 0)
    def _():
        u_ref[...] = _rms(x_ref[...], pre_g_ref[...]).astype(BF16)

    u = u_ref[...]
    g = jnp.dot(u, wg_ref[...], preferred_element_type=F32)
    v = jnp.dot(u, wu_ref[...], preferred_element_type=F32)
    mid = (g * jax.nn.sigmoid(g) * v).astype(BF16)
    part = jnp.dot(mid, wd_ref[...], preferred_element_type=F32)

    @pl.when(j == 0)
    def _():
        acc_ref[...] = part

    @pl.when(j > 0)
    def _():
        acc_ref[...] += part

    @pl.when(j == pl.num_programs(1) - 1)
    def _():
        o_ref[...] = x_ref[...] + 0.5 * _rms(acc_ref[...], post_g_ref[...])


def _ffn(x, pre_g, w_gate, w_up, w_down, post_g):
    t = x.shape[0]
    return pl.pallas_call(
        _ffn_kernel,
        name="ffn",
        grid=(t // FFN_TM, D_FF // FFN_TF),
        in_specs=[
            pl.BlockSpec((FFN_TM, D_MODEL), lambda i, j: (i, 0)),
            pl.BlockSpec((1, D_MODEL), lambda i, j: (0, 0)),
            pl.BlockSpec((D_MODEL, FFN_TF), lambda i, j: (0, j)),
            pl.BlockSpec((D_MODEL, FFN_TF), lambda i, j: (0, j)),
            pl.BlockSpec((FFN_TF, D_MODEL), lambda i, j: (j, 0)),
            pl.BlockSpec((1, D_MODEL), lambda i, j: (0, 0)),
        ],
        out_specs=pl.BlockSpec((FFN_TM, D_MODEL), lambda i, j: (i, 0)),
        out_shape=jax.ShapeDtypeStruct((t, D_MODEL), F32),
        scratch_shapes=[pltpu.VMEM((FFN_TM, D_MODEL), BF16), pltpu.VMEM((FFN_TM, D_MODEL), F32)],
        compiler_params=_params("parallel", "arbitrary"),
    )(x, pre_g, w_gate, w_up, w_down, post_g)


_QKV_HEADS_PER_STEP = QKV_TN // HEAD_DIM
_ROPE_FIRST_STEP = 3 * W_NA // QKV_TN
_ROPE_END_STEP = (3 * W_NA + 2 * W_DIL) // QKV_TN


def _qkv_kernel(x_ref, g_ref, w_ref, cos_ref, sin_ref, o_ref, u_ref):
    j = pl.program_id(1)

    @pl.when(j == 0)
    def _():
        u_ref[...] = _rms(x_ref[...], g_ref[...]).astype(BF16)

    y = jnp.dot(u_ref[...], w_ref[...], preferred_element_type=F32)
    rope = jnp.logical_and(j >= _ROPE_FIRST_STEP, j < _ROPE_END_STEP)

    @pl.when(rope)
    def _():
        cos = cos_ref[...]
        sin = sin_ref[...]
        for h in range(_QKV_HEADS_PER_STEP):
            t = y[:, h * HEAD_DIM:(h + 1) * HEAD_DIM]
            swapped = pltpu.roll(t, HEAD_DIM // 2, axis=1)
            o_ref[:, h * HEAD_DIM:(h + 1) * HEAD_DIM] = (t * cos + swapped * sin).astype(BF16)

    @pl.when(jnp.logical_not(rope))
    def _():
        o_ref[...] = y.astype(BF16)


def _qkv(x, g, w, cos_full, sin_signed, seq):
    t = x.shape[0]
    tiles_per_seq = seq // QKV_TM
    return pl.pallas_call(
        _qkv_kernel,
        name="qkv",
        grid=(t // QKV_TM, 3 * D_MODEL // QKV_TN),
        in_specs=[
            pl.BlockSpec((QKV_TM, D_MODEL), lambda i, j: (i, 0)),
            pl.BlockSpec((1, D_MODEL), lambda i, j: (0, 0)),
            pl.BlockSpec((D_MODEL, QKV_TN), lambda i, j: (0, j)),
            pl.BlockSpec((QKV_TM, HEAD_DIM), lambda i, j: (i % tiles_per_seq, 0)),
            pl.BlockSpec((QKV_TM, HEAD_DIM), lambda i, j: (i % tiles_per_seq, 0)),
        ],
        out_specs=pl.BlockSpec((QKV_TM, QKV_TN), lambda i, j: (i, j)),
        out_shape=jax.ShapeDtypeStruct((t, 3 * D_MODEL), BF16),
        scratch_shapes=[pltpu.VMEM((QKV_TM, D_MODEL), BF16)],
        compiler_params=_params("parallel", "arbitrary"),
    )(x, g, w, cos_full, sin_signed)


def _rope_tables(seq):
    inv = jnp.float32(ROPE_THETA) ** (-jnp.arange(0, HEAD_DIM, 2, dtype=F32) / HEAD_DIM)
    ang = jnp.arange(seq, dtype=F32)[:, None] * inv[None, :]
    cos, sin = jnp.cos(ang), jnp.sin(ang)
    return jnp.concatenate([cos, cos], axis=-1), jnp.concatenate([-sin, sin], axis=-1)


def _na_tile_key_row_start(i, rows):
    return jnp.clip(i * NA_QROWS - NA_ROWS // 2, 0, rows - NA_KROWS)


def _na_kernel(q_ref, k_ref, v_ref, bias_ref, o_ref, *, rows):
    i = pl.program_id(2)
    start = pl.multiple_of(_na_tile_key_row_start(i, rows) * GRID_W, GRID_W)
    nk = NA_KROWS * GRID_W
    q = q_ref[0]
    k = k_ref[0, pl.ds(start, nk), :]
    v = v_ref[0, pl.ds(start, nk), :]
    s = lax.dot_general(q, k, (((1,), (1,)), ((), ())), preferred_element_type=F32) * SCALE
    s = s + bias_ref[0, 0]
    m = jnp.max(s, axis=-1, keepdims=True)
    e = jnp.exp(s - m)
    den = jnp.sum(e, axis=-1, keepdims=True)
    o = jnp.dot(e.astype(BF16), v, preferred_element_type=F32)
    o_ref[0] = o / den


def _na_bias_table(rpb, rows):
    n_tiles = rows // NA_QROWS
    dr_idx = np.zeros((3, NA_QROWS, NA_KROWS), np.int32)
    row_ok = np.zeros((3, NA_QROWS, NA_KROWS), bool)
    for cls, tile in enumerate((0, 1, n_tiles - 1)):
        ks = int(np.clip(tile * NA_QROWS - NA_ROWS // 2, 0, rows - NA_KROWS))
        r = tile * NA_QROWS + np.arange(NA_QROWS)
        rs = np.clip(r - NA_ROWS // 2, 0, rows - NA_ROWS)
        kr = ks + np.arange(NA_KROWS)
        row_ok[cls] = (kr[None, :] >= rs[:, None]) & (kr[None, :] < rs[:, None] + NA_ROWS)
        dr_idx[cls] = np.clip(kr[None, :] - r[:, None] + NA_ROWS - 1, 0, 2 * NA_ROWS - 2)
    c = np.arange(GRID_W)
    qs = np.clip(c - NA_COLS // 2, 0, GRID_W - NA_COLS)
    col_ok = (c[None, :] >= qs[:, None]) & (c[None, :] < qs[:, None] + NA_COLS)
    dc_idx = np.clip(c[None, :] - c[:, None] + NA_COLS - 1, 0, 2 * NA_COLS - 2)
    bias = rpb[:, dr_idx[:, :, :, None, None], dc_idx[None, None, None, :, :]]
    ok = row_ok[:, :, :, None, None] & col_ok[None, None, None, :, :]
    bias = jnp.where(ok[None], bias.astype(F32), NEG)
    bias = bias.transpose(1, 0, 2, 4, 3, 5)
    return bias.reshape(3, N_HEADS_NA, NA_QROWS * GRID_W, NA_KROWS * GRID_W)


def _na(qkv, bias, batch, seq):
    rows = seq // GRID_W
    n_tiles = rows // NA_QROWS
    tq = NA_QROWS * GRID_W

    def bias_map(b, h, i):
        return (jnp.where(i == 0, 0, jnp.where(i == n_tiles - 1, 2, 1)), h, 0, 0)

    return pl.pallas_call(
        functools.partial(_na_kernel, rows=rows),
        name="na",
        grid=(batch, N_HEADS_NA, n_tiles),
        in_specs=[
            pl.BlockSpec((1, tq, HEAD_DIM), lambda b, h, i: (b, i, h)),
            pl.BlockSpec((1, seq, HEAD_DIM), lambda b, h, i: (b, 0, N_HEADS_NA + h)),
            pl.BlockSpec((1, seq, HEAD_DIM), lambda b, h, i: (b, 0, 2 * N_HEADS_NA + h)),
            pl.BlockSpec((1, 1, tq, NA_KROWS * GRID_W), bias_map),
        ],
        out_specs=pl.BlockSpec((1, tq, HEAD_DIM), lambda b, h, i: (b, i, h)),
        out_shape=jax.ShapeDtypeStruct((batch, seq, W_NA), F32),
        compiler_params=_params("parallel", "parallel", "arbitrary"),
    )(qkv, qkv, qkv, bias)


def _dil_kernel(q_ref, k_ref, v_ref, o_ref, lse_ref, *, length):
    i = pl.program_id(3)
    win = DIL_TQ + 2 * DIL_HALF
    start = pl.multiple_of(jnp.clip(i * DIL_TQ - DIL_HALF, 0, length - win), DIL_HALF)
    q = q_ref[0]
    k = k_ref[0, pl.ds(start, win), :]
    v = v_ref[0, pl.ds(start, win), :]
    s = lax.dot_general(q, k, (((1,), (1,)), ((), ())), preferred_element_type=F32) * SCALE
    qpos = i * DIL_TQ + lax.broadcasted_iota(jnp.int32, (DIL_TQ, win), 0)
    kpos = start + lax.broadcasted_iota(jnp.int32, (DIL_TQ, win), 1)
    s = jnp.where(jnp.abs(kpos - qpos) <= DIL_HALF, s, NEG)
    m = jnp.max(s, axis=-1, keepdims=True)
    e = jnp.exp(s - m)
    den = jnp.sum(e, axis=-1, keepdims=True)
    o = jnp.dot(e.astype(BF16), v, preferred_element_type=F32)
    o_ref[0] = o / den
    lse = jnp.broadcast_to(m + jnp.log(den), (DIL_TQ, HEAD_DIM))
    lse_ref[0] = lse.T[0:8, :]


def _dilated_branch(qkv, dil, batch, seq):
    length = seq // dil
    cols = 3 * D_MODEL // HEAD_DIM
    q0 = 3 * N_HEADS_NA
    k0 = q0 + N_HEADS_DIL
    v0 = k0 + N_HEADS_DIL
    folded = qkv.reshape(batch, length, dil * 3 * D_MODEL)
    o, lse = pl.pallas_call(
        functools.partial(_dil_kernel, length=length),
        name=f"dilated_{dil}",
        grid=(batch, N_HEADS_DIL, dil, length // DIL_TQ),
        in_specs=[
            pl.BlockSpec((1, DIL_TQ, HEAD_DIM), lambda b, h, r, i: (b, i, r * cols + q0 + h)),
            pl.BlockSpec((1, length, HEAD_DIM), lambda b, h, r, i: (b, 0, r * cols + k0 + h)),
            pl.BlockSpec((1, length, HEAD_DIM), lambda b, h, r, i: (b, 0, r * cols + v0 + h)),
        ],
        out_specs=[
            pl.BlockSpec((1, DIL_TQ, HEAD_DIM), lambda b, h, r, i: (b, i, r * N_HEADS_DIL + h)),
            pl.BlockSpec((1, 8, DIL_TQ), lambda b, h, r, i: ((b * N_HEADS_DIL + h) * dil + r, 0, i)),
        ],
        out_shape=[
            jax.ShapeDtypeStruct((batch, length, dil * W_DIL), F32),
            jax.ShapeDtypeStruct((batch * N_HEADS_DIL * dil, 8, length), F32),
        ],
        compiler_params=_params("parallel", "parallel", "parallel", "arbitrary"),
    )(folded, folded, folded)
    o = o.reshape(batch * seq, W_DIL)
    lse = lse[:, 0, :].reshape(batch, N_HEADS_DIL, dil, length)
    lse = lse.transpose(0, 3, 2, 1).reshape(batch * seq, N_HEADS_DIL)
    return o, lse


def _mix_out_kernel(ona_ref, o1_ref, o2_ref, o3_ref, lse_ref, outg_ref, wo_ref, h_ref, postg_ref, o_ref, buf_ref):
    lse = lse_ref[...]
    l1 = lse[:, 0:N_HEADS_DIL]
    l2 = lse[:, N_HEADS_DIL:2 * N_HEADS_DIL]
    l3 = lse[:, 2 * N_HEADS_DIL:3 * N_HEADS_DIL]
    mx = jnp.maximum(jnp.maximum(l1, l2), l3)
    e1, e2, e3 = jnp.exp(l1 - mx), jnp.exp(l2 - mx), jnp.exp(l3 - mx)
    z = e1 + e2 + e3
    w1, w2, w3 = e1 / z, e2 / z, e3 / z
    for h in range(N_HEADS):
        cols = slice(h * HEAD_DIM, (h + 1) * HEAD_DIM)
        if h < N_HEADS_NA:
            o = ona_ref[:, cols]
        else:
            d = h - N_HEADS_NA
            dcols = slice(d * HEAD_DIM, (d + 1) * HEAD_DIM)
            o = (w1[:, d:d + 1] * o1_ref[:, dcols] + w2[:, d:d + 1] * o2_ref[:, dcols]
                 + w3[:, d:d + 1] * o3_ref[:, dcols])
        buf_ref[:, cols] = _rms(o, outg_ref[:, cols]).astype(BF16)
    m = jnp.dot(buf_ref[...], wo_ref[...], preferred_element_type=F32)
    o_ref[...] = h_ref[...] + _rms(m, postg_ref[...])


def _mix_out(o_na, o1, o2, o3, lse, out_g, w_o, h, post_g):
    t = h.shape[0]
    row = lambda width: pl.BlockSpec((MIX_TM, width), lambda i: (i, 0))
    const = lambda shape: pl.BlockSpec(shape, lambda i: (0, 0))
    return pl.pallas_call(
        _mix_out_kernel,
        name="mix_out",
        grid=(t // MIX_TM,),
        in_specs=[row(W_NA), row(W_DIL), row(W_DIL), row(W_DIL), row(3 * N_HEADS_DIL),
                  const((1, D_MODEL)), const((D_MODEL, D_MODEL)), row(D_MODEL), const((1, D_MODEL))],
        out_specs=row(D_MODEL),
        out_shape=jax.ShapeDtypeStruct((t, D_MODEL), F32),
        scratch_shapes=[pltpu.VMEM((MIX_TM, D_MODEL), BF16)],
        compiler_params=_params("parallel"),
    )(o_na, o1, o2, o3, lse, out_g, w_o, h, post_g)


def _ple_kernel(h_ref, p_ref, preg_ref, wg_ref, wp_ref, postg_ref, o_ref):
    h = h_ref[...]
    u = _rms(h, preg_ref[...]).astype(BF16)
    gate = jax.nn.sigmoid(jnp.dot(u, wg_ref[...], preferred_element_type=F32))
    emb = jnp.dot(p_ref[...].astype(BF16), wp_ref[...], preferred_element_type=F32)
    o_ref[...] = h + _rms(gate * emb, postg_ref[...])


def _ple(h, p, pre_g, w_gate, w_proj, post_g):
    t = h.shape[0]
    row = lambda width: pl.BlockSpec((PLE_TM, width), lambda i: (i, 0))
    const = lambda shape: pl.BlockSpec(shape, lambda i: (0, 0))
    return pl.pallas_call(
        _ple_kernel,
        name="ple",
        grid=(t // PLE_TM,),
        in_specs=[row(D_MODEL), row(PLE_DIM), const((1, D_MODEL)), const((D_MODEL, D_MODEL)),
                  const((PLE_DIM, D_MODEL)), const((1, D_MODEL))],
        out_specs=row(D_MODEL),
        out_shape=jax.ShapeDtypeStruct((t, D_MODEL), F32),
        compiler_params=_params("parallel"),
    )(h, p, pre_g, w_gate, w_proj, post_g)


def kernel(x, p, ffn1_pre_g, ffn1_w_gate, ffn1_w_up, ffn1_w_down, ffn1_post_g, mix_pre_g, w_qkv, na_rpb, out_g, w_o, mix_post_g, ffn2_pre_g, ffn2_w_gate, ffn2_w_up, ffn2_w_down, ffn2_post_g, ple_pre_g, w_ple_gate, w_ple_proj, ple_post_g):
    batch, seq, d_model = x.shape
    depth = p.shape[0]
    assert d_model == D_MODEL and seq % (GRID_W * NA_KROWS) == 0
    tokens = batch * seq
    rows = seq // GRID_W
    cos_full, sin_signed = _rope_tables(seq)
    gain = lambda g: g.reshape(1, D_MODEL)
    w16 = lambda w: w.astype(BF16)

    h = x.reshape(tokens, D_MODEL)
    for i in range(depth):
        h = _ffn(h, gain(ffn1_pre_g[i]), w16(ffn1_w_gate[i]), w16(ffn1_w_up[i]), w16(ffn1_w_down[i]),
                 gain(ffn1_post_g[i]))
        qkv = _qkv(h, gain(mix_pre_g[i]), w16(w_qkv[i]), cos_full, sin_signed, seq)
        qkv = qkv.reshape(batch, seq, 3 * D_MODEL)
        o_na = _na(qkv, _na_bias_table(na_rpb[i], rows), batch, seq).reshape(tokens, W_NA)
        branches = [_dilated_branch(qkv, dil, batch, seq) for (_, dil) in DIL_PAIRS]
        lse = jnp.concatenate([b[1] for b in branches], axis=-1)
        h = _mix_out(o_na, branches[0][0], branches[1][0], branches[2][0], lse, gain(out_g[i]), w16(w_o[i]), h,
                     gain(mix_post_g[i]))
        h = _ffn(h, gain(ffn2_pre_g[i]), w16(ffn2_w_gate[i]), w16(ffn2_w_up[i]), w16(ffn2_w_down[i]),
                 gain(ffn2_post_g[i]))
        h = _ple(h, p[i].reshape(tokens, PLE_DIM), gain(ple_pre_g[i]), w16(w_ple_gate[i]), w16(w_ple_proj[i]),
                 gain(ple_post_g[i]))
    return h.reshape(batch, seq, D_MODEL)
```

```python
import functools

import jax
import jax.numpy as jnp
import numpy as np
from jax import lax
from jax.experimental import pallas as pl
from jax.experimental.pallas import tpu as pltpu

D_MODEL = 2048
D_FF = 5632
HEAD_DIM = 128
N_HEADS = 16
N_HEADS_NA = 4
N_HEADS_DIL = 12
W_NA = N_HEADS_NA * HEAD_DIM
W_DIL = N_HEADS_DIL * HEAD_DIM
GRID_W = 64
NA_ROWS = 8
NA_COLS = 16
DIL_PAIRS = ((128, 1), (512, 4), (2048, 16))
PLE_DIM = 256
ROPE_THETA = 10000.0
EPS = 1e-6
NEG = -1e30
SCALE = HEAD_DIM ** -0.5

F32 = jnp.float32
BF16 = jnp.bfloat16

VMEM_LIMIT_BYTES = 56 * 1024 * 1024

FFN_TM = 512
FFN_TF = 512
QKV_TM = 1024
QKV_TN = 768
NA_QROWS = 8
NA_KROWS = 16
DIL_TQ = 128
DIL_GROUP = 8
DIL_HALF = 64
MIX_TM = 256
PLE_TM = 512


def _rms(x, g):
    return x * lax.rsqrt(jnp.mean(x * x, axis=-1, keepdims=True) + EPS) * g


def _params(*sem):
    return pltpu.CompilerParams(dimension_semantics=sem, vmem_limit_bytes=VMEM_LIMIT_BYTES)


def _ffn_kernel(x_ref, pre_g_ref, wg_ref, wu_ref, wd_ref, post_g_ref, o_ref, u_ref, acc_ref):
    j = pl.program_id(1)

    @pl.when(j == 0)
    def _():
        u_ref[...] = _rms(x_ref[...], pre_g_ref[...]).astype(BF16)

    u = u_ref[...]
    g = jnp.dot(u, wg_ref[...], preferred_element_type=F32)
    v = jnp.dot(u, wu_ref[...], preferred_element_type=F32)
    mid = (g * jax.nn.sigmoid(g) * v).astype(BF16)
    part = jnp.dot(mid, wd_ref[...], preferred_element_type=F32)

    @pl.when(j == 0)
    def _():
        acc_ref[...] = part

    @pl.when(j > 0)
    def _():
        acc_ref[...] += part

    @pl.when(j == pl.num_programs(1) - 1)
    def _():
        o_ref[...] = x_ref[...] + 0.5 * _rms(acc_ref[...], post_g_ref[...])


def _ffn(x, pre_g, w_gate, w_up, w_down, post_g):
    t = x.shape[0]
    return pl.pallas_call(
        _ffn_kernel,
        name="ffn",
        grid=(t // FFN_TM, D_FF // FFN_TF),
        in_specs=[
            pl.BlockSpec((FFN_TM, D_MODEL), lambda i, j: (i, 0)),
            pl.BlockSpec((1, D_MODEL), lambda i, j: (0, 0)),
            pl.BlockSpec((D_MODEL, FFN_TF), lambda i, j: (0, j)),
            pl.BlockSpec((D_MODEL, FFN_TF), lambda i, j: (0, j)),
            pl.BlockSpec((FFN_TF, D_MODEL), lambda i, j: (j, 0)),
            pl.BlockSpec((1, D_MODEL), lambda i, j: (0, 0)),
        ],
        out_specs=pl.BlockSpec((FFN_TM, D_MODEL), lambda i, j: (i, 0)),
        out_shape=jax.ShapeDtypeStruct((t, D_MODEL), F32),
        scratch_shapes=[pltpu.VMEM((FFN_TM, D_MODEL), BF16), pltpu.VMEM((FFN_TM, D_MODEL), F32)],
        compiler_params=_params("parallel", "arbitrary"),
    )(x, pre_g, w_gate, w_up, w_down, post_g)


_QKV_HEADS_PER_STEP = QKV_TN // HEAD_DIM
_ROPE_FIRST_STEP = 3 * W_NA // QKV_TN
_ROPE_END_STEP = (3 * W_NA + 2 * W_DIL) // QKV_TN


def _qkv_kernel(x_ref, g_ref, w_ref, cos_ref, sin_ref, na_ref, *rest):
    fold_refs, (u_ref, y_ref) = rest[:len(DIL_PAIRS)], rest[len(DIL_PAIRS):]
    j = pl.program_id(1)

    @pl.when(j == 0)
    def _():
        u_ref[...] = _rms(x_ref[...], g_ref[...]).astype(BF16)

    y = jnp.dot(u_ref[...], w_ref[...], preferred_element_type=F32)
    rope = jnp.logical_and(j >= _ROPE_FIRST_STEP, j < _ROPE_END_STEP)

    @pl.when(j < _ROPE_FIRST_STEP)
    def _():
        na_ref[...] = y.astype(BF16)

    @pl.when(rope)
    def _():
        cos = cos_ref[...]
        sin = sin_ref[...]
        for h in range(_QKV_HEADS_PER_STEP):
            t = y[:, h * HEAD_DIM:(h + 1) * HEAD_DIM]
            y_ref[h] = t * cos + pltpu.roll(t, HEAD_DIM // 2, axis=1) * sin

    @pl.when(j >= _ROPE_END_STEP)
    def _():
        for h in range(_QKV_HEADS_PER_STEP):
            y_ref[h] = y[:, h * HEAD_DIM:(h + 1) * HEAD_DIM]

    @pl.when(j >= _ROPE_FIRST_STEP)
    def _():
        for h in range(_QKV_HEADS_PER_STEP):
            cols = slice(h * HEAD_DIM, (h + 1) * HEAD_DIM)
            for (_, dil), f_ref in zip(DIL_PAIRS, fold_refs):
                if dil == 1:
                    f_ref[0, 0, :, cols] = y_ref[h].astype(BF16)
                else:
                    for r in range(dil):
                        f_ref[0, r, :, cols] = y_ref[h, pl.ds(r, QKV_TM // dil, stride=dil), :].astype(BF16)


def _qkv(x, g, w, cos_full, sin_signed, batch, seq):
    t = x.shape[0]
    tiles_per_seq = seq // QKV_TM
    na_steps = _ROPE_FIRST_STEP

    def fold_spec(dil):
        return pl.BlockSpec(
            (1, dil, QKV_TM // dil, QKV_TN),
            lambda i, j: (i // tiles_per_seq, 0, i % tiles_per_seq, jnp.maximum(j - na_steps, 0)))

    return pl.pallas_call(
        _qkv_kernel,
        name="qkv",
        grid=(t // QKV_TM, 3 * D_MODEL // QKV_TN),
        in_specs=[
            pl.BlockSpec((QKV_TM, D_MODEL), lambda i, j: (i, 0)),
            pl.BlockSpec((1, D_MODEL), lambda i, j: (0, 0)),
            pl.BlockSpec((D_MODEL, QKV_TN), lambda i, j: (0, j)),
            pl.BlockSpec((QKV_TM, HEAD_DIM), lambda i, j: (i % tiles_per_seq, 0)),
            pl.BlockSpec((QKV_TM, HEAD_DIM), lambda i, j: (i % tiles_per_seq, 0)),
        ],
        out_specs=[pl.BlockSpec((QKV_TM, QKV_TN), lambda i, j: (i, jnp.minimum(j, na_steps - 1)))]
        + [fold_spec(dil) for (_, dil) in DIL_PAIRS],
        out_shape=[jax.ShapeDtypeStruct((t, 3 * W_NA), BF16)]
        + [jax.ShapeDtypeStruct((batch, dil, seq // dil, 3 * W_DIL), BF16) for (_, dil) in DIL_PAIRS],
        scratch_shapes=[pltpu.VMEM((QKV_TM, D_MODEL), BF16),
                        pltpu.VMEM((_QKV_HEADS_PER_STEP, QKV_TM, HEAD_DIM), F32)],
        compiler_params=_params("parallel", "arbitrary"),
    )(x, g, w, cos_full, sin_signed)


def _rope_tables(seq):
    inv = jnp.float32(ROPE_THETA) ** (-jnp.arange(0, HEAD_DIM, 2, dtype=F32) / HEAD_DIM)
    ang = jnp.arange(seq, dtype=F32)[:, None] * inv[None, :]
    cos, sin = jnp.cos(ang), jnp.sin(ang)
    return jnp.concatenate([cos, cos], axis=-1), jnp.concatenate([-sin, sin], axis=-1)


def _na_tile_key_row_start(i, rows):
    return jnp.clip(i * NA_QROWS - NA_ROWS // 2, 0, rows - NA_KROWS)


def _na_kernel(q_ref, k_ref, v_ref, bias_ref, o_ref, *, rows):
    i = pl.program_id(2)
    start = pl.multiple_of(_na_tile_key_row_start(i, rows) * GRID_W, GRID_W)
    nk = NA_KROWS * GRID_W
    q = q_ref[0]
    k = k_ref[0, pl.ds(start, nk), :]
    v = v_ref[0, pl.ds(start, nk), :]
    s = lax.dot_general(q, k, (((1,), (1,)), ((), ())), preferred_element_type=F32) * SCALE
    s = s + bias_ref[0, 0]
    m = jnp.max(s, axis=-1, keepdims=True)
    e = jnp.exp(s - m)
    den = jnp.sum(e, axis=-1, keepdims=True)
    o = jnp.dot(e.astype(BF16), v, preferred_element_type=F32)
    o_ref[0] = o / den


def _na_bias_table(rpb, rows):
    n_tiles = rows // NA_QROWS
    dr_idx = np.zeros((3, NA_QROWS, NA_KROWS), np.int32)
    row_ok = np.zeros((3, NA_QROWS, NA_KROWS), bool)
    for cls, tile in enumerate((0, 1, n_tiles - 1)):
        ks = int(np.clip(tile * NA_QROWS - NA_ROWS // 2, 0, rows - NA_KROWS))
        r = tile * NA_QROWS + np.arange(NA_QROWS)
        rs = np.clip(r - NA_ROWS // 2, 0, rows - NA_ROWS)
        kr = ks + np.arange(NA_KROWS)
        row_ok[cls] = (kr[None, :] >= rs[:, None]) & (kr[None, :] < rs[:, None] + NA_ROWS)
        dr_idx[cls] = np.clip(kr[None, :] - r[:, None] + NA_ROWS - 1, 0, 2 * NA_ROWS - 2)
    c = np.arange(GRID_W)
    qs = np.clip(c - NA_COLS // 2, 0, GRID_W - NA_COLS)
    col_ok = (c[None, :] >= qs[:, None]) & (c[None, :] < qs[:, None] + NA_COLS)
    n_dr, n_dc = 2 * NA_ROWS - 1, 2 * NA_COLS - 1
    period = 2 * GRID_W
    v = jnp.concatenate([rpb[..., NA_COLS - 1:], jnp.zeros((N_HEADS_NA, n_dr, period - n_dc), F32),
                         rpb[..., :NA_COLS - 1]], axis=-1).astype(F32)
    toe = jnp.tile(v, (1, 1, GRID_W))[..., :GRID_W * (period - 1)]
    toe = toe.reshape(N_HEADS_NA, n_dr, GRID_W, period - 1)[..., :GRID_W]
    slabs = jnp.where(col_ok[None, None], toe, NEG)
    slabs = jnp.concatenate([slabs, jnp.full((N_HEADS_NA, 1, GRID_W, GRID_W), NEG, F32)], axis=1)
    slab_idx = np.where(row_ok, dr_idx, n_dr).reshape(-1)
    bias = jnp.take(slabs, slab_idx, axis=1)
    bias = bias.reshape(N_HEADS_NA, 3, NA_QROWS, NA_KROWS, GRID_W, GRID_W)
    bias = bias.transpose(1, 0, 2, 4, 3, 5)
    return bias.reshape(3, N_HEADS_NA, NA_QROWS * GRID_W, NA_KROWS * GRID_W)


def _na(qkv, bias, batch, seq):
    rows = seq // GRID_W
    n_tiles = rows // NA_QROWS
    tq = NA_QROWS * GRID_W

    def bias_map(b, h, i):
        return (jnp.where(i == 0, 0, jnp.where(i == n_tiles - 1, 2, 1)), h, 0, 0)

    return pl.pallas_call(
        functools.partial(_na_kernel, rows=rows),
        name="na",
        grid=(batch, N_HEADS_NA, n_tiles),
        in_specs=[
            pl.BlockSpec((1, tq, HEAD_DIM), lambda b, h, i: (b, i, h)),
            pl.BlockSpec((1, seq, HEAD_DIM), lambda b, h, i: (b, 0, N_HEADS_NA + h)),
            pl.BlockSpec((1, seq, HEAD_DIM), lambda b, h, i: (b, 0, 2 * N_HEADS_NA + h)),
            pl.BlockSpec((1, 1, tq, NA_KROWS * GRID_W), bias_map),
        ],
        out_specs=pl.BlockSpec((1, tq, HEAD_DIM), lambda b, h, i: (b, i, h)),
        out_shape=jax.ShapeDtypeStruct((batch, seq, W_NA), F32),
        compiler_params=_params("parallel", "parallel", "arbitrary"),
    )(qkv, qkv, qkv, bias)


_DIL_WIN = DIL_TQ + 2 * DIL_HALF
_LSE_ROWS = 8


def _dil_kernel(q_ref, k_ref, v_ref, mask_ref, o_ref, lse_ref, *, dil, length):
    n_i = length // DIL_TQ

    def group(g, carry):
        tiles = []
        for n in range(DIL_GROUP):
            t = g * DIL_GROUP + n
            r = t // n_i
            i = t % n_i
            q0 = pl.multiple_of(i * DIL_TQ, DIL_TQ)
            k0 = pl.multiple_of(jnp.clip(q0 - DIL_HALF, 0, length - _DIL_WIN), DIL_HALF)
            edge = jnp.where(i == 0, 0, jnp.where(i == n_i - 1, 2, 1))
            tiles.append((r, q0, k0, edge))
        scores = []
        for r, q0, k0, edge in tiles:
            q = q_ref[0, r, pl.ds(q0, DIL_TQ), :]
            k = k_ref[0, r, pl.ds(k0, _DIL_WIN), :]
            s = lax.dot_general(q, k, (((1,), (1,)), ((), ())), preferred_element_type=F32) * SCALE
            scores.append(s + mask_ref[edge])
        stats = []
        for s in scores:
            m = jnp.max(s, axis=-1, keepdims=True)
            e = jnp.exp(s - m)
            stats.append((m, e, jnp.sum(e, axis=-1, keepdims=True)))
        outs = []
        for (r, q0, k0, edge), (m, e, den) in zip(tiles, stats):
            v = v_ref[0, r, pl.ds(k0, _DIL_WIN), :]
            outs.append(jnp.dot(e.astype(BF16), v, preferred_element_type=F32) / den)
        for (r, q0, k0, edge), (m, e, den), o in zip(tiles, stats, outs):
            if dil == 1:
                o_ref[0, pl.ds(q0, DIL_TQ), :] = o
            else:
                o_ref[0, pl.ds(r + dil * q0, DIL_TQ, stride=dil), :] = o
            lse = jnp.broadcast_to(m + jnp.log(den), (DIL_TQ, HEAD_DIM))
            lse_ref[0, 0, r, :, pl.ds(q0, DIL_TQ)] = lse.T[0:_LSE_ROWS, :]
        return carry

    lax.fori_loop(0, dil * n_i // DIL_GROUP, group, 0)


def _dil_mask_table(length):
    n_i = length // DIL_TQ
    qq = np.arange(DIL_TQ)[:, None]
    kk = np.arange(_DIL_WIN)[None, :]
    out = []
    for i in (0, 1, n_i - 1):
        k0 = int(np.clip(i * DIL_TQ - DIL_HALF, 0, length - _DIL_WIN))
        out.append(np.where(np.abs(k0 + kk - (i * DIL_TQ + qq)) <= DIL_HALF, 0.0, NEG))
    return jnp.asarray(np.stack(out), F32)


def _dilated_branch(folded, window, dil, batch, seq):
    assert window // (2 * dil) == DIL_HALF
    length = seq // dil
    assert length % DIL_TQ == 0 and length >= _DIL_WIN
    head_block = lambda first: pl.BlockSpec((1, dil, length, HEAD_DIM), lambda b, h: (b, 0, 0, first + h))
    o, lse = pl.pallas_call(
        functools.partial(_dil_kernel, dil=dil, length=length),
        name=f"dilated_{dil}",
        grid=(batch, N_HEADS_DIL),
        in_specs=[
            head_block(0), head_block(N_HEADS_DIL), head_block(2 * N_HEADS_DIL),
            pl.BlockSpec((3, DIL_TQ, _DIL_WIN), lambda b, h: (0, 0, 0)),
        ],
        out_specs=[
            pl.BlockSpec((1, seq, HEAD_DIM), lambda b, h: (b, 0, h)),
            pl.BlockSpec((1, 1, dil, _LSE_ROWS, length), lambda b, h: (b, h, 0, 0, 0)),
        ],
        out_shape=[
            jax.ShapeDtypeStruct((batch, seq, W_DIL), F32),
            jax.ShapeDtypeStruct((batch, N_HEADS_DIL, dil, _LSE_ROWS, length), F32),
        ],
        compiler_params=_params("parallel", "parallel"),
    )(folded, folded, folded, _dil_mask_table(length))
    o = o.reshape(batch * seq, W_DIL)
    lse = lse[:, :, :, 0, :].transpose(0, 3, 2, 1).reshape(batch * seq, N_HEADS_DIL)
    return o, lse


def _mix_out_kernel(ona_ref, o1_ref, o2_ref, o3_ref, lse_ref, outg_ref, wo_ref, h_ref, postg_ref, o_ref, buf_ref):
    lse = lse_ref[...]
    l1 = lse[:, 0:N_HEADS_DIL]
    l2 = lse[:, N_HEADS_DIL:2 * N_HEADS_DIL]
    l3 = lse[:, 2 * N_HEADS_DIL:3 * N_HEADS_DIL]
    mx = jnp.maximum(jnp.maximum(l1, l2), l3)
    e1, e2, e3 = jnp.exp(l1 - mx), jnp.exp(l2 - mx), jnp.exp(l3 - mx)
    z = e1 + e2 + e3
    w1, w2, w3 = e1 / z, e2 / z, e3 / z
    for h in range(N_HEADS):
        cols = slice(h * HEAD_DIM, (h + 1) * HEAD_DIM)
        if h < N_HEADS_NA:
            o = ona_ref[:, cols]
        else:
            d = h - N_HEADS_NA
            dcols = slice(d * HEAD_DIM, (d + 1) * HEAD_DIM)
            o = (w1[:, d:d + 1] * o1_ref[:, dcols] + w2[:, d:d + 1] * o2_ref[:, dcols]
                 + w3[:, d:d + 1] * o3_ref[:, dcols])
        buf_ref[:, cols] = _rms(o, outg_ref[:, cols]).astype(BF16)
    m = jnp.dot(buf_ref[...], wo_ref[...], preferred_element_type=F32)
    o_ref[...] = h_ref[...] + _rms(m, postg_ref[...])


def _mix_out(o_na, o1, o2, o3, lse, out_g, w_o, h, post_g):
    t = h.shape[0]
    row = lambda width: pl.BlockSpec((MIX_TM, width), lambda i: (i, 0))
    const = lambda shape: pl.BlockSpec(shape, lambda i: (0, 0))
    return pl.pallas_call(
        _mix_out_kernel,
        name="mix_out",
        grid=(t // MIX_TM,),
        in_specs=[row(W_NA), row(W_DIL), row(W_DIL), row(W_DIL), row(3 * N_HEADS_DIL),
                  const((1, D_MODEL)), const((D_MODEL, D_MODEL)), row(D_MODEL), const((1, D_MODEL))],
        out_specs=row(D_MODEL),
        out_shape=jax.ShapeDtypeStruct((t, D_MODEL), F32),
        scratch_shapes=[pltpu.VMEM((MIX_TM, D_MODEL), BF16)],
        compiler_params=_params("parallel"),
    )(o_na, o1, o2, o3, lse, out_g, w_o, h, post_g)


def _ple_kernel(h_ref, p_ref, preg_ref, wg_ref, wp_ref, postg_ref, o_ref):
    h = h_ref[...]
    u = _rms(h, preg_ref[...]).astype(BF16)
    gate = jax.nn.sigmoid(jnp.dot(u, wg_ref[...], preferred_element_type=F32))
    emb = jnp.dot(p_ref[...].astype(BF16), wp_ref[...], preferred_element_type=F32)
    o_ref[...] = h + _rms(gate * emb, postg_ref[...])


def _ple(h, p, pre_g, w_gate, w_proj, post_g):
    t = h.shape[0]
    row = lambda width: pl.BlockSpec((PLE_TM, width), lambda i: (i, 0))
    const = lambda shape: pl.BlockSpec(shape, lambda i: (0, 0))
    return pl.pallas_call(
        _ple_kernel,
        name="ple",
        grid=(t // PLE_TM,),
        in_specs=[row(D_MODEL), row(PLE_DIM), const((1, D_MODEL)), const((D_MODEL, D_MODEL)),
                  const((PLE_DIM, D_MODEL)), const((1, D_MODEL))],
        out_specs=row(D_MODEL),
        out_shape=jax.ShapeDtypeStruct((t, D_MODEL), F32),
        compiler_params=_params("parallel"),
    )(h, p, pre_g, w_gate, w_proj, post_g)


def kernel(x, p, ffn1_pre_g, ffn1_w_gate, ffn1_w_up, ffn1_w_down, ffn1_post_g, mix_pre_g, w_qkv, na_rpb, out_g, w_o, mix_post_g, ffn2_pre_g, ffn2_w_gate, ffn2_w_up, ffn2_w_down, ffn2_post_g, ple_pre_g, w_ple_gate, w_ple_proj, ple_post_g):
    batch, seq, d_model = x.shape
    depth = p.shape[0]
    assert d_model == D_MODEL and seq % (GRID_W * NA_KROWS) == 0
    tokens = batch * seq
    rows = seq // GRID_W
    cos_full, sin_signed = _rope_tables(seq)
    gain = lambda g: g.reshape(1, D_MODEL)
    w16 = lambda w: w.astype(BF16)

    h = x.reshape(tokens, D_MODEL)
    for i in range(depth):
        h = _ffn(h, gain(ffn1_pre_g[i]), w16(ffn1_w_gate[i]), w16(ffn1_w_up[i]), w16(ffn1_w_down[i]),
                 gain(ffn1_post_g[i]))
        qkv_na, *folded = _qkv(h, gain(mix_pre_g[i]), w16(w_qkv[i]), cos_full, sin_signed, batch, seq)
        qkv_na = qkv_na.reshape(batch, seq, 3 * W_NA)
        o_na = _na(qkv_na, _na_bias_table(na_rpb[i], rows), batch, seq).reshape(tokens, W_NA)
        branches = [_dilated_branch(f, window, dil, batch, seq) for f, (window, dil) in zip(folded, DIL_PAIRS)]
        lse = jnp.concatenate([b[1] for b in branches], axis=-1)
        h = _mix_out(o_na, branches[0][0], branches[1][0], branches[2][0], lse, gain(out_g[i]), w16(w_o[i]), h,
                     gain(mix_post_g[i]))
        h = _ffn(h, gain(ffn2_pre_g[i]), w16(ffn2_w_gate[i]), w16(ffn2_w_up[i]), w16(ffn2_w_down[i]),
                 gain(ffn2_post_g[i]))
        h = _ple(h, p[i].reshape(tokens, PLE_DIM), gain(ple_pre_g[i]), w16(w_ple_gate[i]), w16(w_ple_proj[i]),
                 gain(ple_post_g[i]))
    return h.reshape(batch, seq, D_MODEL)
```

```python
import functools

import jax
import jax.numpy as jnp
import numpy as np
from jax import lax
from jax.experimental import pallas as pl
from jax.experimental.pallas import tpu as pltpu

D_MODEL = 2048
D_FF = 5632
HEAD_DIM = 128
N_HEADS = 16
N_HEADS_NA = 4
N_HEADS_DIL = 12
W_NA = N_HEADS_NA * HEAD_DIM
W_DIL = N_HEADS_DIL * HEAD_DIM
GRID_W = 64
NA_ROWS = 8
NA_COLS = 16
DIL_PAIRS = ((128, 1), (512, 4), (2048, 16))
PLE_DIM = 256
ROPE_THETA = 10000.0
EPS = 1e-6
NEG = -1e30
SCALE = HEAD_DIM ** -0.5

F32 = jnp.float32
BF16 = jnp.bfloat16

VMEM_LIMIT_BYTES = 56 * 1024 * 1024

FFN_TM = 512
FFN_TF = 512
QKV_TM = 1024
QKV_TN = 768
NA_QROWS = 8
NA_KROWS = 16
DIL_TQ = 128
DIL_GROUP = 8
DIL_HALF = 64
MIX_TM = 256
PLE_TM = 512


def _rms(x, g):
    return x * lax.rsqrt(jnp.mean(x * x, axis=-1, keepdims=True) + EPS) * g


def _params(*sem):
    return pltpu.CompilerParams(dimension_semantics=sem, vmem_limit_bytes=VMEM_LIMIT_BYTES)


def _ffn_kernel(x_ref, pre_g_ref, wg_ref, wu_ref, wd_ref, post_g_ref, o_ref, u_ref, acc_ref):
    j = pl.program_id(1)

    @pl.when(j == 0)
    def _():
        u_ref[...] = _rms(x_ref[...], pre_g_ref[...]).astype(BF16)
        acc_ref[...] = jnp.zeros_like(acc_ref)

    u = u_ref[...]
    g = jnp.dot(u, wg_ref[...], preferred_element_type=F32)
    v = jnp.dot(u, wu_ref[...], preferred_element_type=F32)
    mid = (g * jax.nn.sigmoid(g) * v).astype(BF16)
    acc_ref[...] += jnp.dot(mid, wd_ref[...], preferred_element_type=F32)

    @pl.when(j == pl.num_programs(1) - 1)
    def _():
        o_ref[...] = x_ref[...] + 0.5 * _rms(acc_ref[...], post_g_ref[...])


def _ffn(x, pre_g, w_gate, w_up, w_down, post_g):
    t = x.shape[0]
    return pl.pallas_call(
        _ffn_kernel,
        name="ffn",
        grid=(t // FFN_TM, D_FF // FFN_TF),
        in_specs=[
            pl.BlockSpec((FFN_TM, D_MODEL), lambda i, j: (i, 0)),
            pl.BlockSpec((1, D_MODEL), lambda i, j: (0, 0)),
            pl.BlockSpec((D_MODEL, FFN_TF), lambda i, j: (0, j)),
            pl.BlockSpec((D_MODEL, FFN_TF), lambda i, j: (0, j)),
            pl.BlockSpec((FFN_TF, D_MODEL), lambda i, j: (j, 0)),
            pl.BlockSpec((1, D_MODEL), lambda i, j: (0, 0)),
        ],
        out_specs=pl.BlockSpec((FFN_TM, D_MODEL), lambda i, j: (i, 0)),
        out_shape=jax.ShapeDtypeStruct((t, D_MODEL), F32),
        scratch_shapes=[pltpu.VMEM((FFN_TM, D_MODEL), BF16), pltpu.VMEM((FFN_TM, D_MODEL), F32)],
        compiler_params=_params("parallel", "arbitrary"),
    )(x, pre_g, w_gate, w_up, w_down, post_g)


_QKV_HEADS_PER_STEP = QKV_TN // HEAD_DIM
_ROPE_FIRST_STEP = 3 * W_NA // QKV_TN
_ROPE_END_STEP = (3 * W_NA + 2 * W_DIL) // QKV_TN


def _qkv_kernel(x_ref, g_ref, w_ref, cos_ref, sin_ref, na_ref, *rest):
    fold_refs, (u_ref, y_ref) = rest[:len(DIL_PAIRS)], rest[len(DIL_PAIRS):]
    j = pl.program_id(1)

    @pl.when(j == 0)
    def _():
        u_ref[...] = _rms(x_ref[...], g_ref[...]).astype(BF16)

    y = jnp.dot(u_ref[...], w_ref[...], preferred_element_type=F32)
    rope = jnp.logical_and(j >= _ROPE_FIRST_STEP, j < _ROPE_END_STEP)

    @pl.when(j < _ROPE_FIRST_STEP)
    def _():
        na_ref[...] = y.astype(BF16)

    @pl.when(rope)
    def _():
        cos = cos_ref[...]
        sin = sin_ref[...]
        for h in range(_QKV_HEADS_PER_STEP):
            t = y[:, h * HEAD_DIM:(h + 1) * HEAD_DIM]
            y_ref[h] = t * cos + pltpu.roll(t, HEAD_DIM // 2, axis=1) * sin

    @pl.when(j >= _ROPE_END_STEP)
    def _():
        for h in range(_QKV_HEADS_PER_STEP):
            y_ref[h] = y[:, h * HEAD_DIM:(h + 1) * HEAD_DIM]

    @pl.when(j >= _ROPE_FIRST_STEP)
    def _():
        for h in range(_QKV_HEADS_PER_STEP):
            cols = slice(h * HEAD_DIM, (h + 1) * HEAD_DIM)
            for (_, dil), f_ref in zip(DIL_PAIRS, fold_refs):
                if dil == 1:
                    f_ref[0, 0, :, cols] = y_ref[h].astype(BF16)
                else:
                    for r in range(dil):
                        f_ref[0, r, :, cols] = y_ref[h, pl.ds(r, QKV_TM // dil, stride=dil), :].astype(BF16)


def _qkv(x, g, w, cos_full, sin_signed, batch, seq):
    t = x.shape[0]
    tiles_per_seq = seq // QKV_TM
    na_steps = _ROPE_FIRST_STEP

    def fold_spec(dil):
        return pl.BlockSpec(
            (1, dil, QKV_TM // dil, QKV_TN),
            lambda i, j: (i // tiles_per_seq, 0, i % tiles_per_seq, jnp.maximum(j - na_steps, 0)))

    return pl.pallas_call(
        _qkv_kernel,
        name="qkv",
        grid=(t // QKV_TM, 3 * D_MODEL // QKV_TN),
        in_specs=[
            pl.BlockSpec((QKV_TM, D_MODEL), lambda i, j: (i, 0)),
            pl.BlockSpec((1, D_MODEL), lambda i, j: (0, 0)),
            pl.BlockSpec((D_MODEL, QKV_TN), lambda i, j: (0, j)),
            pl.BlockSpec((QKV_TM, HEAD_DIM), lambda i, j: (i % tiles_per_seq, 0)),
            pl.BlockSpec((QKV_TM, HEAD_DIM), lambda i, j: (i % tiles_per_seq, 0)),
        ],
        out_specs=[pl.BlockSpec((QKV_TM, QKV_TN), lambda i, j: (i, jnp.minimum(j, na_steps - 1)))]
        + [fold_spec(dil) for (_, dil) in DIL_PAIRS],
        out_shape=[jax.ShapeDtypeStruct((t, 3 * W_NA), BF16)]
        + [jax.ShapeDtypeStruct((batch, dil, seq // dil, 3 * W_DIL), BF16) for (_, dil) in DIL_PAIRS],
        scratch_shapes=[pltpu.VMEM((QKV_TM, D_MODEL), BF16),
                        pltpu.VMEM((_QKV_HEADS_PER_STEP, QKV_TM, HEAD_DIM), F32)],
        compiler_params=_params("parallel", "arbitrary"),
    )(x, g, w, cos_full, sin_signed)


def _rope_tables(seq):
    inv = jnp.float32(ROPE_THETA) ** (-jnp.arange(0, HEAD_DIM, 2, dtype=F32) / HEAD_DIM)
    ang = jnp.arange(seq, dtype=F32)[:, None] * inv[None, :]
    cos, sin = jnp.cos(ang), jnp.sin(ang)
    return jnp.concatenate([cos, cos], axis=-1), jnp.concatenate([-sin, sin], axis=-1)


def _na_tile_key_row_start(i, rows):
    return jnp.clip(i * NA_QROWS - NA_ROWS // 2, 0, rows - NA_KROWS)


def _na_kernel(q_ref, k_ref, v_ref, bias_ref, o_ref, *, rows):
    i = pl.program_id(2)
    start = pl.multiple_of(_na_tile_key_row_start(i, rows) * GRID_W, GRID_W)
    nk = NA_KROWS * GRID_W
    q = q_ref[0]
    k = k_ref[0, pl.ds(start, nk), :]
    v = v_ref[0, pl.ds(start, nk), :]
    s = lax.dot_general(q, k, (((1,), (1,)), ((), ())), preferred_element_type=F32) * SCALE
    s = s + bias_ref[0, 0]
    m = jnp.max(s, axis=-1, keepdims=True)
    e = jnp.exp(s - m)
    den = jnp.sum(e, axis=-1, keepdims=True)
    o = jnp.dot(e.astype(BF16), v, preferred_element_type=F32)
    o_ref[0] = o / den


def _na_bias_table(rpb, rows):
    n_tiles = rows // NA_QROWS
    dr_idx = np.zeros((3, NA_QROWS, NA_KROWS), np.int32)
    row_ok = np.zeros((3, NA_QROWS, NA_KROWS), bool)
    for cls, tile in enumerate((0, 1, n_tiles - 1)):
        ks = int(np.clip(tile * NA_QROWS - NA_ROWS // 2, 0, rows - NA_KROWS))
        r = tile * NA_QROWS + np.arange(NA_QROWS)
        rs = np.clip(r - NA_ROWS // 2, 0, rows - NA_ROWS)
        kr = ks + np.arange(NA_KROWS)
        row_ok[cls] = (kr[None, :] >= rs[:, None]) & (kr[None, :] < rs[:, None] + NA_ROWS)
        dr_idx[cls] = np.clip(kr[None, :] - r[:, None] + NA_ROWS - 1, 0, 2 * NA_ROWS - 2)
    c = np.arange(GRID_W)
    qs = np.clip(c - NA_COLS // 2, 0, GRID_W - NA_COLS)
    col_ok = (c[None, :] >= qs[:, None]) & (c[None, :] < qs[:, None] + NA_COLS)
    n_dr, n_dc = 2 * NA_ROWS - 1, 2 * NA_COLS - 1
    period = 2 * GRID_W
    v = jnp.concatenate([rpb[..., NA_COLS - 1:], jnp.zeros((N_HEADS_NA, n_dr, period - n_dc), F32),
                         rpb[..., :NA_COLS - 1]], axis=-1).astype(F32)
    toe = jnp.tile(v, (1, 1, GRID_W))[..., :GRID_W * (period - 1)]
    toe = toe.reshape(N_HEADS_NA, n_dr, GRID_W, period - 1)[..., :GRID_W]
    slabs = jnp.where(col_ok[None, None], toe, NEG)
    slabs = jnp.concatenate([slabs, jnp.full((N_HEADS_NA, 1, GRID_W, GRID_W), NEG, F32)], axis=1)
    slab_idx = np.where(row_ok, dr_idx, n_dr).reshape(-1)
    bias = jnp.take(slabs, slab_idx, axis=1)
    bias = bias.reshape(N_HEADS_NA, 3, NA_QROWS, NA_KROWS, GRID_W, GRID_W)
    bias = bias.transpose(1, 0, 2, 4, 3, 5)
    return bias.reshape(3, N_HEADS_NA, NA_QROWS * GRID_W, NA_KROWS * GRID_W)


def _na(qkv, bias, batch, seq):
    rows = seq // GRID_W
    n_tiles = rows // NA_QROWS
    tq = NA_QROWS * GRID_W

    def bias_map(b, h, i):
        return (jnp.where(i == 0, 0, jnp.where(i == n_tiles - 1, 2, 1)), h, 0, 0)

    return pl.pallas_call(
        functools.partial(_na_kernel, rows=rows),
        name="na",
        grid=(batch, N_HEADS_NA, n_tiles),
        in_specs=[
            pl.BlockSpec((1, tq, HEAD_DIM), lambda b, h, i: (b, i, h)),
            pl.BlockSpec((1, seq, HEAD_DIM), lambda b, h, i: (b, 0, N_HEADS_NA + h)),
            pl.BlockSpec((1, seq, HEAD_DIM), lambda b, h, i: (b, 0, 2 * N_HEADS_NA + h)),
            pl.BlockSpec((1, 1, tq, NA_KROWS * GRID_W), bias_map),
        ],
        out_specs=pl.BlockSpec((1, tq, HEAD_DIM), lambda b, h, i: (b, i, h)),
        out_shape=jax.ShapeDtypeStruct((batch, seq, W_NA), F32),
        compiler_params=_params("parallel", "parallel", "arbitrary"),
    )(qkv, qkv, qkv, bias)


_DIL_WIN = DIL_TQ + 2 * DIL_HALF


def _dil_kernel(q_ref, k_ref, v_ref, mask_ref, o_ref, lse_ref, *, dil, length):
    n_i = length // DIL_TQ

    def group(g, carry):
        tiles = []
        for n in range(DIL_GROUP):
            t = g * DIL_GROUP + n
            r = t // n_i
            i = t % n_i
            q0 = pl.multiple_of(i * DIL_TQ, DIL_TQ)
            k0 = pl.multiple_of(jnp.clip(q0 - DIL_HALF, 0, length - _DIL_WIN), DIL_HALF)
            edge = jnp.where(i == 0, 0, jnp.where(i == n_i - 1, 2, 1))
            tiles.append((r, q0, k0, edge))
        scores = []
        for r, q0, k0, edge in tiles:
            q = q_ref[0, r, pl.ds(q0, DIL_TQ), :]
            k = k_ref[0, r, pl.ds(k0, _DIL_WIN), :]
            s = lax.dot_general(q, k, (((1,), (1,)), ((), ())), preferred_element_type=F32) * SCALE
            scores.append(s + mask_ref[edge])
        stats = []
        for s in scores:
            m = jnp.max(s, axis=-1, keepdims=True)
            e = jnp.exp(s - m)
            stats.append((m, e, jnp.sum(e, axis=-1, keepdims=True)))
        outs = []
        for (r, q0, k0, edge), (m, e, den) in zip(tiles, stats):
            v = v_ref[0, r, pl.ds(k0, _DIL_WIN), :]
            outs.append(jnp.dot(e.astype(BF16), v, preferred_element_type=F32) / den)
        for (r, q0, k0, edge), (m, e, den), o in zip(tiles, stats, outs):
            rows = pl.ds(q0, DIL_TQ) if dil == 1 else pl.ds(r + dil * q0, DIL_TQ, stride=dil)
            o_ref[0, rows, :] = o
            lse_ref[0, rows, :] = jnp.broadcast_to(m + jnp.log(den), (DIL_TQ, HEAD_DIM))
        return carry

    lax.fori_loop(0, dil * n_i // DIL_GROUP, group, 0)


def _dil_mask_table(length):
    n_i = length // DIL_TQ
    qq = np.arange(DIL_TQ)[:, None]
    kk = np.arange(_DIL_WIN)[None, :]
    out = []
    for i in (0, 1, n_i - 1):
        k0 = int(np.clip(i * DIL_TQ - DIL_HALF, 0, length - _DIL_WIN))
        out.append(np.where(np.abs(k0 + kk - (i * DIL_TQ + qq)) <= DIL_HALF, 0.0, NEG))
    return jnp.asarray(np.stack(out), F32)


def _dilated_branch(folded, window, dil, batch, seq):
    assert window // (2 * dil) == DIL_HALF
    length = seq // dil
    assert length % DIL_TQ == 0 and length >= _DIL_WIN
    head_block = lambda first: pl.BlockSpec((1, dil, length, HEAD_DIM), lambda b, h: (b, 0, 0, first + h))
    o, lse = pl.pallas_call(
        functools.partial(_dil_kernel, dil=dil, length=length),
        name=f"dilated_{dil}",
        grid=(batch, N_HEADS_DIL),
        in_specs=[
            head_block(0), head_block(N_HEADS_DIL), head_block(2 * N_HEADS_DIL),
            pl.BlockSpec((3, DIL_TQ, _DIL_WIN), lambda b, h: (0, 0, 0)),
        ],
        out_specs=[pl.BlockSpec((1, seq, HEAD_DIM), lambda b, h: (b, 0, h))] * 2,
        out_shape=[jax.ShapeDtypeStruct((batch, seq, W_DIL), F32)] * 2,
        compiler_params=_params("parallel", "parallel"),
    )(folded, folded, folded, _dil_mask_table(length))
    return o.reshape(batch * seq, W_DIL), lse.reshape(batch * seq, W_DIL)


def _mix_out_kernel(ona_ref, o1_ref, o2_ref, o3_ref, l1_ref, l2_ref, l3_ref, outg_ref, wo_ref, h_ref, postg_ref,
                    o_ref, buf_ref):
    for h in range(N_HEADS):
        cols = slice(h * HEAD_DIM, (h + 1) * HEAD_DIM)
        if h < N_HEADS_NA:
            o = ona_ref[:, cols]
        else:
            dcols = slice((h - N_HEADS_NA) * HEAD_DIM, (h - N_HEADS_NA + 1) * HEAD_DIM)
            l1, l2, l3 = l1_ref[:, dcols], l2_ref[:, dcols], l3_ref[:, dcols]
            mx = jnp.maximum(jnp.maximum(l1, l2), l3)
            e1, e2, e3 = jnp.exp(l1 - mx), jnp.exp(l2 - mx), jnp.exp(l3 - mx)
            o = (e1 * o1_ref[:, dcols] + e2 * o2_ref[:, dcols] + e3 * o3_ref[:, dcols]) / (e1 + e2 + e3)
        buf_ref[:, cols] = _rms(o, outg_ref[:, cols]).astype(BF16)
    m = jnp.dot(buf_ref[...], wo_ref[...], preferred_element_type=F32)
    o_ref[...] = h_ref[...] + _rms(m, postg_ref[...])


def _mix_out(o_na, branches, out_g, w_o, h, post_g):
    t = h.shape[0]
    row = lambda width: pl.BlockSpec((MIX_TM, width), lambda i: (i, 0))
    const = lambda shape: pl.BlockSpec(shape, lambda i: (0, 0))
    return pl.pallas_call(
        _mix_out_kernel,
        name="mix_out",
        grid=(t // MIX_TM,),
        in_specs=[row(W_NA)] + [row(W_DIL)] * (2 * len(DIL_PAIRS))
        + [const((1, D_MODEL)), const((D_MODEL, D_MODEL)), row(D_MODEL), const((1, D_MODEL))],
        out_specs=row(D_MODEL),
        out_shape=jax.ShapeDtypeStruct((t, D_MODEL), F32),
        scratch_shapes=[pltpu.VMEM((MIX_TM, D_MODEL), BF16)],
        compiler_params=_params("parallel"),
    )(o_na, *[b[0] for b in branches], *[b[1] for b in branches], out_g, w_o, h, post_g)


def _ple_kernel(h_ref, p_ref, preg_ref, wg_ref, wp_ref, postg_ref, o_ref):
    h = h_ref[...]
    u = _rms(h, preg_ref[...]).astype(BF16)
    gate = jax.nn.sigmoid(jnp.dot(u, wg_ref[...], preferred_element_type=F32))
    emb = jnp.dot(p_ref[...].astype(BF16), wp_ref[...], preferred_element_type=F32)
    o_ref[...] = h + _rms(gate * emb, postg_ref[...])


def _ple(h, p, pre_g, w_gate, w_proj, post_g):
    t = h.shape[0]
    row = lambda width: pl.BlockSpec((PLE_TM, width), lambda i: (i, 0))
    const = lambda shape: pl.BlockSpec(shape, lambda i: (0, 0))
    return pl.pallas_call(
        _ple_kernel,
        name="ple",
        grid=(t // PLE_TM,),
        in_specs=[row(D_MODEL), row(PLE_DIM), const((1, D_MODEL)), const((D_MODEL, D_MODEL)),
                  const((PLE_DIM, D_MODEL)), const((1, D_MODEL))],
        out_specs=row(D_MODEL),
        out_shape=jax.ShapeDtypeStruct((t, D_MODEL), F32),
        compiler_params=_params("parallel"),
    )(h, p, pre_g, w_gate, w_proj, post_g)


def kernel(x, p, ffn1_pre_g, ffn1_w_gate, ffn1_w_up, ffn1_w_down, ffn1_post_g, mix_pre_g, w_qkv, na_rpb, out_g, w_o, mix_post_g, ffn2_pre_g, ffn2_w_gate, ffn2_w_up, ffn2_w_down, ffn2_post_g, ple_pre_g, w_ple_gate, w_ple_proj, ple_post_g):
    batch, seq, d_model = x.shape
    depth = p.shape[0]
    assert d_model == D_MODEL and seq % (GRID_W * NA_KROWS) == 0
    tokens = batch * seq
    rows = seq // GRID_W
    cos_full, sin_signed = _rope_tables(seq)
    gain = lambda g: g.reshape(1, D_MODEL)
    w16 = lambda w: w.astype(BF16)

    h = x.reshape(tokens, D_MODEL)
    for i in range(depth):
        h = _ffn(h, gain(ffn1_pre_g[i]), w16(ffn1_w_gate[i]), w16(ffn1_w_up[i]), w16(ffn1_w_down[i]),
                 gain(ffn1_post_g[i]))
        qkv_na, *folded = _qkv(h, gain(mix_pre_g[i]), w16(w_qkv[i]), cos_full, sin_signed, batch, seq)
        qkv_na = qkv_na.reshape(batch, seq, 3 * W_NA)
        o_na = _na(qkv_na, _na_bias_table(na_rpb[i], rows), batch, seq).reshape(tokens, W_NA)
        branches = [_dilated_branch(f, window, dil, batch, seq) for f, (window, dil) in zip(folded, DIL_PAIRS)]
        h = _mix_out(o_na, branches, gain(out_g[i]), w16(w_o[i]), h, gain(mix_post_g[i]))
        h = _ffn(h, gain(ffn2_pre_g[i]), w16(ffn2_w_gate[i]), w16(ffn2_w_up[i]), w16(ffn2_w_down[i]),
                 gain(ffn2_post_g[i]))
        h = _ple(h, p[i].reshape(tokens, PLE_DIM), gain(ple_pre_g[i]), w16(w_ple_gate[i]), w16(w_ple_proj[i]),
                 gain(ple_post_g[i]))
    return h.reshape(batch, seq, D_MODEL)
```

```python
import functools
import math

import jax
import jax.numpy as jnp
import numpy as np
from jax import lax
from jax.experimental import pallas as pl
from jax.experimental.pallas import tpu as pltpu

D_MODEL = 2048
D_FF = 5632
HEAD_DIM = 128
N_HEADS = 16
N_HEADS_NA = 4
N_HEADS_DIL = 12
W_NA = N_HEADS_NA * HEAD_DIM
W_DIL = N_HEADS_DIL * HEAD_DIM
GRID_W = 64
NA_ROWS = 8
NA_COLS = 16
DIL_PAIRS = ((128, 1), (512, 4), (2048, 16))
PLE_DIM = 256
ROPE_THETA = 10000.0
EPS = 1e-6
NEG = -1e30
SCALE = HEAD_DIM ** -0.5

F32 = jnp.float32
BF16 = jnp.bfloat16

VMEM_LIMIT_BYTES = 56 * 1024 * 1024

FFN_TM = 512
FFN_TF = 512
QKV_TM = 1024
QKV_TN = 768
FOLD = 4
NA_QROWS = 8
NA_KROWS = 16
DIL_TQ = 128
DIL_GROUP = 8
DIL_HALF = 64
MIX_TM = 256
PLE_TM = 512


def _rms(x, g):
    return x * lax.rsqrt(jnp.mean(x * x, axis=-1, keepdims=True) + EPS) * g


def _params(*sem):
    return pltpu.CompilerParams(dimension_semantics=sem, vmem_limit_bytes=VMEM_LIMIT_BYTES)


def _ffn_kernel(x_ref, pre_g_ref, wg_ref, wu_ref, wd_ref, post_g_ref, o_ref, u_ref):
    j = pl.program_id(1)

    @pl.when(j == 0)
    def _():
        u_ref[...] = _rms(x_ref[...], pre_g_ref[...]).astype(BF16)
        o_ref[...] = jnp.zeros_like(o_ref)

    u = u_ref[...]
    g = jnp.dot(u, wg_ref[...], preferred_element_type=F32)
    v = jnp.dot(u, wu_ref[...], preferred_element_type=F32)
    mid = (g * jax.nn.sigmoid(g) * v).astype(BF16)
    o_ref[...] += jnp.dot(mid, wd_ref[...], preferred_element_type=F32)

    @pl.when(j == pl.num_programs(1) - 1)
    def _():
        o_ref[...] = x_ref[...] + 0.5 * _rms(o_ref[...], post_g_ref[...])


def _ffn(x, pre_g, w_gate, w_up, w_down, post_g):
    t = x.shape[0]
    return pl.pallas_call(
        _ffn_kernel,
        name="ffn",
        grid=(t // FFN_TM, D_FF // FFN_TF),
        in_specs=[
            pl.BlockSpec((FFN_TM, D_MODEL), lambda i, j: (i, 0)),
            pl.BlockSpec((1, D_MODEL), lambda i, j: (0, 0)),
            pl.BlockSpec((D_MODEL, FFN_TF), lambda i, j: (0, j)),
            pl.BlockSpec((D_MODEL, FFN_TF), lambda i, j: (0, j)),
            pl.BlockSpec((FFN_TF, D_MODEL), lambda i, j: (j, 0)),
            pl.BlockSpec((1, D_MODEL), lambda i, j: (0, 0)),
        ],
        out_specs=pl.BlockSpec((FFN_TM, D_MODEL), lambda i, j: (i, 0)),
        out_shape=jax.ShapeDtypeStruct((t, D_MODEL), F32),
        scratch_shapes=[pltpu.VMEM((FFN_TM, D_MODEL), BF16)],
        compiler_params=_params("parallel", "arbitrary"),
    )(x, pre_g, w_gate, w_up, w_down, post_g)


Q_SCALE_NA = SCALE
Q_SCALE_DIL = SCALE * math.log2(math.e)
_DIL_COL_STEPS = 3 * W_DIL // QKV_TN
_DIL_STEPS_PER_PART = W_DIL // QKV_TN
_DOT_N = 2 * HEAD_DIM


def _qkv_na_kernel(x_ref, g_ref, w_ref, colscale_ref, o_ref, u_ref):
    @pl.when(pl.program_id(1) == 0)
    def _():
        u_ref[...] = _rms(x_ref[...], g_ref[...]).astype(BF16)

    y = jnp.dot(u_ref[...], w_ref[...], preferred_element_type=F32)
    o_ref[...] = (y * colscale_ref[...]).astype(BF16)


def _qkv_na(x, g, w):
    t = x.shape[0]
    colscale = jnp.concatenate([jnp.full((1, W_NA), Q_SCALE_NA, F32), jnp.ones((1, 2 * W_NA), F32)], axis=-1)
    return pl.pallas_call(
        _qkv_na_kernel,
        name="qkv_na",
        grid=(t // QKV_TM, 3 * W_NA // QKV_TN),
        in_specs=[
            pl.BlockSpec((QKV_TM, D_MODEL), lambda i, j: (i, 0)),
            pl.BlockSpec((1, D_MODEL), lambda i, j: (0, 0)),
            pl.BlockSpec((D_MODEL, QKV_TN), lambda i, j: (0, j)),
            pl.BlockSpec((1, QKV_TN), lambda i, j: (0, j)),
        ],
        out_specs=pl.BlockSpec((QKV_TM, QKV_TN), lambda i, j: (i, j)),
        out_shape=jax.ShapeDtypeStruct((t, 3 * W_NA), BF16),
        scratch_shapes=[pltpu.VMEM((QKV_TM, D_MODEL), BF16)],
        compiler_params=_params("parallel", "arbitrary"),
    )(x, g, w, colscale)


def _qkv_dil_kernel(x_ref, g_ref, w_ref, cos_ref, sin_ref, fa_ref, fb_ref, u_ref, y_ref, y4_ref):
    j = pl.program_id(1)

    @pl.when(j == 0)
    def _():
        u_ref[...] = _rms(x_ref[...], g_ref[...]).astype(BF16)

    rotary = j < 2 * _DIL_STEPS_PER_PART
    scale = jnp.where(j < _DIL_STEPS_PER_PART, Q_SCALE_DIL, 1.0).astype(F32)
    cos = cos_ref[...]
    sin = sin_ref[...]
    for pair in range(QKV_TN // _DOT_N):
        y2 = jnp.dot(u_ref[...], w_ref[:, pair * _DOT_N:(pair + 1) * _DOT_N], preferred_element_type=F32)
        for half in range(_DOT_N // HEAD_DIM):
            h = pair * (_DOT_N // HEAD_DIM) + half
            cols = slice(h * HEAD_DIM, (h + 1) * HEAD_DIM)
            y = y2[:, half * HEAD_DIM:(half + 1) * HEAD_DIM]
            y_ref[h] = jnp.where(rotary, y * cos + pltpu.roll(y, HEAD_DIM // 2, axis=1) * sin, y) * scale
            for r in range(FOLD):
                y4 = y_ref[h, pl.ds(r, QKV_TM // FOLD, stride=FOLD), :]
                fa_ref[0, r, :, cols] = y4.astype(BF16)
                y4_ref[h, r] = y4
                for c in range(FOLD):
                    fb_ref[0, r + FOLD * c, :, cols] = (
                        y4_ref[h, r, pl.ds(c, QKV_TM // FOLD ** 2, stride=FOLD), :].astype(BF16))


def _qkv_dil(x, g, w, cos_full, sin_signed, batch, seq):
    t = x.shape[0]
    tiles_per_seq = seq // QKV_TM

    def fold_spec(dil):
        return pl.BlockSpec((1, dil, QKV_TM // dil, QKV_TN),
                            lambda i, j: (i // tiles_per_seq, 0, i % tiles_per_seq, j))

    return pl.pallas_call(
        _qkv_dil_kernel,
        name="qkv_dil",
        grid=(t // QKV_TM, _DIL_COL_STEPS),
        in_specs=[
            pl.BlockSpec((QKV_TM, D_MODEL), lambda i, j: (i, 0)),
            pl.BlockSpec((1, D_MODEL), lambda i, j: (0, 0)),
            pl.BlockSpec((D_MODEL, QKV_TN), lambda i, j: (0, j)),
            pl.BlockSpec((QKV_TM, HEAD_DIM), lambda i, j: (i % tiles_per_seq, 0)),
            pl.BlockSpec((QKV_TM, HEAD_DIM), lambda i, j: (i % tiles_per_seq, 0)),
        ],
        out_specs=[fold_spec(f) for f in (FOLD, FOLD ** 2)],
        out_shape=[jax.ShapeDtypeStruct((batch, f, seq // f, 3 * W_DIL), BF16) for f in (FOLD, FOLD ** 2)],
        scratch_shapes=[pltpu.VMEM((QKV_TM, D_MODEL), BF16),
                        pltpu.VMEM((QKV_TN // HEAD_DIM, QKV_TM, HEAD_DIM), F32),
                        pltpu.VMEM((QKV_TN // HEAD_DIM, FOLD, QKV_TM // FOLD, HEAD_DIM), F32)],
        compiler_params=_params("parallel", "arbitrary"),
    )(x, g, w, cos_full, sin_signed)


def _rope_tables(seq):
    inv = jnp.float32(ROPE_THETA) ** (-jnp.arange(0, HEAD_DIM, 2, dtype=F32) / HEAD_DIM)
    ang = jnp.arange(seq, dtype=F32)[:, None] * inv[None, :]
    cos, sin = jnp.cos(ang), jnp.sin(ang)
    return jnp.concatenate([cos, cos], axis=-1), jnp.concatenate([-sin, sin], axis=-1)


def _na_tile_key_row_start(i, rows):
    return jnp.clip(i * NA_QROWS - NA_ROWS // 2, 0, rows - NA_KROWS)


def _na_kernel(q_ref, k_ref, v_ref, bias_ref, o_ref, *, rows):
    i = pl.program_id(2)
    start = pl.multiple_of(_na_tile_key_row_start(i, rows) * GRID_W, GRID_W)
    nk = NA_KROWS * GRID_W
    q = q_ref[0]
    k = k_ref[0, pl.ds(start, nk), :]
    v = v_ref[0, pl.ds(start, nk), :]
    s = lax.dot_general(q, k, (((1,), (1,)), ((), ())), preferred_element_type=F32) + bias_ref[0, 0]
    m = jnp.max(s, axis=-1, keepdims=True)
    e = jnp.exp(s - m)
    den = jnp.sum(e, axis=-1, keepdims=True)
    o = jnp.dot(e.astype(BF16), v, preferred_element_type=F32)
    o_ref[0] = o / den


def _na_bias_table(rpb, rows):
    n_tiles = rows // NA_QROWS
    dr_idx = np.zeros((3, NA_QROWS, NA_KROWS), np.int32)
    row_ok = np.zeros((3, NA_QROWS, NA_KROWS), bool)
    for cls, tile in enumerate((0, 1, n_tiles - 1)):
        ks = int(np.clip(tile * NA_QROWS - NA_ROWS // 2, 0, rows - NA_KROWS))
        r = tile * NA_QROWS + np.arange(NA_QROWS)
        rs = np.clip(r - NA_ROWS // 2, 0, rows - NA_ROWS)
        kr = ks + np.arange(NA_KROWS)
        row_ok[cls] = (kr[None, :] >= rs[:, None]) & (kr[None, :] < rs[:, None] + NA_ROWS)
        dr_idx[cls] = np.clip(kr[None, :] - r[:, None] + NA_ROWS - 1, 0, 2 * NA_ROWS - 2)
    c = np.arange(GRID_W)
    qs = np.clip(c - NA_COLS // 2, 0, GRID_W - NA_COLS)
    col_ok = (c[None, :] >= qs[:, None]) & (c[None, :] < qs[:, None] + NA_COLS)
    n_dr, n_dc = 2 * NA_ROWS - 1, 2 * NA_COLS - 1
    period = 2 * GRID_W
    v = jnp.concatenate([rpb[..., NA_COLS - 1:], jnp.zeros((N_HEADS_NA, n_dr, period - n_dc), F32),
                         rpb[..., :NA_COLS - 1]], axis=-1).astype(F32)
    toe = jnp.tile(v, (1, 1, GRID_W))[..., :GRID_W * (period - 1)]
    toe = toe.reshape(N_HEADS_NA, n_dr, GRID_W, period - 1)[..., :GRID_W]
    slabs = jnp.where(col_ok[None, None], toe, NEG)
    slabs = jnp.concatenate([slabs, jnp.full((N_HEADS_NA, 1, GRID_W, GRID_W), NEG, F32)], axis=1)
    slab_idx = np.where(row_ok, dr_idx, n_dr).reshape(-1)
    bias = jnp.take(slabs, slab_idx, axis=1)
    bias = bias.reshape(N_HEADS_NA, 3, NA_QROWS, NA_KROWS, GRID_W, GRID_W)
    bias = bias.transpose(1, 0, 2, 4, 3, 5)
    return bias.reshape(3, N_HEADS_NA, NA_QROWS * GRID_W, NA_KROWS * GRID_W)


def _na(qkv, bias, batch, seq):
    rows = seq // GRID_W
    n_tiles = rows // NA_QROWS
    tq = NA_QROWS * GRID_W

    def bias_map(b, h, i):
        return (jnp.where(i == 0, 0, jnp.where(i == n_tiles - 1, 2, 1)), h, 0, 0)

    return pl.pallas_call(
        functools.partial(_na_kernel, rows=rows),
        name="na",
        grid=(batch, N_HEADS_NA, n_tiles),
        in_specs=[
            pl.BlockSpec((1, tq, HEAD_DIM), lambda b, h, i: (b, i, h)),
            pl.BlockSpec((1, seq, HEAD_DIM), lambda b, h, i: (b, 0, N_HEADS_NA + h)),
            pl.BlockSpec((1, seq, HEAD_DIM), lambda b, h, i: (b, 0, 2 * N_HEADS_NA + h)),
            pl.BlockSpec((1, 1, tq, NA_KROWS * GRID_W), bias_map),
        ],
        out_specs=pl.BlockSpec((1, tq, HEAD_DIM), lambda b, h, i: (b, i, h)),
        out_shape=jax.ShapeDtypeStruct((batch, seq, W_NA), F32),
        compiler_params=_params("parallel", "parallel", "arbitrary"),
    )(qkv, qkv, qkv, bias)


_DIL_WIN = DIL_TQ + 2 * DIL_HALF


def _dil_kernel(q_ref, k_ref, v_ref, mask_ref, o_ref, lse_ref, *, dil, sub, length):
    fold = dil * sub
    qn, kn = DIL_TQ // sub, _DIL_WIN // sub
    n_i = length // qn

    def gather(ref, p, start, size):
        parts = [ref[0, p + dil * c, pl.ds(start, size), :] for c in range(sub)]
        return parts[0] if sub == 1 else jnp.concatenate(parts, axis=0)

    def group(g, carry):
        tiles = []
        for n in range(DIL_GROUP):
            t = g * DIL_GROUP + n
            p = t // n_i
            i = t % n_i
            q0 = pl.multiple_of(i * qn, qn)
            k0 = pl.multiple_of(jnp.clip(q0 - DIL_HALF // sub, 0, length - kn), DIL_HALF // sub)
            edge = jnp.where(i == 0, 0, jnp.where(i == n_i - 1, 2, 1))
            tiles.append((p, q0, k0, edge))
        scores = []
        for p, q0, k0, edge in tiles:
            s = lax.dot_general(gather(q_ref, p, q0, qn), gather(k_ref, p, k0, kn), (((1,), (1,)), ((), ())),
                                preferred_element_type=F32)
            scores.append(s + mask_ref[edge])
        stats = []
        for s in scores:
            m = jnp.max(s, axis=-1, keepdims=True)
            e = jnp.exp2(s - m)
            stats.append((m, e, jnp.sum(e, axis=-1, keepdims=True)))
        outs = []
        for (p, q0, k0, edge), (m, e, den) in zip(tiles, stats):
            outs.append(jnp.dot(e.astype(BF16), gather(v_ref, p, k0, kn), preferred_element_type=F32) / den)
        for (p, q0, k0, edge), (m, e, den), o in zip(tiles, stats, outs):
            lse = jnp.broadcast_to((m + jnp.log2(den)) * math.log(2.0), (DIL_TQ, HEAD_DIM))
            for c in range(sub):
                rows = pl.ds(fold * q0 + p + dil * c, qn, stride=fold)
                o_ref[0, rows, :] = o[c * qn:(c + 1) * qn]
                lse_ref[0, rows, :] = lse[c * qn:(c + 1) * qn]
        return carry

    lax.fori_loop(0, dil * n_i // DIL_GROUP, group, 0)


def _dil_mask_table(sub, length):
    qn, kn = DIL_TQ // sub, _DIL_WIN // sub
    n_i = length // qn
    c = np.arange(sub)[:, None]
    out = []
    for i in (0, 1, n_i - 1):
        k0 = int(np.clip(i * qn - DIL_HALF // sub, 0, length - kn))
        qpos = ((i * qn + np.arange(qn))[None, :] * sub + c).reshape(-1)
        kpos = ((k0 + np.arange(kn))[None, :] * sub + c).reshape(-1)
        out.append(np.where(np.abs(kpos[None, :] - qpos[:, None]) <= DIL_HALF, 0.0, NEG))
    return jnp.asarray(np.stack(out), F32)


def _dilated_branch(folded, fold, window, dil, batch, seq):
    assert window // (2 * dil) == DIL_HALF and fold % dil == 0
    sub = fold // dil
    length = seq // fold
    assert DIL_TQ % sub == 0 and length % (DIL_TQ // sub) == 0 and length >= _DIL_WIN // sub
    assert (dil * length // (DIL_TQ // sub)) % DIL_GROUP == 0
    head_block = lambda first: pl.BlockSpec((1, fold, length, HEAD_DIM), lambda b, h: (b, 0, 0, first + h))
    o, lse = pl.pallas_call(
        functools.partial(_dil_kernel, dil=dil, sub=sub, length=length),
        name=f"dilated_{dil}",
        grid=(batch, N_HEADS_DIL),
        in_specs=[
            head_block(0), head_block(N_HEADS_DIL), head_block(2 * N_HEADS_DIL),
            pl.BlockSpec((3, DIL_TQ, _DIL_WIN), lambda b, h: (0, 0, 0)),
        ],
        out_specs=[pl.BlockSpec((1, seq, HEAD_DIM), lambda b, h: (b, 0, h))] * 2,
        out_shape=[jax.ShapeDtypeStruct((batch, seq, W_DIL), F32)] * 2,
        compiler_params=_params("parallel", "parallel"),
    )(folded, folded, folded, _dil_mask_table(sub, length))
    return o.reshape(batch * seq, W_DIL), lse.reshape(batch * seq, W_DIL)


def _mix_out_kernel(ona_ref, o1_ref, o2_ref, o3_ref, l1_ref, l2_ref, l3_ref, outg_ref, wo_ref, h_ref, postg_ref,
                    o_ref, buf_ref):
    for h in range(N_HEADS):
        cols = slice(h * HEAD_DIM, (h + 1) * HEAD_DIM)
        if h < N_HEADS_NA:
            o = ona_ref[:, cols]
        else:
            dcols = slice((h - N_HEADS_NA) * HEAD_DIM, (h - N_HEADS_NA + 1) * HEAD_DIM)
            l1, l2, l3 = l1_ref[:, dcols], l2_ref[:, dcols], l3_ref[:, dcols]
            mx = jnp.maximum(jnp.maximum(l1, l2), l3)
            e1, e2, e3 = jnp.exp(l1 - mx), jnp.exp(l2 - mx), jnp.exp(l3 - mx)
            o = (e1 * o1_ref[:, dcols] + e2 * o2_ref[:, dcols] + e3 * o3_ref[:, dcols]) / (e1 + e2 + e3)
        buf_ref[:, cols] = _rms(o, outg_ref[:, cols]).astype(BF16)
    m = jnp.dot(buf_ref[...], wo_ref[...], preferred_element_type=F32)
    o_ref[...] = h_ref[...] + _rms(m, postg_ref[...])


def _mix_out(o_na, branches, out_g, w_o, h, post_g):
    t = h.shape[0]
    row = lambda width: pl.BlockSpec((MIX_TM, width), lambda i: (i, 0))
    const = lambda shape: pl.BlockSpec(shape, lambda i: (0, 0))
    return pl.pallas_call(
        _mix_out_kernel,
        name="mix_out",
        grid=(t // MIX_TM,),
        in_specs=[row(W_NA)] + [row(W_DIL)] * (2 * len(DIL_PAIRS))
        + [const((1, D_MODEL)), const((D_MODEL, D_MODEL)), row(D_MODEL), const((1, D_MODEL))],
        out_specs=row(D_MODEL),
        out_shape=jax.ShapeDtypeStruct((t, D_MODEL), F32),
        scratch_shapes=[pltpu.VMEM((MIX_TM, D_MODEL), BF16)],
        compiler_params=_params("parallel"),
    )(o_na, *[b[0] for b in branches], *[b[1] for b in branches], out_g, w_o, h, post_g)


def _ple_kernel(h_ref, p_ref, preg_ref, wg_ref, wp_ref, postg_ref, o_ref):
    h = h_ref[...]
    u = _rms(h, preg_ref[...]).astype(BF16)
    gate = jax.nn.sigmoid(jnp.dot(u, wg_ref[...], preferred_element_type=F32))
    emb = jnp.dot(p_ref[...].astype(BF16), wp_ref[...], preferred_element_type=F32)
    o_ref[...] = h + _rms(gate * emb, postg_ref[...])


def _ple(h, p, pre_g, w_gate, w_proj, post_g):
    t = h.shape[0]
    row = lambda width: pl.BlockSpec((PLE_TM, width), lambda i: (i, 0))
    const = lambda shape: pl.BlockSpec(shape, lambda i: (0, 0))
    return pl.pallas_call(
        _ple_kernel,
        name="ple",
        grid=(t // PLE_TM,),
        in_specs=[row(D_MODEL), row(PLE_DIM), const((1, D_MODEL)), const((D_MODEL, D_MODEL)),
                  const((PLE_DIM, D_MODEL)), const((1, D_MODEL))],
        out_specs=row(D_MODEL),
        out_shape=jax.ShapeDtypeStruct((t, D_MODEL), F32),
        compiler_params=_params("parallel"),
    )(h, p, pre_g, w_gate, w_proj, post_g)


def kernel(x, p, ffn1_pre_g, ffn1_w_gate, ffn1_w_up, ffn1_w_down, ffn1_post_g, mix_pre_g, w_qkv, na_rpb, out_g, w_o, mix_post_g, ffn2_pre_g, ffn2_w_gate, ffn2_w_up, ffn2_w_down, ffn2_post_g, ple_pre_g, w_ple_gate, w_ple_proj, ple_post_g):
    batch, seq, d_model = x.shape
    depth = p.shape[0]
    assert d_model == D_MODEL and seq % (GRID_W * NA_KROWS) == 0
    tokens = batch * seq
    rows = seq // GRID_W
    cos_full, sin_signed = _rope_tables(seq)
    gain = lambda g: g.reshape(1, D_MODEL)
    w16 = lambda w: w.astype(BF16)

    h = x.reshape(tokens, D_MODEL)
    for i in range(depth):
        h = _ffn(h, gain(ffn1_pre_g[i]), w16(ffn1_w_gate[i]), w16(ffn1_w_up[i]), w16(ffn1_w_down[i]),
                 gain(ffn1_post_g[i]))
        qkv_na = _qkv_na(h, gain(mix_pre_g[i]), w16(w_qkv[i][:, :3 * W_NA])).reshape(batch, seq, 3 * W_NA)
        folded = dict(zip((FOLD, FOLD ** 2), _qkv_dil(h, gain(mix_pre_g[i]), w16(w_qkv[i][:, 3 * W_NA:]), cos_full,
                                                      sin_signed, batch, seq)))
        o_na = _na(qkv_na, _na_bias_table(na_rpb[i], rows), batch, seq).reshape(tokens, W_NA)
        branches = [_dilated_branch(folded[max(dil, FOLD)], max(dil, FOLD), window, dil, batch, seq)
                    for (window, dil) in DIL_PAIRS]
        h = _mix_out(o_na, branches, gain(out_g[i]), w16(w_o[i]), h, gain(mix_post_g[i]))
        h = _ffn(h, gain(ffn2_pre_g[i]), w16(ffn2_w_gate[i]), w16(ffn2_w_up[i]), w16(ffn2_w_down[i]),
                 gain(ffn2_post_g[i]))
        h = _ple(h, p[i].reshape(tokens, PLE_DIM), gain(ple_pre_g[i]), w16(w_ple_gate[i]), w16(w_ple_proj[i]),
                 gain(ple_post_g[i]))
    return h.reshape(batch, seq, D_MODEL)
```

```python
import functools
import math

import jax
import jax.numpy as jnp
import numpy as np
from jax import lax
from jax.experimental import pallas as pl
from jax.experimental.pallas import tpu as pltpu

D_MODEL = 2048
D_FF = 5632
HEAD_DIM = 128
N_HEADS = 16
N_HEADS_NA = 4
N_HEADS_DIL = 12
W_NA = N_HEADS_NA * HEAD_DIM
W_DIL = N_HEADS_DIL * HEAD_DIM
GRID_W = 64
NA_ROWS = 8
NA_COLS = 16
DIL_PAIRS = ((128, 1), (512, 4), (2048, 16))
PLE_DIM = 256
ROPE_THETA = 10000.0
EPS = 1e-6
NEG = -1e30
SCALE = HEAD_DIM ** -0.5

F32 = jnp.float32
BF16 = jnp.bfloat16

VMEM_LIMIT_BYTES = 56 * 1024 * 1024

FFN_TM = 512
FFN_TF = 512
QKV_TM = 1024
QKV_TN = 768
FOLD = 4
NA_QROWS = 8
NA_KROWS = 16
DIL_TQ = 128
DIL_GROUP = 8
DIL_HALF = 64
DIL_PARTS = 2
DIL_MERGE_ROWS = 256
MIX_TM = 512
PLE_TM = 512


def _rms(x, g):
    return x * lax.rsqrt(jnp.mean(x * x, axis=-1, keepdims=True) + EPS) * g


def _params(*sem):
    return pltpu.CompilerParams(dimension_semantics=sem, vmem_limit_bytes=VMEM_LIMIT_BYTES)


def _ffn_kernel(x_ref, pre_g_ref, wg_ref, wu_ref, wd_ref, post_g_ref, o_ref, u_ref):
    j = pl.program_id(1)

    @pl.when(j == 0)
    def _():
        u_ref[...] = _rms(x_ref[...], pre_g_ref[...]).astype(BF16)
        o_ref[...] = jnp.zeros_like(o_ref)

    u = u_ref[...]
    g = jnp.dot(u, wg_ref[...], preferred_element_type=F32)
    v = jnp.dot(u, wu_ref[...], preferred_element_type=F32)
    mid = (g * jax.nn.sigmoid(g) * v).astype(BF16)
    o_ref[...] += jnp.dot(mid, wd_ref[...], preferred_element_type=F32)

    @pl.when(j == pl.num_programs(1) - 1)
    def _():
        o_ref[...] = x_ref[...] + 0.5 * _rms(o_ref[...], post_g_ref[...])


def _ffn(x, pre_g, w_gate, w_up, w_down, post_g):
    t = x.shape[0]
    return pl.pallas_call(
        _ffn_kernel,
        name="ffn",
        grid=(t // FFN_TM, D_FF // FFN_TF),
        in_specs=[
            pl.BlockSpec((FFN_TM, D_MODEL), lambda i, j: (i, 0)),
            pl.BlockSpec((1, D_MODEL), lambda i, j: (0, 0)),
            pl.BlockSpec((D_MODEL, FFN_TF), lambda i, j: (0, j)),
            pl.BlockSpec((D_MODEL, FFN_TF), lambda i, j: (0, j)),
            pl.BlockSpec((FFN_TF, D_MODEL), lambda i, j: (j, 0)),
            pl.BlockSpec((1, D_MODEL), lambda i, j: (0, 0)),
        ],
        out_specs=pl.BlockSpec((FFN_TM, D_MODEL), lambda i, j: (i, 0)),
        out_shape=jax.ShapeDtypeStruct((t, D_MODEL), F32),
        scratch_shapes=[pltpu.VMEM((FFN_TM, D_MODEL), BF16)],
        compiler_params=_params("parallel", "arbitrary"),
    )(x, pre_g, w_gate, w_up, w_down, post_g)


Q_SCALE_NA = SCALE
Q_SCALE_DIL = SCALE * math.log2(math.e)
_DIL_COL_STEPS = 3 * W_DIL // QKV_TN
_DIL_STEPS_PER_PART = W_DIL // QKV_TN
_DOT_N = 2 * HEAD_DIM


def _qkv_na_kernel(x_ref, g_ref, w_ref, colscale_ref, o_ref, u_ref):
    @pl.when(pl.program_id(1) == 0)
    def _():
        u_ref[...] = _rms(x_ref[...], g_ref[...]).astype(BF16)

    y = jnp.dot(u_ref[...], w_ref[...], preferred_element_type=F32) * colscale_ref[...]
    for h in range(QKV_TN // HEAD_DIM):
        o_ref[0, h] = y[:, h * HEAD_DIM:(h + 1) * HEAD_DIM].astype(BF16)


def _qkv_na(x, g, w, batch, seq):
    t = x.shape[0]
    tiles_per_seq = seq // QKV_TM
    colscale = jnp.concatenate([jnp.full((1, W_NA), Q_SCALE_NA, F32), jnp.ones((1, 2 * W_NA), F32)], axis=-1)
    return pl.pallas_call(
        _qkv_na_kernel,
        name="qkv_na",
        grid=(t // QKV_TM, 3 * W_NA // QKV_TN),
        in_specs=[
            pl.BlockSpec((QKV_TM, D_MODEL), lambda i, j: (i, 0)),
            pl.BlockSpec((1, D_MODEL), lambda i, j: (0, 0)),
            pl.BlockSpec((D_MODEL, QKV_TN), lambda i, j: (0, j)),
            pl.BlockSpec((1, QKV_TN), lambda i, j: (0, j)),
        ],
        out_specs=pl.BlockSpec((1, QKV_TN // HEAD_DIM, QKV_TM, HEAD_DIM),
                               lambda i, j: (i // tiles_per_seq, j, i % tiles_per_seq, 0)),
        out_shape=jax.ShapeDtypeStruct((batch, 3 * N_HEADS_NA, seq, HEAD_DIM), BF16),
        scratch_shapes=[pltpu.VMEM((QKV_TM, D_MODEL), BF16)],
        compiler_params=_params("parallel", "arbitrary"),
    )(x, g, w, colscale)


def _qkv_dil_kernel(x_ref, g_ref, w_ref, cos_ref, sin_ref, fa_ref, fb_ref, u_ref, y_ref, y4_ref):
    j = pl.program_id(1)

    @pl.when(j == 0)
    def _():
        u_ref[...] = _rms(x_ref[...], g_ref[...]).astype(BF16)

    rotary = j < 2 * _DIL_STEPS_PER_PART
    scale = jnp.where(j < _DIL_STEPS_PER_PART, Q_SCALE_DIL, 1.0).astype(F32)
    cos = cos_ref[...]
    sin = sin_ref[...]
    for pair in range(QKV_TN // _DOT_N):
        y2 = jnp.dot(u_ref[...], w_ref[:, pair * _DOT_N:(pair + 1) * _DOT_N], preferred_element_type=F32)
        for half in range(_DOT_N // HEAD_DIM):
            h = pair * (_DOT_N // HEAD_DIM) + half
            y = y2[:, half * HEAD_DIM:(half + 1) * HEAD_DIM]
            y_ref[h] = jnp.where(rotary, y * cos + pltpu.roll(y, HEAD_DIM // 2, axis=1) * sin, y) * scale
            for r in range(FOLD):
                y4 = y_ref[h, pl.ds(r, QKV_TM // FOLD, stride=FOLD), :]
                fa_ref[0, h, r] = y4.astype(BF16)
                y4_ref[h, r] = y4
                for c in range(FOLD):
                    fb_ref[0, h, r + FOLD * c] = (
                        y4_ref[h, r, pl.ds(c, QKV_TM // FOLD ** 2, stride=FOLD), :].astype(BF16))


def _qkv_dil(x, g, w, cos_full, sin_signed, batch, seq):
    t = x.shape[0]
    tiles_per_seq = seq // QKV_TM

    def fold_spec(f):
        return pl.BlockSpec((1, QKV_TN // HEAD_DIM, f, QKV_TM // f, HEAD_DIM),
                            lambda i, j: (i // tiles_per_seq, j, 0, i % tiles_per_seq, 0))

    return pl.pallas_call(
        _qkv_dil_kernel,
        name="qkv_dil",
        grid=(t // QKV_TM, _DIL_COL_STEPS),
        in_specs=[
            pl.BlockSpec((QKV_TM, D_MODEL), lambda i, j: (i, 0)),
            pl.BlockSpec((1, D_MODEL), lambda i, j: (0, 0)),
            pl.BlockSpec((D_MODEL, QKV_TN), lambda i, j: (0, j)),
            pl.BlockSpec((QKV_TM, HEAD_DIM), lambda i, j: (i % tiles_per_seq, 0)),
            pl.BlockSpec((QKV_TM, HEAD_DIM), lambda i, j: (i % tiles_per_seq, 0)),
        ],
        out_specs=[fold_spec(f) for f in (FOLD, FOLD ** 2)],
        out_shape=[jax.ShapeDtypeStruct((batch, 3 * N_HEADS_DIL, f, seq // f, HEAD_DIM), BF16)
                   for f in (FOLD, FOLD ** 2)],
        scratch_shapes=[pltpu.VMEM((QKV_TM, D_MODEL), BF16),
                        pltpu.VMEM((QKV_TN // HEAD_DIM, QKV_TM, HEAD_DIM), F32),
                        pltpu.VMEM((QKV_TN // HEAD_DIM, FOLD, QKV_TM // FOLD, HEAD_DIM), F32)],
        compiler_params=_params("parallel", "arbitrary"),
    )(x, g, w, cos_full, sin_signed)


def _rope_tables(seq):
    inv = jnp.float32(ROPE_THETA) ** (-jnp.arange(0, HEAD_DIM, 2, dtype=F32) / HEAD_DIM)
    ang = jnp.arange(seq, dtype=F32)[:, None] * inv[None, :]
    cos, sin = jnp.cos(ang), jnp.sin(ang)
    return jnp.concatenate([cos, cos], axis=-1), jnp.concatenate([-sin, sin], axis=-1)


def _na_tile_key_row_start(i, rows):
    return jnp.clip(i * NA_QROWS - NA_ROWS // 2, 0, rows - NA_KROWS)


def _na_kernel(q_ref, k_ref, v_ref, bias_ref, g_ref, o_ref, *, rows):
    i = pl.program_id(2)
    start = pl.multiple_of(_na_tile_key_row_start(i, rows) * GRID_W, GRID_W)
    nk = NA_KROWS * GRID_W
    q = q_ref[0, 0]
    k = k_ref[0, 0, pl.ds(start, nk), :]
    v = v_ref[0, 0, pl.ds(start, nk), :]
    s = lax.dot_general(q, k, (((1,), (1,)), ((), ())), preferred_element_type=F32) + bias_ref[0, 0]
    m = jnp.max(s, axis=-1, keepdims=True)
    e = jnp.exp(s - m)
    den = jnp.sum(e, axis=-1, keepdims=True)
    o = jnp.dot(e.astype(BF16), v, preferred_element_type=F32) / den
    o_ref[0, 0] = _rms(o, g_ref[0]).astype(BF16)


def _na_bias_table(rpb, rows):
    n_tiles = rows // NA_QROWS
    dr_idx = np.zeros((3, NA_QROWS, NA_KROWS), np.int32)
    row_ok = np.zeros((3, NA_QROWS, NA_KROWS), bool)
    for cls, tile in enumerate((0, 1, n_tiles - 1)):
        ks = int(np.clip(tile * NA_QROWS - NA_ROWS // 2, 0, rows - NA_KROWS))
        r = tile * NA_QROWS + np.arange(NA_QROWS)
        rs = np.clip(r - NA_ROWS // 2, 0, rows - NA_ROWS)
        kr = ks + np.arange(NA_KROWS)
        row_ok[cls] = (kr[None, :] >= rs[:, None]) & (kr[None, :] < rs[:, None] + NA_ROWS)
        dr_idx[cls] = np.clip(kr[None, :] - r[:, None] + NA_ROWS - 1, 0, 2 * NA_ROWS - 2)
    c = np.arange(GRID_W)
    qs = np.clip(c - NA_COLS // 2, 0, GRID_W - NA_COLS)
    col_ok = (c[None, :] >= qs[:, None]) & (c[None, :] < qs[:, None] + NA_COLS)
    n_dr, n_dc = 2 * NA_ROWS - 1, 2 * NA_COLS - 1
    period = 2 * GRID_W
    v = jnp.concatenate([rpb[..., NA_COLS - 1:], jnp.zeros((N_HEADS_NA, n_dr, period - n_dc), F32),
                         rpb[..., :NA_COLS - 1]], axis=-1).astype(F32)
    toe = jnp.tile(v, (1, 1, GRID_W))[..., :GRID_W * (period - 1)]
    toe = toe.reshape(N_HEADS_NA, n_dr, GRID_W, period - 1)[..., :GRID_W]
    slabs = jnp.where(col_ok[None, None], toe, NEG)
    slabs = jnp.concatenate([slabs, jnp.full((N_HEADS_NA, 1, GRID_W, GRID_W), NEG, F32)], axis=1)
    slab_idx = np.where(row_ok, dr_idx, n_dr).reshape(-1)
    bias = jnp.take(slabs, slab_idx, axis=1)
    bias = bias.reshape(N_HEADS_NA, 3, NA_QROWS, NA_KROWS, GRID_W, GRID_W)
    bias = bias.transpose(1, 0, 2, 4, 3, 5)
    return bias.reshape(3, N_HEADS_NA, NA_QROWS * GRID_W, NA_KROWS * GRID_W)


def _na(qkv, bias, head_g, batch, seq):
    rows = seq // GRID_W
    n_tiles = rows // NA_QROWS
    tq = NA_QROWS * GRID_W

    def bias_map(b, h, i):
        return (jnp.where(i == 0, 0, jnp.where(i == n_tiles - 1, 2, 1)), h, 0, 0)

    return pl.pallas_call(
        functools.partial(_na_kernel, rows=rows),
        name="na",
        grid=(batch, N_HEADS_NA, n_tiles),
        in_specs=[
            pl.BlockSpec((1, 1, tq, HEAD_DIM), lambda b, h, i: (b, h, i, 0)),
            pl.BlockSpec((1, 1, seq, HEAD_DIM), lambda b, h, i: (b, N_HEADS_NA + h, 0, 0)),
            pl.BlockSpec((1, 1, seq, HEAD_DIM), lambda b, h, i: (b, 2 * N_HEADS_NA + h, 0, 0)),
            pl.BlockSpec((1, 1, tq, NA_KROWS * GRID_W), bias_map),
            pl.BlockSpec((1, 1, HEAD_DIM), lambda b, h, i: (h, 0, 0)),
        ],
        out_specs=pl.BlockSpec((1, 1, tq, HEAD_DIM), lambda b, h, i: (b, h, i, 0)),
        out_shape=jax.ShapeDtypeStruct((batch, N_HEADS_NA, seq, HEAD_DIM), BF16),
        compiler_params=_params("parallel", "parallel", "arbitrary"),
    )(qkv, qkv, qkv, bias, head_g)


_DIL_WIN = DIL_TQ + 2 * DIL_HALF
_BRANCH_FOLD = tuple(max(dil, FOLD) for (_, dil) in DIL_PAIRS)


def _dil_branch(q_ref, k_ref, v_ref, mask_ref, os_ref, ls_ref, slot, part, *, dil, sub, length):
    fold = dil * sub
    qn, kn = DIL_TQ // sub, _DIL_WIN // sub
    n_i = length // qn
    n_p = n_i // DIL_PARTS
    row0 = part * (length // DIL_PARTS)

    def gather(ref, p, start, size):
        parts = [ref[0, 0, p + dil * c, pl.ds(start, size), :] for c in range(sub)]
        return parts[0] if sub == 1 else jnp.concatenate(parts, axis=0)

    def group(g, carry):
        tiles = []
        for n in range(DIL_GROUP):
            t = g * DIL_GROUP + n
            p = t // n_p
            i = part * n_p + t % n_p
            q0 = pl.multiple_of(i * qn, qn)
            k0 = pl.multiple_of(jnp.clip(q0 - DIL_HALF // sub, 0, length - kn), DIL_HALF // sub)
            edge = jnp.where(i == 0, 0, jnp.where(i == n_i - 1, 2, 1))
            tiles.append((p, q0, k0, edge))
        scores = []
        for p, q0, k0, edge in tiles:
            s = lax.dot_general(gather(q_ref, p, q0, qn), gather(k_ref, p, k0, kn), (((1,), (1,)), ((), ())),
                                preferred_element_type=F32)
            scores.append(s + mask_ref[edge])
        stats = []
        for s in scores:
            m = jnp.max(s, axis=-1, keepdims=True)
            e = jnp.exp2(s - m)
            stats.append((m, e, jnp.sum(e, axis=-1, keepdims=True)))
        outs = []
        for (p, q0, k0, edge), (m, e, den) in zip(tiles, stats):
            outs.append(jnp.dot(e.astype(BF16), gather(v_ref, p, k0, kn), preferred_element_type=F32) / den)
        for (p, q0, k0, edge), (m, e, den), o in zip(tiles, stats, outs):
            lse = jnp.broadcast_to(m + jnp.log2(den), (DIL_TQ, HEAD_DIM))
            for c in range(sub):
                rows = pl.ds(fold * (q0 - row0) + p + dil * c, qn, stride=fold)
                os_ref[slot, rows, :] = o[c * qn:(c + 1) * qn]
                ls_ref[slot, rows, :] = lse[c * qn:(c + 1) * qn]
        return carry

    lax.fori_loop(0, dil * n_p // DIL_GROUP, group, 0)


def _dilated_kernel(qa_ref, ka_ref, va_ref, qb_ref, kb_ref, vb_ref, m0_ref, m1_ref, m2_ref, g_ref, o_ref,
                    os_ref, ls_ref, *, seq):
    part = pl.program_id(2)
    stored = {FOLD: (qa_ref, ka_ref, va_ref), FOLD ** 2: (qb_ref, kb_ref, vb_ref)}
    for slot, ((_, dil), fold, mask_ref) in enumerate(zip(DIL_PAIRS, _BRANCH_FOLD, (m0_ref, m1_ref, m2_ref))):
        _dil_branch(*stored[fold], mask_ref, os_ref, ls_ref, slot, part, dil=dil, sub=fold // dil,
                    length=seq // fold)

    def merge(c, carry):
        rows = pl.ds(pl.multiple_of(c * DIL_MERGE_ROWS, DIL_MERGE_ROWS), DIL_MERGE_ROWS)
        lses = [ls_ref[b, rows, :] for b in range(len(DIL_PAIRS))]
        mx = functools.reduce(jnp.maximum, lses)
        ws = [jnp.exp2(l - mx) for l in lses]
        o = sum(w * os_ref[b, rows, :] for b, w in enumerate(ws)) / sum(ws)
        o_ref[0, 0, rows, :] = _rms(o, g_ref[0]).astype(BF16)
        return carry

    lax.fori_loop(0, seq // DIL_PARTS // DIL_MERGE_ROWS, merge, 0)


def _dil_mask_table(sub, length):
    qn, kn = DIL_TQ // sub, _DIL_WIN // sub
    n_i = length // qn
    c = np.arange(sub)[:, None]
    out = []
    for i in (0, 1, n_i - 1):
        k0 = int(np.clip(i * qn - DIL_HALF // sub, 0, length - kn))
        qpos = ((i * qn + np.arange(qn))[None, :] * sub + c).reshape(-1)
        kpos = ((k0 + np.arange(kn))[None, :] * sub + c).reshape(-1)
        out.append(np.where(np.abs(kpos[None, :] - qpos[:, None]) <= DIL_HALF, 0.0, NEG))
    return jnp.asarray(np.stack(out), F32)


def _dilated(fa, fb, head_g, batch, seq):
    masks = []
    for (window, dil), fold in zip(DIL_PAIRS, _BRANCH_FOLD):
        sub, length = fold // dil, seq // fold
        assert window // (2 * dil) == DIL_HALF and fold % dil == 0 and DIL_TQ % sub == 0
        n_p = length // (DIL_TQ // sub) // DIL_PARTS
        assert n_p * DIL_PARTS * (DIL_TQ // sub) == length and (dil * n_p) % DIL_GROUP == 0 and n_p >= 1
        masks.append(_dil_mask_table(sub, length))

    def stored(f, first):
        return pl.BlockSpec((1, 1, f, seq // f, HEAD_DIM), lambda b, h, part: (b, first + h, 0, 0, 0))

    part_rows = seq // DIL_PARTS
    return pl.pallas_call(
        functools.partial(_dilated_kernel, seq=seq),
        name="dilated",
        grid=(batch, N_HEADS_DIL, DIL_PARTS),
        in_specs=[stored(f, first) for f in (FOLD, FOLD ** 2) for first in (0, N_HEADS_DIL, 2 * N_HEADS_DIL)]
        + [pl.BlockSpec((3, DIL_TQ, _DIL_WIN), lambda b, h, part: (0, 0, 0))] * len(DIL_PAIRS)
        + [pl.BlockSpec((1, 1, HEAD_DIM), lambda b, h, part: (N_HEADS_NA + h, 0, 0))],
        out_specs=pl.BlockSpec((1, 1, part_rows, HEAD_DIM), lambda b, h, part: (b, h, part, 0)),
        out_shape=jax.ShapeDtypeStruct((batch, N_HEADS_DIL, seq, HEAD_DIM), BF16),
        scratch_shapes=[pltpu.VMEM((len(DIL_PAIRS), part_rows, HEAD_DIM), F32)] * 2,
        compiler_params=_params("parallel", "parallel", "arbitrary"),
    )(fa, fa, fa, fb, fb, fb, *masks, head_g)


def _mix_out_kernel(ona_ref, odil_ref, wo_ref, h_ref, postg_ref, o_ref):
    heads = [ona_ref[0, h] for h in range(N_HEADS_NA)] + [odil_ref[0, h] for h in range(N_HEADS_DIL)]
    m = jnp.dot(jnp.concatenate(heads, axis=-1), wo_ref[...], preferred_element_type=F32)
    o_ref[...] = h_ref[...] + _rms(m, postg_ref[...])


def _mix_out(o_na, o_dil, w_o, h, post_g, seq):
    t = h.shape[0]
    tiles_per_seq = seq // MIX_TM
    heads = lambda n: pl.BlockSpec((1, n, MIX_TM, HEAD_DIM), lambda i: (i // tiles_per_seq, 0, i % tiles_per_seq, 0))
    row = pl.BlockSpec((MIX_TM, D_MODEL), lambda i: (i, 0))
    const = lambda shape: pl.BlockSpec(shape, lambda i: (0, 0))
    return pl.pallas_call(
        _mix_out_kernel,
        name="mix_out",
        grid=(t // MIX_TM,),
        in_specs=[heads(N_HEADS_NA), heads(N_HEADS_DIL), const((D_MODEL, D_MODEL)), row, const((1, D_MODEL))],
        out_specs=row,
        out_shape=jax.ShapeDtypeStruct((t, D_MODEL), F32),
        compiler_params=_params("parallel"),
    )(o_na, o_dil, w_o, h, post_g)


def _ple_kernel(h_ref, p_ref, preg_ref, wg_ref, wp_ref, postg_ref, o_ref):
    h = h_ref[...]
    u = _rms(h, preg_ref[...]).astype(BF16)
    gate = jax.nn.sigmoid(jnp.dot(u, wg_ref[...], preferred_element_type=F32))
    emb = jnp.dot(p_ref[...].astype(BF16), wp_ref[...], preferred_element_type=F32)
    o_ref[...] = h + _rms(gate * emb, postg_ref[...])


def _ple(h, p, pre_g, w_gate, w_proj, post_g):
    t = h.shape[0]
    row = lambda width: pl.BlockSpec((PLE_TM, width), lambda i: (i, 0))
    const = lambda shape: pl.BlockSpec(shape, lambda i: (0, 0))
    return pl.pallas_call(
        _ple_kernel,
        name="ple",
        grid=(t // PLE_TM,),
        in_specs=[row(D_MODEL), row(PLE_DIM), const((1, D_MODEL)), const((D_MODEL, D_MODEL)),
                  const((PLE_DIM, D_MODEL)), const((1, D_MODEL))],
        out_specs=row(D_MODEL),
        out_shape=jax.ShapeDtypeStruct((t, D_MODEL), F32),
        compiler_params=_params("parallel"),
    )(h, p, pre_g, w_gate, w_proj, post_g)


def kernel(x, p, ffn1_pre_g, ffn1_w_gate, ffn1_w_up, ffn1_w_down, ffn1_post_g, mix_pre_g, w_qkv, na_rpb, out_g, w_o, mix_post_g, ffn2_pre_g, ffn2_w_gate, ffn2_w_up, ffn2_w_down, ffn2_post_g, ple_pre_g, w_ple_gate, w_ple_proj, ple_post_g):
    batch, seq, d_model = x.shape
    depth = p.shape[0]
    assert d_model == D_MODEL and seq % (GRID_W * NA_KROWS) == 0
    tokens = batch * seq
    rows = seq // GRID_W
    cos_full, sin_signed = _rope_tables(seq)
    gain = lambda g: g.reshape(1, D_MODEL)
    w16 = lambda w: w.astype(BF16)

    h = x.reshape(tokens, D_MODEL)
    for i in range(depth):
        h = _ffn(h, gain(ffn1_pre_g[i]), w16(ffn1_w_gate[i]), w16(ffn1_w_up[i]), w16(ffn1_w_down[i]),
                 gain(ffn1_post_g[i]))
        qkv_na = _qkv_na(h, gain(mix_pre_g[i]), w16(w_qkv[i][:, :3 * W_NA]), batch, seq)
        fa, fb = _qkv_dil(h, gain(mix_pre_g[i]), w16(w_qkv[i][:, 3 * W_NA:]), cos_full, sin_signed, batch, seq)
        head_g = out_g[i].reshape(N_HEADS, 1, HEAD_DIM)
        o_na = _na(qkv_na, _na_bias_table(na_rpb[i], rows), head_g, batch, seq)
        o_dil = _dilated(fa, fb, head_g, batch, seq)
        h = _mix_out(o_na, o_dil, w16(w_o[i]), h, gain(mix_post_g[i]), seq)
        h = _ffn(h, gain(ffn2_pre_g[i]), w16(ffn2_w_gate[i]), w16(ffn2_w_up[i]), w16(ffn2_w_down[i]),
                 gain(ffn2_post_g[i]))
        h = _ple(h, p[i].reshape(tokens, PLE_DIM), gain(ple_pre_g[i]), w16(w_ple_gate[i]), w16(w_ple_proj[i]),
                 gain(ple_post_g[i]))
    return h.reshape(batch, seq, D_MODEL)
```

```python
import functools
import math

import jax
import jax.numpy as jnp
import numpy as np
from jax import lax
from jax.experimental import pallas as pl
from jax.experimental.pallas import tpu as pltpu

D_MODEL = 2048
D_FF = 5632
HEAD_DIM = 128
N_HEADS = 16
N_HEADS_NA = 4
N_HEADS_DIL = 12
W_NA = N_HEADS_NA * HEAD_DIM
W_DIL = N_HEADS_DIL * HEAD_DIM
GRID_W = 64
NA_ROWS = 8
NA_COLS = 16
DIL_PAIRS = ((128, 1), (512, 4), (2048, 16))
PLE_DIM = 256
ROPE_THETA = 10000.0
EPS = 1e-6
NEG = -1e30
SCALE = HEAD_DIM ** -0.5

F32 = jnp.float32
BF16 = jnp.bfloat16

VMEM_LIMIT_BYTES = 56 * 1024 * 1024

FFN_TM = 512
FFN_TF = 512
QKV_TM = 1024
QKV_TN = 768
FOLD = 4
NA_QROWS = 8
NA_KROWS = 16
NA_SUB_ROWS = 128
DIL_TQ = 128
DIL_GROUP = 32
DIL_LAG = 2
DIL_HALF = 64
DIL_PARTS = 2
DIL_MERGE_ROWS = 1024
MIX_TM = 512
PLE_TM = 512


def _rms(x, g):
    return x * lax.rsqrt(jnp.mean(x * x, axis=-1, keepdims=True) + EPS) * g


def _params(*sem):
    return pltpu.CompilerParams(dimension_semantics=sem, vmem_limit_bytes=VMEM_LIMIT_BYTES)


def _skewed(n, lag, score, softmax, finish):
    scores, probs = {}, {}
    for t in range(n + 2 * lag):
        if t < n:
            scores[t] = score(t)
        if 0 <= t - lag < n:
            probs[t - lag] = softmax(scores.pop(t - lag))
        if 0 <= t - 2 * lag < n:
            finish(t - 2 * lag, probs.pop(t - 2 * lag))


def _ffn_kernel(x_ref, pre_g_ref, wg_ref, wu_ref, wd_ref, post_g_ref, o_ref, u_ref):
    j = pl.program_id(1)

    @pl.when(j == 0)
    def _():
        u_ref[...] = _rms(x_ref[...], pre_g_ref[...]).astype(BF16)
        o_ref[...] = jnp.zeros_like(o_ref)

    u = u_ref[...]
    g = jnp.dot(u, wg_ref[...], preferred_element_type=F32)
    v = jnp.dot(u, wu_ref[...], preferred_element_type=F32)
    mid = (g * jax.nn.sigmoid(g) * v).astype(BF16)
    o_ref[...] += jnp.dot(mid, wd_ref[...], preferred_element_type=F32)

    @pl.when(j == pl.num_programs(1) - 1)
    def _():
        o_ref[...] = x_ref[...] + 0.5 * _rms(o_ref[...], post_g_ref[...])


def _ffn(x, pre_g, w_gate, w_up, w_down, post_g):
    t = x.shape[0]
    return pl.pallas_call(
        _ffn_kernel,
        name="ffn",
        grid=(t // FFN_TM, D_FF // FFN_TF),
        in_specs=[
            pl.BlockSpec((FFN_TM, D_MODEL), lambda i, j: (i, 0)),
            pl.BlockSpec((1, D_MODEL), lambda i, j: (0, 0)),
            pl.BlockSpec((D_MODEL, FFN_TF), lambda i, j: (0, j)),
            pl.BlockSpec((D_MODEL, FFN_TF), lambda i, j: (0, j)),
            pl.BlockSpec((FFN_TF, D_MODEL), lambda i, j: (j, 0)),
            pl.BlockSpec((1, D_MODEL), lambda i, j: (0, 0)),
        ],
        out_specs=pl.BlockSpec((FFN_TM, D_MODEL), lambda i, j: (i, 0)),
        out_shape=jax.ShapeDtypeStruct((t, D_MODEL), F32),
        scratch_shapes=[pltpu.VMEM((FFN_TM, D_MODEL), BF16)],
        compiler_params=_params("parallel", "arbitrary"),
    )(x, pre_g, w_gate, w_up, w_down, post_g)


Q_SCALE_NA = SCALE
Q_SCALE_DIL = SCALE * math.log2(math.e)
_DIL_COL_STEPS = 3 * W_DIL // QKV_TN
_DIL_STEPS_PER_PART = W_DIL // QKV_TN
_DOT_N = 2 * HEAD_DIM


def _qkv_na_kernel(x_ref, g_ref, w_ref, colscale_ref, o_ref, u_ref):
    @pl.when(pl.program_id(1) == 0)
    def _():
        u_ref[...] = _rms(x_ref[...], g_ref[...]).astype(BF16)

    y = jnp.dot(u_ref[...], w_ref[...], preferred_element_type=F32) * colscale_ref[...]
    for h in range(QKV_TN // HEAD_DIM):
        o_ref[0, h] = y[:, h * HEAD_DIM:(h + 1) * HEAD_DIM].astype(BF16)


def _qkv_na(x, g, w, batch, seq):
    t = x.shape[0]
    tiles_per_seq = seq // QKV_TM
    colscale = jnp.concatenate([jnp.full((1, W_NA), Q_SCALE_NA, F32), jnp.ones((1, 2 * W_NA), F32)], axis=-1)
    return pl.pallas_call(
        _qkv_na_kernel,
        name="qkv_na",
        grid=(t // QKV_TM, 3 * W_NA // QKV_TN),
        in_specs=[
            pl.BlockSpec((QKV_TM, D_MODEL), lambda i, j: (i, 0)),
            pl.BlockSpec((1, D_MODEL), lambda i, j: (0, 0)),
            pl.BlockSpec((D_MODEL, QKV_TN), lambda i, j: (0, j)),
            pl.BlockSpec((1, QKV_TN), lambda i, j: (0, j)),
        ],
        out_specs=pl.BlockSpec((1, QKV_TN // HEAD_DIM, QKV_TM, HEAD_DIM),
                               lambda i, j: (i // tiles_per_seq, j, i % tiles_per_seq, 0)),
        out_shape=jax.ShapeDtypeStruct((batch, 3 * N_HEADS_NA, seq, HEAD_DIM), BF16),
        scratch_shapes=[pltpu.VMEM((QKV_TM, D_MODEL), BF16)],
        compiler_params=_params("parallel", "arbitrary"),
    )(x, g, w, colscale)


def _qkv_dil_kernel(x_ref, g_ref, w_ref, cos_ref, sin_ref, fa_ref, fb_ref, u_ref, y_ref, y4_ref):
    j = pl.program_id(1)

    @pl.when(j == 0)
    def _():
        u_ref[...] = _rms(x_ref[...], g_ref[...]).astype(BF16)

    rotary = j < 2 * _DIL_STEPS_PER_PART
    scale = jnp.where(j < _DIL_STEPS_PER_PART, Q_SCALE_DIL, 1.0).astype(F32)
    cos = cos_ref[...]
    sin = sin_ref[...]
    for pair in range(QKV_TN // _DOT_N):
        y2 = jnp.dot(u_ref[...], w_ref[:, pair * _DOT_N:(pair + 1) * _DOT_N], preferred_element_type=F32)
        for half in range(_DOT_N // HEAD_DIM):
            h = pair * (_DOT_N // HEAD_DIM) + half
            y = y2[:, half * HEAD_DIM:(half + 1) * HEAD_DIM]
            y_ref[h] = jnp.where(rotary, y * cos + pltpu.roll(y, HEAD_DIM // 2, axis=1) * sin, y) * scale
            for r in range(FOLD):
                y4 = y_ref[h, pl.ds(r, QKV_TM // FOLD, stride=FOLD), :]
                fa_ref[0, h, r] = y4.astype(BF16)
                y4_ref[h, r] = y4
                for c in range(FOLD):
                    fb_ref[0, h, r + FOLD * c] = (
                        y4_ref[h, r, pl.ds(c, QKV_TM // FOLD ** 2, stride=FOLD), :].astype(BF16))


def _qkv_dil(x, g, w, cos_full, sin_signed, batch, seq):
    t = x.shape[0]
    tiles_per_seq = seq // QKV_TM

    def fold_spec(f):
        return pl.BlockSpec((1, QKV_TN // HEAD_DIM, f, QKV_TM // f, HEAD_DIM),
                            lambda i, j: (i // tiles_per_seq, j, 0, i % tiles_per_seq, 0))

    return pl.pallas_call(
        _qkv_dil_kernel,
        name="qkv_dil",
        grid=(t // QKV_TM, _DIL_COL_STEPS),
        in_specs=[
            pl.BlockSpec((QKV_TM, D_MODEL), lambda i, j: (i, 0)),
            pl.BlockSpec((1, D_MODEL), lambda i, j: (0, 0)),
            pl.BlockSpec((D_MODEL, QKV_TN), lambda i, j: (0, j)),
            pl.BlockSpec((QKV_TM, HEAD_DIM), lambda i, j: (i % tiles_per_seq, 0)),
            pl.BlockSpec((QKV_TM, HEAD_DIM), lambda i, j: (i % tiles_per_seq, 0)),
        ],
        out_specs=[fold_spec(f) for f in (FOLD, FOLD ** 2)],
        out_shape=[jax.ShapeDtypeStruct((batch, 3 * N_HEADS_DIL, f, seq // f, HEAD_DIM), BF16)
                   for f in (FOLD, FOLD ** 2)],
        scratch_shapes=[pltpu.VMEM((QKV_TM, D_MODEL), BF16),
                        pltpu.VMEM((QKV_TN // HEAD_DIM, QKV_TM, HEAD_DIM), F32),
                        pltpu.VMEM((QKV_TN // HEAD_DIM, FOLD, QKV_TM // FOLD, HEAD_DIM), F32)],
        compiler_params=_params("parallel", "arbitrary"),
    )(x, g, w, cos_full, sin_signed)


def _rope_tables(seq):
    inv = jnp.float32(ROPE_THETA) ** (-jnp.arange(0, HEAD_DIM, 2, dtype=F32) / HEAD_DIM)
    ang = jnp.arange(seq, dtype=F32)[:, None] * inv[None, :]
    cos, sin = jnp.cos(ang), jnp.sin(ang)
    return jnp.concatenate([cos, cos], axis=-1), jnp.concatenate([-sin, sin], axis=-1)


def _na_tile_key_row_start(i, rows):
    return jnp.clip(i * NA_QROWS - NA_ROWS // 2, 0, rows - NA_KROWS)


def _na_kernel(q_ref, k_ref, v_ref, bias_ref, g_ref, o_ref, *, rows):
    i = pl.program_id(2)
    start = pl.multiple_of(_na_tile_key_row_start(i, rows) * GRID_W, GRID_W)
    nk = NA_KROWS * GRID_W
    k = k_ref[0, 0, pl.ds(start, nk), :]
    v = v_ref[0, 0, pl.ds(start, nk), :]
    sub = lambda n: slice(n * NA_SUB_ROWS, (n + 1) * NA_SUB_ROWS)

    def score(n):
        return lax.dot_general(q_ref[0, 0, sub(n), :], k, (((1,), (1,)), ((), ())),
                               preferred_element_type=F32) + bias_ref[0, 0, sub(n), :]

    def softmax(s):
        e = jnp.exp(s - jnp.max(s, axis=-1, keepdims=True))
        return e.astype(BF16), jnp.sum(e, axis=-1, keepdims=True)

    def finish(n, prob):
        e, den = prob
        o = jnp.dot(e, v, preferred_element_type=F32) / den
        o_ref[0, 0, sub(n), :] = _rms(o, g_ref[0]).astype(BF16)

    _skewed(NA_QROWS * GRID_W // NA_SUB_ROWS, 1, score, softmax, finish)


def _na_bias_table(rpb, rows):
    n_tiles = rows // NA_QROWS
    dr_idx = np.zeros((3, NA_QROWS, NA_KROWS), np.int32)
    row_ok = np.zeros((3, NA_QROWS, NA_KROWS), bool)
    for cls, tile in enumerate((0, 1, n_tiles - 1)):
        ks = int(np.clip(tile * NA_QROWS - NA_ROWS // 2, 0, rows - NA_KROWS))
        r = tile * NA_QROWS + np.arange(NA_QROWS)
        rs = np.clip(r - NA_ROWS // 2, 0, rows - NA_ROWS)
        kr = ks + np.arange(NA_KROWS)
        row_ok[cls] = (kr[None, :] >= rs[:, None]) & (kr[None, :] < rs[:, None] + NA_ROWS)
        dr_idx[cls] = np.clip(kr[None, :] - r[:, None] + NA_ROWS - 1, 0, 2 * NA_ROWS - 2)
    c = np.arange(GRID_W)
    qs = np.clip(c - NA_COLS // 2, 0, GRID_W - NA_COLS)
    col_ok = (c[None, :] >= qs[:, None]) & (c[None, :] < qs[:, None] + NA_COLS)
    n_dr, n_dc = 2 * NA_ROWS - 1, 2 * NA_COLS - 1
    period = 2 * GRID_W
    v = jnp.concatenate([rpb[..., NA_COLS - 1:], jnp.zeros((N_HEADS_NA, n_dr, period - n_dc), F32),
                         rpb[..., :NA_COLS - 1]], axis=-1).astype(F32)
    toe = jnp.tile(v, (1, 1, GRID_W))[..., :GRID_W * (period - 1)]
    toe = toe.reshape(N_HEADS_NA, n_dr, GRID_W, period - 1)[..., :GRID_W]
    slabs = jnp.where(col_ok[None, None], toe, NEG)
    slabs = jnp.concatenate([slabs, jnp.full((N_HEADS_NA, 1, GRID_W, GRID_W), NEG, F32)], axis=1)
    slab_idx = np.where(row_ok, dr_idx, n_dr).reshape(-1)
    bias = jnp.take(slabs, slab_idx, axis=1)
    bias = bias.reshape(N_HEADS_NA, 3, NA_QROWS, NA_KROWS, GRID_W, GRID_W)
    bias = bias.transpose(1, 0, 2, 4, 3, 5)
    return bias.reshape(3, N_HEADS_NA, NA_QROWS * GRID_W, NA_KROWS * GRID_W)


def _na(qkv, bias, head_g, batch, seq):
    rows = seq // GRID_W
    n_tiles = rows // NA_QROWS
    tq = NA_QROWS * GRID_W

    def bias_map(b, h, i):
        return (jnp.where(i == 0, 0, jnp.where(i == n_tiles - 1, 2, 1)), h, 0, 0)

    return pl.pallas_call(
        functools.partial(_na_kernel, rows=rows),
        name="na",
        grid=(batch, N_HEADS_NA, n_tiles),
        in_specs=[
            pl.BlockSpec((1, 1, tq, HEAD_DIM), lambda b, h, i: (b, h, i, 0)),
            pl.BlockSpec((1, 1, seq, HEAD_DIM), lambda b, h, i: (b, N_HEADS_NA + h, 0, 0)),
            pl.BlockSpec((1, 1, seq, HEAD_DIM), lambda b, h, i: (b, 2 * N_HEADS_NA + h, 0, 0)),
            pl.BlockSpec((1, 1, tq, NA_KROWS * GRID_W), bias_map),
            pl.BlockSpec((1, 1, HEAD_DIM), lambda b, h, i: (h, 0, 0)),
        ],
        out_specs=pl.BlockSpec((1, 1, tq, HEAD_DIM), lambda b, h, i: (b, h, i, 0)),
        out_shape=jax.ShapeDtypeStruct((batch, N_HEADS_NA, seq, HEAD_DIM), BF16),
        compiler_params=_params("parallel", "parallel", "arbitrary"),
    )(qkv, qkv, qkv, bias, head_g)


_DIL_WIN = DIL_TQ + 2 * DIL_HALF
_BRANCH_FOLD = tuple(max(dil, FOLD) for (_, dil) in DIL_PAIRS)


def _dil_branch(q_ref, k_ref, v_ref, mask_ref, os_ref, ls_ref, slot, part, *, dil, sub, length):
    fold = dil * sub
    qn, kn = DIL_TQ // sub, _DIL_WIN // sub
    n_i = length // qn
    n_p = n_i // DIL_PARTS
    row0 = part * (length // DIL_PARTS)

    def gather(ref, p, start, size):
        parts = [ref[0, 0, p + dil * c, pl.ds(start, size), :] for c in range(sub)]
        return parts[0] if sub == 1 else jnp.concatenate(parts, axis=0)

    def group(g, carry):
        tiles = []
        for n in range(DIL_GROUP):
            t = g * DIL_GROUP + n
            p = t // n_p
            i = part * n_p + t % n_p
            q0 = pl.multiple_of(i * qn, qn)
            k0 = pl.multiple_of(jnp.clip(q0 - DIL_HALF // sub, 0, length - kn), DIL_HALF // sub)
            edge = jnp.where(i == 0, 0, jnp.where(i == n_i - 1, 2, 1))
            tiles.append((p, q0, k0, edge))

        def score(n):
            p, q0, k0, edge = tiles[n]
            return lax.dot_general(gather(q_ref, p, q0, qn), gather(k_ref, p, k0, kn), (((1,), (1,)), ((), ())),
                                   preferred_element_type=F32) + mask_ref[edge]

        def softmax(s):
            m = jnp.max(s, axis=-1, keepdims=True)
            e = jnp.exp2(s - m)
            return m, e.astype(BF16), jnp.sum(e, axis=-1, keepdims=True)

        def finish(n, prob):
            p, q0, k0, edge = tiles[n]
            m, e, den = prob
            o = jnp.dot(e, gather(v_ref, p, k0, kn), preferred_element_type=F32) / den
            lse = jnp.broadcast_to(m + jnp.log2(den), (DIL_TQ, HEAD_DIM))
            for c in range(sub):
                rows = pl.ds(fold * (q0 - row0) + p + dil * c, qn, stride=fold)
                os_ref[slot, rows, :] = o[c * qn:(c + 1) * qn]
                ls_ref[slot, rows, :] = lse[c * qn:(c + 1) * qn]

        _skewed(DIL_GROUP, DIL_LAG, score, softmax, finish)
        return carry

    lax.fori_loop(0, dil * n_p // DIL_GROUP, group, 0)


def _dilated_kernel(qa_ref, ka_ref, va_ref, qb_ref, kb_ref, vb_ref, m0_ref, m1_ref, m2_ref, g_ref, o_ref,
                    os_ref, ls_ref, *, seq):
    part = pl.program_id(2)
    stored = {FOLD: (qa_ref, ka_ref, va_ref), FOLD ** 2: (qb_ref, kb_ref, vb_ref)}
    for slot, ((_, dil), fold, mask_ref) in enumerate(zip(DIL_PAIRS, _BRANCH_FOLD, (m0_ref, m1_ref, m2_ref))):
        _dil_branch(*stored[fold], mask_ref, os_ref, ls_ref, slot, part, dil=dil, sub=fold // dil,
                    length=seq // fold)

    def merge(c, carry):
        rows = pl.ds(pl.multiple_of(c * DIL_MERGE_ROWS, DIL_MERGE_ROWS), DIL_MERGE_ROWS)
        lses = [ls_ref[b, rows, :] for b in range(len(DIL_PAIRS))]
        mx = functools.reduce(jnp.maximum, lses)
        ws = [jnp.exp2(l - mx) for l in lses]
        num = functools.reduce(jnp.add, [w * os_ref[b, rows, :] for b, w in enumerate(ws)])
        o = num / functools.reduce(jnp.add, ws)
        o_ref[0, 0, rows, :] = _rms(o, g_ref[0]).astype(BF16)
        return carry

    lax.fori_loop(0, seq // DIL_PARTS // DIL_MERGE_ROWS, merge, 0)


def _dil_mask_table(sub, length):
    qn, kn = DIL_TQ // sub, _DIL_WIN // sub
    n_i = length // qn
    c = np.arange(sub)[:, None]
    out = []
    for i in (0, 1, n_i - 1):
        k0 = int(np.clip(i * qn - DIL_HALF // sub, 0, length - kn))
        qpos = ((i * qn + np.arange(qn))[None, :] * sub + c).reshape(-1)
        kpos = ((k0 + np.arange(kn))[None, :] * sub + c).reshape(-1)
        out.append(np.where(np.abs(kpos[None, :] - qpos[:, None]) <= DIL_HALF, 0.0, NEG))
    return jnp.asarray(np.stack(out), F32)


def _dilated(fa, fb, head_g, batch, seq):
    masks = []
    for (window, dil), fold in zip(DIL_PAIRS, _BRANCH_FOLD):
        sub, length = fold // dil, seq // fold
        assert window // (2 * dil) == DIL_HALF and fold % dil == 0 and DIL_TQ % sub == 0
        n_p = length // (DIL_TQ // sub) // DIL_PARTS
        assert n_p * DIL_PARTS * (DIL_TQ // sub) == length and (dil * n_p) % DIL_GROUP == 0 and n_p >= 1
        masks.append(_dil_mask_table(sub, length))

    def stored(f, first):
        return pl.BlockSpec((1, 1, f, seq // f, HEAD_DIM), lambda b, h, part: (b, first + h, 0, 0, 0))

    part_rows = seq // DIL_PARTS
    return pl.pallas_call(
        functools.partial(_dilated_kernel, seq=seq),
        name="dilated",
        grid=(batch, N_HEADS_DIL, DIL_PARTS),
        in_specs=[stored(f, first) for f in (FOLD, FOLD ** 2) for first in (0, N_HEADS_DIL, 2 * N_HEADS_DIL)]
        + [pl.BlockSpec((3, DIL_TQ, _DIL_WIN), lambda b, h, part: (0, 0, 0))] * len(DIL_PAIRS)
        + [pl.BlockSpec((1, 1, HEAD_DIM), lambda b, h, part: (N_HEADS_NA + h, 0, 0))],
        out_specs=pl.BlockSpec((1, 1, part_rows, HEAD_DIM), lambda b, h, part: (b, h, part, 0)),
        out_shape=jax.ShapeDtypeStruct((batch, N_HEADS_DIL, seq, HEAD_DIM), BF16),
        scratch_shapes=[pltpu.VMEM((len(DIL_PAIRS), part_rows, HEAD_DIM), F32)] * 2,
        compiler_params=_params("parallel", "parallel", "arbitrary"),
    )(fa, fa, fa, fb, fb, fb, *masks, head_g)


def _mix_out_kernel(ona_ref, odil_ref, wo_ref, h_ref, postg_ref, o_ref):
    heads = [ona_ref[0, h] for h in range(N_HEADS_NA)] + [odil_ref[0, h] for h in range(N_HEADS_DIL)]
    m = jnp.dot(jnp.concatenate(heads, axis=-1), wo_ref[...], preferred_element_type=F32)
    o_ref[...] = h_ref[...] + _rms(m, postg_ref[...])


def _mix_out(o_na, o_dil, w_o, h, post_g, seq):
    t = h.shape[0]
    tiles_per_seq = seq // MIX_TM
    heads = lambda n: pl.BlockSpec((1, n, MIX_TM, HEAD_DIM), lambda i: (i // tiles_per_seq, 0, i % tiles_per_seq, 0))
    row = pl.BlockSpec((MIX_TM, D_MODEL), lambda i: (i, 0))
    const = lambda shape: pl.BlockSpec(shape, lambda i: (0, 0))
    return pl.pallas_call(
        _mix_out_kernel,
        name="mix_out",
        grid=(t // MIX_TM,),
        in_specs=[heads(N_HEADS_NA), heads(N_HEADS_DIL), const((D_MODEL, D_MODEL)), row, const((1, D_MODEL))],
        out_specs=row,
        out_shape=jax.ShapeDtypeStruct((t, D_MODEL), F32),
        compiler_params=_params("parallel"),
    )(o_na, o_dil, w_o, h, post_g)


def _ple_kernel(h_ref, p_ref, preg_ref, wg_ref, wp_ref, postg_ref, o_ref):
    h = h_ref[...]
    u = _rms(h, preg_ref[...]).astype(BF16)
    gate = jax.nn.sigmoid(jnp.dot(u, wg_ref[...], preferred_element_type=F32))
    emb = jnp.dot(p_ref[...].astype(BF16), wp_ref[...], preferred_element_type=F32)
    o_ref[...] = h + _rms(gate * emb, postg_ref[...])


def _ple(h, p, pre_g, w_gate, w_proj, post_g):
    t = h.shape[0]
    row = lambda width: pl.BlockSpec((PLE_TM, width), lambda i: (i, 0))
    const = lambda shape: pl.BlockSpec(shape, lambda i: (0, 0))
    return pl.pallas_call(
        _ple_kernel,
        name="ple",
        grid=(t // PLE_TM,),
        in_specs=[row(D_MODEL), row(PLE_DIM), const((1, D_MODEL)), const((D_MODEL, D_MODEL)),
                  const((PLE_DIM, D_MODEL)), const((1, D_MODEL))],
        out_specs=row(D_MODEL),
        out_shape=jax.ShapeDtypeStruct((t, D_MODEL), F32),
        compiler_params=_params("parallel"),
    )(h, p, pre_g, w_gate, w_proj, post_g)


def kernel(x, p, ffn1_pre_g, ffn1_w_gate, ffn1_w_up, ffn1_w_down, ffn1_post_g, mix_pre_g, w_qkv, na_rpb, out_g, w_o, mix_post_g, ffn2_pre_g, ffn2_w_gate, ffn2_w_up, ffn2_w_down, ffn2_post_g, ple_pre_g, w_ple_gate, w_ple_proj, ple_post_g):
    batch, seq, d_model = x.shape
    depth = p.shape[0]
    assert d_model == D_MODEL and seq % (GRID_W * NA_KROWS) == 0
    tokens = batch * seq
    rows = seq // GRID_W
    cos_full, sin_signed = _rope_tables(seq)
    gain = lambda g: g.reshape(1, D_MODEL)
    w16 = lambda w: w.astype(BF16)

    h = x.reshape(tokens, D_MODEL)
    for i in range(depth):
        h = _ffn(h, gain(ffn1_pre_g[i]), w16(ffn1_w_gate[i]), w16(ffn1_w_up[i]), w16(ffn1_w_down[i]),
                 gain(ffn1_post_g[i]))
        qkv_na = _qkv_na(h, gain(mix_pre_g[i]), w16(w_qkv[i][:, :3 * W_NA]), batch, seq)
        fa, fb = _qkv_dil(h, gain(mix_pre_g[i]), w16(w_qkv[i][:, 3 * W_NA:]), cos_full, sin_signed, batch, seq)
        head_g = out_g[i].reshape(N_HEADS, 1, HEAD_DIM)
        o_na = _na(qkv_na, _na_bias_table(na_rpb[i], rows), head_g, batch, seq)
        o_dil = _dilated(fa, fb, head_g, batch, seq)
        h = _mix_out(o_na, o_dil, w16(w_o[i]), h, gain(mix_post_g[i]), seq)
        h = _ffn(h, gain(ffn2_pre_g[i]), w16(ffn2_w_gate[i]), w16(ffn2_w_up[i]), w16(ffn2_w_down[i]),
                 gain(ffn2_post_g[i]))
        h = _ple(h, p[i].reshape(tokens, PLE_DIM), gain(ple_pre_g[i]), w16(w_ple_gate[i]), w16(w_ple_proj[i]),
                 gain(ple_post_g[i]))
    return h.reshape(batch, seq, D_MODEL)
```

```python
import functools
import math

import jax
import jax.numpy as jnp
import numpy as np
from jax import lax
from jax.experimental import pallas as pl
from jax.experimental.pallas import tpu as pltpu

D_MODEL = 2048
D_FF = 5632
HEAD_DIM = 128
N_HEADS = 16
N_HEADS_NA = 4
N_HEADS_DIL = 12
W_NA = N_HEADS_NA * HEAD_DIM
W_DIL = N_HEADS_DIL * HEAD_DIM
GRID_W = 64
NA_ROWS = 8
NA_COLS = 16
DIL_PAIRS = ((128, 1), (512, 4), (2048, 16))
PLE_DIM = 256
ROPE_THETA = 10000.0
EPS = 1e-6
NEG = -1e30
SCALE = HEAD_DIM ** -0.5

F32 = jnp.float32
BF16 = jnp.bfloat16

VMEM_LIMIT_BYTES = 56 * 1024 * 1024

FFN_TM = 512
FFN_TF = 512
QKV_TM = 1024
QKV_TN = 768
FOLD = 4
NA_QROWS = 8
NA_KROWS = 16
NA_SUB_ROWS = 128
DIL_TQ = 128
DIL_GROUP = 32
DIL_LAG = 2
DIL_HALF = 64
DIL_PARTS = 2
DIL_MERGE_ROWS = 1024
MIX_TM = 512
PLE_TM = 512


def _rms(x, g):
    return x * lax.rsqrt(jnp.mean(x * x, axis=-1, keepdims=True) + EPS) * g


def _params(*sem):
    return pltpu.CompilerParams(dimension_semantics=sem, vmem_limit_bytes=VMEM_LIMIT_BYTES)


def _normed_input(first, x_ref, g_ref, u_ref):
    if not first:
        return u_ref[...]
    u = _rms(x_ref[...], g_ref[...]).astype(BF16)
    u_ref[...] = u
    return u


def _first_or_later(step):
    j = pl.program_id(1)
    pl.when(j == 0)(functools.partial(step, True))
    pl.when(j > 0)(functools.partial(step, False))


def _skewed(n, lag, score, softmax, finish):
    scores, probs = {}, {}
    for t in range(n + 2 * lag):
        if t < n:
            scores[t] = score(t)
        if 0 <= t - lag < n:
            probs[t - lag] = softmax(scores.pop(t - lag))
        if 0 <= t - 2 * lag < n:
            finish(t - 2 * lag, probs.pop(t - 2 * lag))


def _ffn_kernel(x_ref, pre_g_ref, wg_ref, wu_ref, wd_ref, post_g_ref, o_ref, u_ref):
    j = pl.program_id(1)
    last = pl.num_programs(1) - 1

    def step(first, final):
        u = _normed_input(first, x_ref, pre_g_ref, u_ref)
        g = jnp.dot(u, wg_ref[...], preferred_element_type=F32)
        v = jnp.dot(u, wu_ref[...], preferred_element_type=F32)
        mid = (g * jax.nn.sigmoid(g) * v).astype(BF16)
        acc = jnp.dot(mid, wd_ref[...], preferred_element_type=F32)
        if not first:
            acc = o_ref[...] + acc
        o_ref[...] = x_ref[...] + 0.5 * _rms(acc, post_g_ref[...]) if final else acc

    pl.when(j == 0)(functools.partial(step, True, False))
    pl.when(jnp.logical_and(j > 0, j < last))(functools.partial(step, False, False))
    pl.when(j == last)(functools.partial(step, False, True))


def _ffn(x, pre_g, w_gate, w_up, w_down, post_g):
    t = x.shape[0]
    assert D_FF // FFN_TF >= 2
    return pl.pallas_call(
        _ffn_kernel,
        name="ffn",
        grid=(t // FFN_TM, D_FF // FFN_TF),
        in_specs=[
            pl.BlockSpec((FFN_TM, D_MODEL), lambda i, j: (i, 0)),
            pl.BlockSpec((1, D_MODEL), lambda i, j: (0, 0)),
            pl.BlockSpec((D_MODEL, FFN_TF), lambda i, j: (0, j)),
            pl.BlockSpec((D_MODEL, FFN_TF), lambda i, j: (0, j)),
            pl.BlockSpec((FFN_TF, D_MODEL), lambda i, j: (j, 0)),
            pl.BlockSpec((1, D_MODEL), lambda i, j: (0, 0)),
        ],
        out_specs=pl.BlockSpec((FFN_TM, D_MODEL), lambda i, j: (i, 0)),
        out_shape=jax.ShapeDtypeStruct((t, D_MODEL), F32),
        scratch_shapes=[pltpu.VMEM((FFN_TM, D_MODEL), BF16)],
        compiler_params=_params("parallel", "arbitrary"),
    )(x, pre_g, w_gate, w_up, w_down, post_g)


Q_SCALE_NA = SCALE
Q_SCALE_DIL = SCALE * math.log2(math.e)
_DIL_COL_STEPS = 3 * W_DIL // QKV_TN
_DIL_STEPS_PER_PART = W_DIL // QKV_TN
_DOT_N = 2 * HEAD_DIM


def _qkv_na_kernel(x_ref, g_ref, w_ref, colscale_ref, o_ref, u_ref):
    def step(first):
        u = _normed_input(first, x_ref, g_ref, u_ref)
        y = jnp.dot(u, w_ref[...], preferred_element_type=F32) * colscale_ref[...]
        for h in range(QKV_TN // HEAD_DIM):
            o_ref[0, h] = y[:, h * HEAD_DIM:(h + 1) * HEAD_DIM].astype(BF16)

    _first_or_later(step)


def _qkv_na(x, g, w, batch, seq):
    t = x.shape[0]
    tiles_per_seq = seq // QKV_TM
    colscale = jnp.concatenate([jnp.full((1, W_NA), Q_SCALE_NA, F32), jnp.ones((1, 2 * W_NA), F32)], axis=-1)
    return pl.pallas_call(
        _qkv_na_kernel,
        name="qkv_na",
        grid=(t // QKV_TM, 3 * W_NA // QKV_TN),
        in_specs=[
            pl.BlockSpec((QKV_TM, D_MODEL), lambda i, j: (i, 0)),
            pl.BlockSpec((1, D_MODEL), lambda i, j: (0, 0)),
            pl.BlockSpec((D_MODEL, QKV_TN), lambda i, j: (0, j)),
            pl.BlockSpec((1, QKV_TN), lambda i, j: (0, j)),
        ],
        out_specs=pl.BlockSpec((1, QKV_TN // HEAD_DIM, QKV_TM, HEAD_DIM),
                               lambda i, j: (i // tiles_per_seq, j, i % tiles_per_seq, 0)),
        out_shape=jax.ShapeDtypeStruct((batch, 3 * N_HEADS_NA, seq, HEAD_DIM), BF16),
        scratch_shapes=[pltpu.VMEM((QKV_TM, D_MODEL), BF16)],
        compiler_params=_params("parallel", "arbitrary"),
    )(x, g, w, colscale)


def _qkv_dil_kernel(x_ref, g_ref, w_ref, cos_ref, sin_ref, fa_ref, fb_ref, u_ref, y_ref, y4_ref):
    j = pl.program_id(1)
    rotary = j < 2 * _DIL_STEPS_PER_PART
    scale = jnp.where(j < _DIL_STEPS_PER_PART, Q_SCALE_DIL, 1.0).astype(F32)

    def step(first):
        u = _normed_input(first, x_ref, g_ref, u_ref)
        cos = cos_ref[...]
        sin = sin_ref[...]
        for pair in range(QKV_TN // _DOT_N):
            y2 = jnp.dot(u, w_ref[:, pair * _DOT_N:(pair + 1) * _DOT_N], preferred_element_type=F32)
            for half in range(_DOT_N // HEAD_DIM):
                h = pair * (_DOT_N // HEAD_DIM) + half
                y = y2[:, half * HEAD_DIM:(half + 1) * HEAD_DIM]
                y_ref[h] = jnp.where(rotary, y * cos + pltpu.roll(y, HEAD_DIM // 2, axis=1) * sin, y) * scale
                for r in range(FOLD):
                    y4 = y_ref[h, pl.ds(r, QKV_TM // FOLD, stride=FOLD), :]
                    fa_ref[0, h, r] = y4.astype(BF16)
                    y4_ref[h, r] = y4
                    for c in range(FOLD):
                        fb_ref[0, h, r + FOLD * c] = (
                            y4_ref[h, r, pl.ds(c, QKV_TM // FOLD ** 2, stride=FOLD), :].astype(BF16))

    _first_or_later(step)


def _qkv_dil(x, g, w, cos_full, sin_signed, batch, seq):
    t = x.shape[0]
    tiles_per_seq = seq // QKV_TM

    def fold_spec(f):
        return pl.BlockSpec((1, QKV_TN // HEAD_DIM, f, QKV_TM // f, HEAD_DIM),
                            lambda i, j: (i // tiles_per_seq, j, 0, i % tiles_per_seq, 0))

    return pl.pallas_call(
        _qkv_dil_kernel,
        name="qkv_dil",
        grid=(t // QKV_TM, _DIL_COL_STEPS),
        in_specs=[
            pl.BlockSpec((QKV_TM, D_MODEL), lambda i, j: (i, 0)),
            pl.BlockSpec((1, D_MODEL), lambda i, j: (0, 0)),
            pl.BlockSpec((D_MODEL, QKV_TN), lambda i, j: (0, j)),
            pl.BlockSpec((QKV_TM, HEAD_DIM), lambda i, j: (i % tiles_per_seq, 0)),
            pl.BlockSpec((QKV_TM, HEAD_DIM), lambda i, j: (i % tiles_per_seq, 0)),
        ],
        out_specs=[fold_spec(f) for f in (FOLD, FOLD ** 2)],
        out_shape=[jax.ShapeDtypeStruct((batch, 3 * N_HEADS_DIL, f, seq // f, HEAD_DIM), BF16)
                   for f in (FOLD, FOLD ** 2)],
        scratch_shapes=[pltpu.VMEM((QKV_TM, D_MODEL), BF16),
                        pltpu.VMEM((QKV_TN // HEAD_DIM, QKV_TM, HEAD_DIM), F32),
                        pltpu.VMEM((QKV_TN // HEAD_DIM, FOLD, QKV_TM // FOLD, HEAD_DIM), F32)],
        compiler_params=_params("parallel", "arbitrary"),
    )(x, g, w, cos_full, sin_signed)


def _rope_tables(seq):
    inv = jnp.float32(ROPE_THETA) ** (-jnp.arange(0, HEAD_DIM, 2, dtype=F32) / HEAD_DIM)
    ang = jnp.arange(seq, dtype=F32)[:, None] * inv[None, :]
    cos, sin = jnp.cos(ang), jnp.sin(ang)
    return jnp.concatenate([cos, cos], axis=-1), jnp.concatenate([-sin, sin], axis=-1)


def _na_tile_key_row_start(i, rows):
    return jnp.clip(i * NA_QROWS - NA_ROWS // 2, 0, rows - NA_KROWS)


def _na_kernel(q_ref, k_ref, v_ref, bias_ref, g_ref, o_ref, *, rows):
    i = pl.program_id(2)
    start = pl.multiple_of(_na_tile_key_row_start(i, rows) * GRID_W, GRID_W)
    nk = NA_KROWS * GRID_W
    k = k_ref[0, 0, pl.ds(start, nk), :]
    v = v_ref[0, 0, pl.ds(start, nk), :]
    sub = lambda n: slice(n * NA_SUB_ROWS, (n + 1) * NA_SUB_ROWS)

    def score(n):
        return lax.dot_general(q_ref[0, 0, sub(n), :], k, (((1,), (1,)), ((), ())),
                               preferred_element_type=F32) + bias_ref[0, 0, sub(n), :]

    def softmax(s):
        e = jnp.exp(s - jnp.max(s, axis=-1, keepdims=True))
        return e.astype(BF16), jnp.sum(e, axis=-1, keepdims=True)

    def finish(n, prob):
        e, den = prob
        o = jnp.dot(e, v, preferred_element_type=F32) / den
        o_ref[0, 0, sub(n), :] = _rms(o, g_ref[0]).astype(BF16)

    _skewed(NA_QROWS * GRID_W // NA_SUB_ROWS, 1, score, softmax, finish)


def _na_bias_table(rpb, rows):
    n_tiles = rows // NA_QROWS
    dr_idx = np.zeros((3, NA_QROWS, NA_KROWS), np.int32)
    row_ok = np.zeros((3, NA_QROWS, NA_KROWS), bool)
    for cls, tile in enumerate((0, 1, n_tiles - 1)):
        ks = int(np.clip(tile * NA_QROWS - NA_ROWS // 2, 0, rows - NA_KROWS))
        r = tile * NA_QROWS + np.arange(NA_QROWS)
        rs = np.clip(r - NA_ROWS // 2, 0, rows - NA_ROWS)
        kr = ks + np.arange(NA_KROWS)
        row_ok[cls] = (kr[None, :] >= rs[:, None]) & (kr[None, :] < rs[:, None] + NA_ROWS)
        dr_idx[cls] = np.clip(kr[None, :] - r[:, None] + NA_ROWS - 1, 0, 2 * NA_ROWS - 2)
    c = np.arange(GRID_W)
    qs = np.clip(c - NA_COLS // 2, 0, GRID_W - NA_COLS)
    col_ok = (c[None, :] >= qs[:, None]) & (c[None, :] < qs[:, None] + NA_COLS)
    n_dr, n_dc = 2 * NA_ROWS - 1, 2 * NA_COLS - 1
    period = 2 * GRID_W
    v = jnp.concatenate([rpb[..., NA_COLS - 1:], jnp.zeros((N_HEADS_NA, n_dr, period - n_dc), F32),
                         rpb[..., :NA_COLS - 1]], axis=-1).astype(F32)
    toe = jnp.tile(v, (1, 1, GRID_W))[..., :GRID_W * (period - 1)]
    toe = toe.reshape(N_HEADS_NA, n_dr, GRID_W, period - 1)[..., :GRID_W]
    slabs = jnp.where(col_ok[None, None], toe, NEG)
    slabs = jnp.concatenate([slabs, jnp.full((N_HEADS_NA, 1, GRID_W, GRID_W), NEG, F32)], axis=1)
    slab_idx = np.where(row_ok, dr_idx, n_dr)
    return pl.pallas_call(
        functools.partial(_na_bias_kernel, slab_idx=slab_idx),
        name="na_bias",
        grid=(N_HEADS_NA,),
        in_specs=[pl.BlockSpec((1, n_dr + 1, GRID_W, GRID_W), lambda h: (h, 0, 0, 0))],
        out_specs=pl.BlockSpec((3, 1, NA_QROWS * GRID_W, NA_KROWS * GRID_W), lambda h: (0, h, 0, 0)),
        out_shape=jax.ShapeDtypeStruct((3, N_HEADS_NA, NA_QROWS * GRID_W, NA_KROWS * GRID_W), F32),
        compiler_params=_params("parallel"),
    )(slabs)


def _na_bias_kernel(slabs_ref, o_ref, *, slab_idx):
    n_cls, n_q, n_k = slab_idx.shape
    for cls in range(n_cls):
        for rq in range(n_q):
            row = jnp.concatenate([slabs_ref[0, int(slab_idx[cls, rq, rk])] for rk in range(n_k)], axis=-1)
            o_ref[cls, 0, rq * GRID_W:(rq + 1) * GRID_W, :] = row


def _na(qkv, bias, head_g, batch, seq):
    rows = seq // GRID_W
    n_tiles = rows // NA_QROWS
    tq = NA_QROWS * GRID_W

    def bias_map(b, h, i):
        return (jnp.where(i == 0, 0, jnp.where(i == n_tiles - 1, 2, 1)), h, 0, 0)

    return pl.pallas_call(
        functools.partial(_na_kernel, rows=rows),
        name="na",
        grid=(batch, N_HEADS_NA, n_tiles),
        in_specs=[
            pl.BlockSpec((1, 1, tq, HEAD_DIM), lambda b, h, i: (b, h, i, 0)),
            pl.BlockSpec((1, 1, seq, HEAD_DIM), lambda b, h, i: (b, N_HEADS_NA + h, 0, 0)),
            pl.BlockSpec((1, 1, seq, HEAD_DIM), lambda b, h, i: (b, 2 * N_HEADS_NA + h, 0, 0)),
            pl.BlockSpec((1, 1, tq, NA_KROWS * GRID_W), bias_map),
            pl.BlockSpec((1, 1, HEAD_DIM), lambda b, h, i: (h, 0, 0)),
        ],
        out_specs=pl.BlockSpec((1, 1, tq, HEAD_DIM), lambda b, h, i: (b, h, i, 0)),
        out_shape=jax.ShapeDtypeStruct((batch, N_HEADS_NA, seq, HEAD_DIM), BF16),
        compiler_params=_params("parallel", "parallel", "arbitrary"),
    )(qkv, qkv, qkv, bias, head_g)


_DIL_WIN = DIL_TQ + 2 * DIL_HALF
_BRANCH_FOLD = tuple(max(dil, FOLD) for (_, dil) in DIL_PAIRS)


def _dil_branch(q_ref, k_ref, v_ref, mask_ref, os_ref, ls_ref, slot, part, *, dil, sub, length):
    fold = dil * sub
    qn, kn = DIL_TQ // sub, _DIL_WIN // sub
    n_i = length // qn
    n_p = n_i // DIL_PARTS
    row0 = part * (length // DIL_PARTS)

    def gather(ref, p, start, size):
        parts = [ref[0, 0, p + dil * c, pl.ds(start, size), :] for c in range(sub)]
        return parts[0] if sub == 1 else jnp.concatenate(parts, axis=0)

    def group(g, carry):
        tiles = []
        for n in range(DIL_GROUP):
            t = g * DIL_GROUP + n
            p = t // n_p
            i = part * n_p + t % n_p
            q0 = pl.multiple_of(i * qn, qn)
            k0 = pl.multiple_of(jnp.clip(q0 - DIL_HALF // sub, 0, length - kn), DIL_HALF // sub)
            edge = jnp.where(i == 0, 0, jnp.where(i == n_i - 1, 2, 1))
            tiles.append((p, q0, k0, edge))

        def score(n):
            p, q0, k0, edge = tiles[n]
            return lax.dot_general(gather(q_ref, p, q0, qn), gather(k_ref, p, k0, kn), (((1,), (1,)), ((), ())),
                                   preferred_element_type=F32) + mask_ref[edge]

        def softmax(s):
            m = jnp.max(s, axis=-1, keepdims=True)
            e = jnp.exp2(s - m)
            return m, e.astype(BF16), jnp.sum(e, axis=-1, keepdims=True)

        def finish(n, prob):
            p, q0, k0, edge = tiles[n]
            m, e, den = prob
            o = jnp.dot(e, gather(v_ref, p, k0, kn), preferred_element_type=F32) / den
            lse = jnp.broadcast_to(m + jnp.log2(den), (DIL_TQ, HEAD_DIM))
            for c in range(sub):
                rows = pl.ds(fold * (q0 - row0) + p + dil * c, qn, stride=fold)
                os_ref[slot, rows, :] = o[c * qn:(c + 1) * qn]
                ls_ref[slot, rows, :] = lse[c * qn:(c + 1) * qn]

        _skewed(DIL_GROUP, DIL_LAG, score, softmax, finish)
        return carry

    lax.fori_loop(0, dil * n_p // DIL_GROUP, group, 0)


def _dilated_kernel(qa_ref, ka_ref, va_ref, qb_ref, kb_ref, vb_ref, m0_ref, m1_ref, m2_ref, g_ref, o_ref,
                    os_ref, ls_ref, *, seq):
    part = pl.program_id(2)
    stored = {FOLD: (qa_ref, ka_ref, va_ref), FOLD ** 2: (qb_ref, kb_ref, vb_ref)}
    for slot, ((_, dil), fold, mask_ref) in enumerate(zip(DIL_PAIRS, _BRANCH_FOLD, (m0_ref, m1_ref, m2_ref))):
        _dil_branch(*stored[fold], mask_ref, os_ref, ls_ref, slot, part, dil=dil, sub=fold // dil,
                    length=seq // fold)

    def merge(c, carry):
        rows = pl.ds(pl.multiple_of(c * DIL_MERGE_ROWS, DIL_MERGE_ROWS), DIL_MERGE_ROWS)
        lses = [ls_ref[b, rows, :] for b in range(len(DIL_PAIRS))]
        mx = functools.reduce(jnp.maximum, lses)
        ws = [jnp.exp2(l - mx) for l in lses]
        num = functools.reduce(jnp.add, [w * os_ref[b, rows, :] for b, w in enumerate(ws)])
        o = num / functools.reduce(jnp.add, ws)
        o_ref[0, 0, rows, :] = _rms(o, g_ref[0]).astype(BF16)
        return carry

    lax.fori_loop(0, seq // DIL_PARTS // DIL_MERGE_ROWS, merge, 0)


def _dil_mask_table(sub, length):
    qn, kn = DIL_TQ // sub, _DIL_WIN // sub
    n_i = length // qn
    c = np.arange(sub)[:, None]
    out = []
    for i in (0, 1, n_i - 1):
        k0 = int(np.clip(i * qn - DIL_HALF // sub, 0, length - kn))
        qpos = ((i * qn + np.arange(qn))[None, :] * sub + c).reshape(-1)
        kpos = ((k0 + np.arange(kn))[None, :] * sub + c).reshape(-1)
        out.append(np.where(np.abs(kpos[None, :] - qpos[:, None]) <= DIL_HALF, 0.0, NEG))
    return jnp.asarray(np.stack(out), F32)


def _dilated(fa, fb, head_g, batch, seq):
    masks = []
    for (window, dil), fold in zip(DIL_PAIRS, _BRANCH_FOLD):
        sub, length = fold // dil, seq // fold
        assert window // (2 * dil) == DIL_HALF and fold % dil == 0 and DIL_TQ % sub == 0
        n_p = length // (DIL_TQ // sub) // DIL_PARTS
        assert n_p * DIL_PARTS * (DIL_TQ // sub) == length and (dil * n_p) % DIL_GROUP == 0 and n_p >= 1
        masks.append(_dil_mask_table(sub, length))

    def stored(f, first):
        return pl.BlockSpec((1, 1, f, seq // f, HEAD_DIM), lambda b, h, part: (b, first + h, 0, 0, 0))

    part_rows = seq // DIL_PARTS
    return pl.pallas_call(
        functools.partial(_dilated_kernel, seq=seq),
        name="dilated",
        grid=(batch, N_HEADS_DIL, DIL_PARTS),
        in_specs=[stored(f, first) for f in (FOLD, FOLD ** 2) for first in (0, N_HEADS_DIL, 2 * N_HEADS_DIL)]
        + [pl.BlockSpec((3, DIL_TQ, _DIL_WIN), lambda b, h, part: (0, 0, 0))] * len(DIL_PAIRS)
        + [pl.BlockSpec((1, 1, HEAD_DIM), lambda b, h, part: (N_HEADS_NA + h, 0, 0))],
        out_specs=pl.BlockSpec((1, 1, part_rows, HEAD_DIM), lambda b, h, part: (b, h, part, 0)),
        out_shape=jax.ShapeDtypeStruct((batch, N_HEADS_DIL, seq, HEAD_DIM), BF16),
        scratch_shapes=[pltpu.VMEM((len(DIL_PAIRS), part_rows, HEAD_DIM), F32)] * 2,
        compiler_params=_params("parallel", "parallel", "arbitrary"),
    )(fa, fa, fa, fb, fb, fb, *masks, head_g)


def _mix_out_kernel(ona_ref, odil_ref, wo_ref, h_ref, postg_ref, o_ref):
    heads = [ona_ref[0, h] for h in range(N_HEADS_NA)] + [odil_ref[0, h] for h in range(N_HEADS_DIL)]
    m = jnp.dot(jnp.concatenate(heads, axis=-1), wo_ref[...], preferred_element_type=F32)
    o_ref[...] = h_ref[...] + _rms(m, postg_ref[...])


def _mix_out(o_na, o_dil, w_o, h, post_g, seq):
    t = h.shape[0]
    tiles_per_seq = seq // MIX_TM
    heads = lambda n: pl.BlockSpec((1, n, MIX_TM, HEAD_DIM), lambda i: (i // tiles_per_seq, 0, i % tiles_per_seq, 0))
    row = pl.BlockSpec((MIX_TM, D_MODEL), lambda i: (i, 0))
    const = lambda shape: pl.BlockSpec(shape, lambda i: (0, 0))
    return pl.pallas_call(
        _mix_out_kernel,
        name="mix_out",
        grid=(t // MIX_TM,),
        in_specs=[heads(N_HEADS_NA), heads(N_HEADS_DIL), const((D_MODEL, D_MODEL)), row, const((1, D_MODEL))],
        out_specs=row,
        out_shape=jax.ShapeDtypeStruct((t, D_MODEL), F32),
        compiler_params=_params("parallel"),
    )(o_na, o_dil, w_o, h, post_g)


def _ple_kernel(h_ref, p_ref, preg_ref, wg_ref, wp_ref, postg_ref, o_ref):
    h = h_ref[...]
    u = _rms(h, preg_ref[...]).astype(BF16)
    gate = jax.nn.sigmoid(jnp.dot(u, wg_ref[...], preferred_element_type=F32))
    emb = jnp.dot(p_ref[...].astype(BF16), wp_ref[...], preferred_element_type=F32)
    o_ref[...] = h + _rms(gate * emb, postg_ref[...])


def _ple(h, p, pre_g, w_gate, w_proj, post_g):
    t = h.shape[0]
    row = lambda width: pl.BlockSpec((PLE_TM, width), lambda i: (i, 0))
    const = lambda shape: pl.BlockSpec(shape, lambda i: (0, 0))
    return pl.pallas_call(
        _ple_kernel,
        name="ple",
        grid=(t // PLE_TM,),
        in_specs=[row(D_MODEL), row(PLE_DIM), const((1, D_MODEL)), const((D_MODEL, D_MODEL)),
                  const((PLE_DIM, D_MODEL)), const((1, D_MODEL))],
        out_specs=row(D_MODEL),
        out_shape=jax.ShapeDtypeStruct((t, D_MODEL), F32),
        compiler_params=_params("parallel"),
    )(h, p, pre_g, w_gate, w_proj, post_g)


def kernel(x, p, ffn1_pre_g, ffn1_w_gate, ffn1_w_up, ffn1_w_down, ffn1_post_g, mix_pre_g, w_qkv, na_rpb, out_g, w_o, mix_post_g, ffn2_pre_g, ffn2_w_gate, ffn2_w_up, ffn2_w_down, ffn2_post_g, ple_pre_g, w_ple_gate, w_ple_proj, ple_post_g):
    batch, seq, d_model = x.shape
    depth = p.shape[0]
    assert d_model == D_MODEL and seq % (GRID_W * NA_KROWS) == 0
    tokens = batch * seq
    rows = seq // GRID_W
    cos_full, sin_signed = _rope_tables(seq)
    gain = lambda g: g.reshape(1, D_MODEL)
    w16 = lambda w: w.astype(BF16)

    h = x.reshape(tokens, D_MODEL)
    for i in range(depth):
        h = _ffn(h, gain(ffn1_pre_g[i]), w16(ffn1_w_gate[i]), w16(ffn1_w_up[i]), w16(ffn1_w_down[i]),
                 gain(ffn1_post_g[i]))
        qkv_na = _qkv_na(h, gain(mix_pre_g[i]), w16(w_qkv[i][:, :3 * W_NA]), batch, seq)
        fa, fb = _qkv_dil(h, gain(mix_pre_g[i]), w16(w_qkv[i][:, 3 * W_NA:]), cos_full, sin_signed, batch, seq)
        head_g = out_g[i].reshape(N_HEADS, 1, HEAD_DIM)
        o_na = _na(qkv_na, _na_bias_table(na_rpb[i], rows), head_g, batch, seq)
        o_dil = _dilated(fa, fb, head_g, batch, seq)
        h = _mix_out(o_na, o_dil, w16(w_o[i]), h, gain(mix_post_g[i]), seq)
        h = _ffn(h, gain(ffn2_pre_g[i]), w16(ffn2_w_gate[i]), w16(ffn2_w_up[i]), w16(ffn2_w_down[i]),
                 gain(ffn2_post_g[i]))
        h = _ple(h, p[i].reshape(tokens, PLE_DIM), gain(ple_pre_g[i]), w16(w_ple_gate[i]), w16(w_ple_proj[i]),
                 gain(ple_post_g[i]))
    return h.reshape(batch, seq, D_MODEL)
```

```python
import functools
import math

import jax
import jax.numpy as jnp
import numpy as np
from jax import lax
from jax.experimental import pallas as pl
from jax.experimental.pallas import tpu as pltpu

D_MODEL = 2048
D_FF = 5632
HEAD_DIM = 128
N_HEADS = 16
N_HEADS_NA = 4
N_HEADS_DIL = 12
W_NA = N_HEADS_NA * HEAD_DIM
W_DIL = N_HEADS_DIL * HEAD_DIM
GRID_W = 64
NA_ROWS = 8
NA_COLS = 16
DIL_PAIRS = ((128, 1), (512, 4), (2048, 16))
PLE_DIM = 256
ROPE_THETA = 10000.0
EPS = 1e-6
NEG = -1e30
SCALE = HEAD_DIM ** -0.5

F32 = jnp.float32
BF16 = jnp.bfloat16

VMEM_LIMIT_BYTES = 56 * 1024 * 1024

FFN_TM = 512
FFN_TF = 512
QKV_TM = 1024
QKV_TN = 768
FOLD = 4
NA_QROWS = 8
NA_KROWS = 16
NA_SUB_ROWS = 128
DIL_TQ = 128
DIL_GROUP = 32
DIL_LAG = 2
DIL_HALF = 64
DIL_PARTS = 2
DIL_MERGE_ROWS = 1024
MIX_TM = 512
PLE_TM = 512


def _rms(x, g):
    return x * lax.rsqrt(jnp.mean(x * x, axis=-1, keepdims=True) + EPS) * g


def _params(*sem):
    return pltpu.CompilerParams(dimension_semantics=sem, vmem_limit_bytes=VMEM_LIMIT_BYTES)


def _normed_input(first, x_ref, g_ref, u_ref):
    if not first:
        return u_ref[...]
    u = _rms(x_ref[...], g_ref[...]).astype(BF16)
    u_ref[...] = u
    return u


def _first_or_later(step):
    j = pl.program_id(1)
    pl.when(j == 0)(functools.partial(step, True))
    pl.when(j > 0)(functools.partial(step, False))


def _skewed(n, lag, score, softmax, finish):
    scores, probs = {}, {}
    for t in range(n + 2 * lag):
        if t < n:
            scores[t] = score(t)
        if 0 <= t - lag < n:
            probs[t - lag] = softmax(scores.pop(t - lag))
        if 0 <= t - 2 * lag < n:
            finish(t - 2 * lag, probs.pop(t - 2 * lag))


def _ffn_kernel(x_ref, pre_g_ref, wg_ref, wu_ref, wd_ref, post_g_ref, o_ref, u_ref):
    j = pl.program_id(1)
    last = pl.num_programs(1) - 1

    def step(first, final):
        u = _normed_input(first, x_ref, pre_g_ref, u_ref)
        g = jnp.dot(u, wg_ref[...], preferred_element_type=F32)
        v = jnp.dot(u, wu_ref[...], preferred_element_type=F32)
        mid = (g * jax.nn.sigmoid(g) * v).astype(BF16)
        acc = jnp.dot(mid, wd_ref[...], preferred_element_type=F32)
        if not first:
            acc = o_ref[...] + acc
        o_ref[...] = x_ref[...] + 0.5 * _rms(acc, post_g_ref[...]) if final else acc

    pl.when(j == 0)(functools.partial(step, True, False))
    pl.when(jnp.logical_and(j > 0, j < last))(functools.partial(step, False, False))
    pl.when(j == last)(functools.partial(step, False, True))


def _ffn(x, pre_g, w_gate, w_up, w_down, post_g):
    t = x.shape[0]
    assert D_FF // FFN_TF >= 2
    return pl.pallas_call(
        _ffn_kernel,
        name="ffn",
        grid=(t // FFN_TM, D_FF // FFN_TF),
        in_specs=[
            pl.BlockSpec((FFN_TM, D_MODEL), lambda i, j: (i, 0)),
            pl.BlockSpec((1, D_MODEL), lambda i, j: (0, 0)),
            pl.BlockSpec((D_MODEL, FFN_TF), lambda i, j: (0, j)),
            pl.BlockSpec((D_MODEL, FFN_TF), lambda i, j: (0, j)),
            pl.BlockSpec((FFN_TF, D_MODEL), lambda i, j: (j, 0)),
            pl.BlockSpec((1, D_MODEL), lambda i, j: (0, 0)),
        ],
        out_specs=pl.BlockSpec((FFN_TM, D_MODEL), lambda i, j: (i, 0)),
        out_shape=jax.ShapeDtypeStruct((t, D_MODEL), F32),
        scratch_shapes=[pltpu.VMEM((FFN_TM, D_MODEL), BF16)],
        compiler_params=_params("parallel", "arbitrary"),
    )(x, pre_g, w_gate, w_up, w_down, post_g)


Q_SCALE_NA = SCALE
Q_SCALE_DIL = SCALE * math.log2(math.e)
_DIL_COL_STEPS = 3 * W_DIL // QKV_TN
_DIL_STEPS_PER_PART = W_DIL // QKV_TN
_DOT_N = 2 * HEAD_DIM


def _qkv_na_kernel(x_ref, g_ref, w_ref, colscale_ref, o_ref, u_ref):
    def step(first):
        u = _normed_input(first, x_ref, g_ref, u_ref)
        y = jnp.dot(u, w_ref[...], preferred_element_type=F32) * colscale_ref[...]
        for h in range(QKV_TN // HEAD_DIM):
            o_ref[0, h] = y[:, h * HEAD_DIM:(h + 1) * HEAD_DIM].astype(BF16)

    _first_or_later(step)


def _qkv_na(x, g, w, batch, seq):
    t = x.shape[0]
    tiles_per_seq = seq // QKV_TM
    colscale = jnp.concatenate([jnp.full((1, W_NA), Q_SCALE_NA, F32), jnp.ones((1, 2 * W_NA), F32)], axis=-1)
    return pl.pallas_call(
        _qkv_na_kernel,
        name="qkv_na",
        grid=(t // QKV_TM, 3 * W_NA // QKV_TN),
        in_specs=[
            pl.BlockSpec((QKV_TM, D_MODEL), lambda i, j: (i, 0)),
            pl.BlockSpec((1, D_MODEL), lambda i, j: (0, 0)),
            pl.BlockSpec((D_MODEL, QKV_TN), lambda i, j: (0, j)),
            pl.BlockSpec((1, QKV_TN), lambda i, j: (0, j)),
        ],
        out_specs=[pl.BlockSpec((1, QKV_TN // HEAD_DIM, QKV_TM, HEAD_DIM),
                                lambda i, j: (i // tiles_per_seq, j, i % tiles_per_seq, 0)),
                   pl.BlockSpec((QKV_TM, D_MODEL), lambda i, j: (i, 0))],
        out_shape=[jax.ShapeDtypeStruct((batch, 3 * N_HEADS_NA, seq, HEAD_DIM), BF16),
                   jax.ShapeDtypeStruct((t, D_MODEL), BF16)],
        compiler_params=_params("parallel", "arbitrary"),
    )(x, g, w, colscale)


def _qkv_dil_kernel(u_ref, w_ref, cos_ref, sin_ref, fa_ref, fb_ref, y_ref, y4_ref):
    j = pl.program_id(1)
    rotary = j < 2 * _DIL_STEPS_PER_PART
    scale = jnp.where(j < _DIL_STEPS_PER_PART, Q_SCALE_DIL, 1.0).astype(F32)

    u = u_ref[...]
    cos = cos_ref[...]
    sin = sin_ref[...]
    for pair in range(QKV_TN // _DOT_N):
        y2 = jnp.dot(u, w_ref[:, pair * _DOT_N:(pair + 1) * _DOT_N], preferred_element_type=F32)
        for half in range(_DOT_N // HEAD_DIM):
            h = pair * (_DOT_N // HEAD_DIM) + half
            y = y2[:, half * HEAD_DIM:(half + 1) * HEAD_DIM]
            y_ref[h] = jnp.where(rotary, y * cos + pltpu.roll(y, HEAD_DIM // 2, axis=1) * sin, y) * scale
            for r in range(FOLD):
                y4 = y_ref[h, pl.ds(r, QKV_TM // FOLD, stride=FOLD), :]
                fa_ref[0, h, r] = y4.astype(BF16)
                y4_ref[h, r] = y4
                for c in range(FOLD):
                    fb_ref[0, h, r + FOLD * c] = (
                        y4_ref[h, r, pl.ds(c, QKV_TM // FOLD ** 2, stride=FOLD), :].astype(BF16))


def _qkv_dil(u, w, cos_full, sin_signed, batch, seq):
    t = u.shape[0]
    tiles_per_seq = seq // QKV_TM

    def fold_spec(f):
        return pl.BlockSpec((1, QKV_TN // HEAD_DIM, f, QKV_TM // f, HEAD_DIM),
                            lambda i, j: (i // tiles_per_seq, j, 0, i % tiles_per_seq, 0))

    return pl.pallas_call(
        _qkv_dil_kernel,
        name="qkv_dil",
        grid=(t // QKV_TM, _DIL_COL_STEPS),
        in_specs=[
            pl.BlockSpec((QKV_TM, D_MODEL), lambda i, j: (i, 0)),
            pl.BlockSpec((D_MODEL, QKV_TN), lambda i, j: (0, j)),
            pl.BlockSpec((QKV_TM, HEAD_DIM), lambda i, j: (i % tiles_per_seq, 0)),
            pl.BlockSpec((QKV_TM, HEAD_DIM), lambda i, j: (i % tiles_per_seq, 0)),
        ],
        out_specs=[fold_spec(f) for f in (FOLD, FOLD ** 2)],
        out_shape=[jax.ShapeDtypeStruct((batch, 3 * N_HEADS_DIL, f, seq // f, HEAD_DIM), BF16)
                   for f in (FOLD, FOLD ** 2)],
        scratch_shapes=[pltpu.VMEM((QKV_TN // HEAD_DIM, QKV_TM, HEAD_DIM), F32),
                        pltpu.VMEM((QKV_TN // HEAD_DIM, FOLD, QKV_TM // FOLD, HEAD_DIM), F32)],
        compiler_params=_params("parallel", "arbitrary"),
    )(u, w, cos_full, sin_signed)


def _rope_tables(seq):
    inv = jnp.float32(ROPE_THETA) ** (-jnp.arange(0, HEAD_DIM, 2, dtype=F32) / HEAD_DIM)
    ang = jnp.arange(seq, dtype=F32)[:, None] * inv[None, :]
    cos, sin = jnp.cos(ang), jnp.sin(ang)
    return jnp.concatenate([cos, cos], axis=-1), jnp.concatenate([-sin, sin], axis=-1)


def _na_tile_key_row_start(i, rows):
    return jnp.clip(i * NA_QROWS - NA_ROWS // 2, 0, rows - NA_KROWS)


def _na_kernel(q_ref, k_ref, v_ref, bias_ref, g_ref, o_ref, *, rows):
    i = pl.program_id(2)
    start = pl.multiple_of(_na_tile_key_row_start(i, rows) * GRID_W, GRID_W)
    nk = NA_KROWS * GRID_W
    k = k_ref[0, 0, pl.ds(start, nk), :]
    v = v_ref[0, 0, pl.ds(start, nk), :]
    sub = lambda n: slice(n * NA_SUB_ROWS, (n + 1) * NA_SUB_ROWS)

    def score(n):
        return lax.dot_general(q_ref[0, 0, sub(n), :], k, (((1,), (1,)), ((), ())),
                               preferred_element_type=F32) + bias_ref[0, 0, sub(n), :]

    def softmax(s):
        e = jnp.exp(s - jnp.max(s, axis=-1, keepdims=True))
        return e.astype(BF16), jnp.sum(e, axis=-1, keepdims=True)

    def finish(n, prob):
        e, den = prob
        o = jnp.dot(e, v, preferred_element_type=F32) / den
        o_ref[0, 0, sub(n), :] = _rms(o, g_ref[0]).astype(BF16)

    _skewed(NA_QROWS * GRID_W // NA_SUB_ROWS, 1, score, softmax, finish)


def _na_bias_table(rpb, rows):
    n_tiles = rows // NA_QROWS
    dr_idx = np.zeros((3, NA_QROWS, NA_KROWS), np.int32)
    row_ok = np.zeros((3, NA_QROWS, NA_KROWS), bool)
    for cls, tile in enumerate((0, 1, n_tiles - 1)):
        ks = int(np.clip(tile * NA_QROWS - NA_ROWS // 2, 0, rows - NA_KROWS))
        r = tile * NA_QROWS + np.arange(NA_QROWS)
        rs = np.clip(r - NA_ROWS // 2, 0, rows - NA_ROWS)
        kr = ks + np.arange(NA_KROWS)
        row_ok[cls] = (kr[None, :] >= rs[:, None]) & (kr[None, :] < rs[:, None] + NA_ROWS)
        dr_idx[cls] = np.clip(kr[None, :] - r[:, None] + NA_ROWS - 1, 0, 2 * NA_ROWS - 2)
    c = np.arange(GRID_W)
    qs = np.clip(c - NA_COLS // 2, 0, GRID_W - NA_COLS)
    col_ok = (c[None, :] >= qs[:, None]) & (c[None, :] < qs[:, None] + NA_COLS)
    n_dr, n_dc = 2 * NA_ROWS - 1, 2 * NA_COLS - 1
    period = 2 * GRID_W
    v = jnp.concatenate([rpb[..., NA_COLS - 1:], jnp.zeros((N_HEADS_NA, n_dr, period - n_dc), F32),
                         rpb[..., :NA_COLS - 1]], axis=-1).astype(F32)
    toe = jnp.tile(v, (1, 1, GRID_W))[..., :GRID_W * (period - 1)]
    toe = toe.reshape(N_HEADS_NA, n_dr, GRID_W, period - 1)[..., :GRID_W]
    slabs = jnp.where(col_ok[None, None], toe, NEG)
    slabs = jnp.concatenate([slabs, jnp.full((N_HEADS_NA, 1, GRID_W, GRID_W), NEG, F32)], axis=1)
    slab_idx = np.where(row_ok, dr_idx, n_dr)
    return pl.pallas_call(
        functools.partial(_na_bias_kernel, slab_idx=slab_idx),
        name="na_bias",
        grid=(N_HEADS_NA,),
        in_specs=[pl.BlockSpec((1, n_dr + 1, GRID_W, GRID_W), lambda h: (h, 0, 0, 0))],
        out_specs=pl.BlockSpec((3, 1, NA_QROWS * GRID_W, NA_KROWS * GRID_W), lambda h: (0, h, 0, 0)),
        out_shape=jax.ShapeDtypeStruct((3, N_HEADS_NA, NA_QROWS * GRID_W, NA_KROWS * GRID_W), F32),
        compiler_params=_params("parallel"),
    )(slabs)


def _na_bias_kernel(slabs_ref, o_ref, *, slab_idx):
    n_cls, n_q, n_k = slab_idx.shape
    for cls in range(n_cls):
        for rq in range(n_q):
            row = jnp.concatenate([slabs_ref[0, int(slab_idx[cls, rq, rk])] for rk in range(n_k)], axis=-1)
            o_ref[cls, 0, rq * GRID_W:(rq + 1) * GRID_W, :] = row


def _na(qkv, bias, head_g, batch, seq):
    rows = seq // GRID_W
    n_tiles = rows // NA_QROWS
    tq = NA_QROWS * GRID_W

    def bias_map(b, h, i):
        return (jnp.where(i == 0, 0, jnp.where(i == n_tiles - 1, 2, 1)), h, 0, 0)

    return pl.pallas_call(
        functools.partial(_na_kernel, rows=rows),
        name="na",
        grid=(batch, N_HEADS_NA, n_tiles),
        in_specs=[
            pl.BlockSpec((1, 1, tq, HEAD_DIM), lambda b, h, i: (b, h, i, 0)),
            pl.BlockSpec((1, 1, seq, HEAD_DIM), lambda b, h, i: (b, N_HEADS_NA + h, 0, 0)),
            pl.BlockSpec((1, 1, seq, HEAD_DIM), lambda b, h, i: (b, 2 * N_HEADS_NA + h, 0, 0)),
            pl.BlockSpec((1, 1, tq, NA_KROWS * GRID_W), bias_map),
            pl.BlockSpec((1, 1, HEAD_DIM), lambda b, h, i: (h, 0, 0)),
        ],
        out_specs=pl.BlockSpec((1, 1, tq, HEAD_DIM), lambda b, h, i: (b, h, i, 0)),
        out_shape=jax.ShapeDtypeStruct((batch, N_HEADS_NA, seq, HEAD_DIM), BF16),
        compiler_params=_params("parallel", "parallel", "arbitrary"),
    )(qkv, qkv, qkv, bias, head_g)


_DIL_WIN = DIL_TQ + 2 * DIL_HALF
_BRANCH_FOLD = tuple(max(dil, FOLD) for (_, dil) in DIL_PAIRS)


def _dil_branch(q_ref, k_ref, v_ref, mask_ref, os_ref, ls_ref, slot, part, *, dil, sub, length):
    fold = dil * sub
    qn, kn = DIL_TQ // sub, _DIL_WIN // sub
    n_i = length // qn
    n_p = n_i // DIL_PARTS
    row0 = part * (length // DIL_PARTS)

    def gather(ref, p, start, size):
        parts = [ref[0, 0, p + dil * c, pl.ds(start, size), :] for c in range(sub)]
        return parts[0] if sub == 1 else jnp.concatenate(parts, axis=0)

    def group(g, carry):
        tiles = []
        for n in range(DIL_GROUP):
            t = g * DIL_GROUP + n
            p = t // n_p
            i = part * n_p + t % n_p
            q0 = pl.multiple_of(i * qn, qn)
            k0 = pl.multiple_of(jnp.clip(q0 - DIL_HALF // sub, 0, length - kn), DIL_HALF // sub)
            edge = jnp.where(i == 0, 0, jnp.where(i == n_i - 1, 2, 1))
            tiles.append((p, q0, k0, edge))

        def score(n):
            p, q0, k0, edge = tiles[n]
            return lax.dot_general(gather(q_ref, p, q0, qn), gather(k_ref, p, k0, kn), (((1,), (1,)), ((), ())),
                                   preferred_element_type=F32) + mask_ref[edge]

        def softmax(s):
            m = jnp.max(s, axis=-1, keepdims=True)
            e = jnp.exp2(s - m)
            return m, e.astype(BF16), jnp.sum(e, axis=-1, keepdims=True)

        def finish(n, prob):
            p, q0, k0, edge = tiles[n]
            m, e, den = prob
            o = jnp.dot(e, gather(v_ref, p, k0, kn), preferred_element_type=F32) / den
            lse = jnp.broadcast_to(m + jnp.log2(den), (DIL_TQ, HEAD_DIM))
            for c in range(sub):
                rows = pl.ds(fold * (q0 - row0) + p + dil * c, qn, stride=fold)
                os_ref[slot, rows, :] = o[c * qn:(c + 1) * qn]
                ls_ref[slot, rows, :] = lse[c * qn:(c + 1) * qn]

        _skewed(DIL_GROUP, DIL_LAG, score, softmax, finish)
        return carry

    lax.fori_loop(0, dil * n_p // DIL_GROUP, group, 0)


def _dilated_kernel(qa_ref, ka_ref, va_ref, qb_ref, kb_ref, vb_ref, m0_ref, m1_ref, m2_ref, g_ref, o_ref,
                    os_ref, ls_ref, *, seq):
    part = pl.program_id(2)
    stored = {FOLD: (qa_ref, ka_ref, va_ref), FOLD ** 2: (qb_ref, kb_ref, vb_ref)}
    for slot, ((_, dil), fold, mask_ref) in enumerate(zip(DIL_PAIRS, _BRANCH_FOLD, (m0_ref, m1_ref, m2_ref))):
        _dil_branch(*stored[fold], mask_ref, os_ref, ls_ref, slot, part, dil=dil, sub=fold // dil,
                    length=seq // fold)

    def merge(c, carry):
        rows = pl.ds(pl.multiple_of(c * DIL_MERGE_ROWS, DIL_MERGE_ROWS), DIL_MERGE_ROWS)
        lses = [ls_ref[b, rows, :] for b in range(len(DIL_PAIRS))]
        mx = functools.reduce(jnp.maximum, lses)
        ws = [jnp.exp2(l - mx) for l in lses]
        num = functools.reduce(jnp.add, [w * os_ref[b, rows, :] for b, w in enumerate(ws)])
        o = num / functools.reduce(jnp.add, ws)
        o_ref[0, 0, rows, :] = _rms(o, g_ref[0]).astype(BF16)
        return carry

    lax.fori_loop(0, seq // DIL_PARTS // DIL_MERGE_ROWS, merge, 0)


def _dil_mask_table(sub, length):
    qn, kn = DIL_TQ // sub, _DIL_WIN // sub
    n_i = length // qn
    c = np.arange(sub)[:, None]
    out = []
    for i in (0, 1, n_i - 1):
        k0 = int(np.clip(i * qn - DIL_HALF // sub, 0, length - kn))
        qpos = ((i * qn + np.arange(qn))[None, :] * sub + c).reshape(-1)
        kpos = ((k0 + np.arange(kn))[None, :] * sub + c).reshape(-1)
        out.append(np.where(np.abs(kpos[None, :] - qpos[:, None]) <= DIL_HALF, 0.0, NEG))
    return jnp.asarray(np.stack(out), F32)


def _dilated(fa, fb, head_g, batch, seq):
    masks = []
    for (window, dil), fold in zip(DIL_PAIRS, _BRANCH_FOLD):
        sub, length = fold // dil, seq // fold
        assert window // (2 * dil) == DIL_HALF and fold % dil == 0 and DIL_TQ % sub == 0
        n_p = length // (DIL_TQ // sub) // DIL_PARTS
        assert n_p * DIL_PARTS * (DIL_TQ // sub) == length and (dil * n_p) % DIL_GROUP == 0 and n_p >= 1
        masks.append(_dil_mask_table(sub, length))

    def stored(f, first):
        return pl.BlockSpec((1, 1, f, seq // f, HEAD_DIM), lambda b, h, part: (b, first + h, 0, 0, 0))

    part_rows = seq // DIL_PARTS
    return pl.pallas_call(
        functools.partial(_dilated_kernel, seq=seq),
        name="dilated",
        grid=(batch, N_HEADS_DIL, DIL_PARTS),
        in_specs=[stored(f, first) for f in (FOLD, FOLD ** 2) for first in (0, N_HEADS_DIL, 2 * N_HEADS_DIL)]
        + [pl.BlockSpec((3, DIL_TQ, _DIL_WIN), lambda b, h, part: (0, 0, 0))] * len(DIL_PAIRS)
        + [pl.BlockSpec((1, 1, HEAD_DIM), lambda b, h, part: (N_HEADS_NA + h, 0, 0))],
        out_specs=pl.BlockSpec((1, 1, part_rows, HEAD_DIM), lambda b, h, part: (b, h, part, 0)),
        out_shape=jax.ShapeDtypeStruct((batch, N_HEADS_DIL, seq, HEAD_DIM), BF16),
        scratch_shapes=[pltpu.VMEM((len(DIL_PAIRS), part_rows, HEAD_DIM), F32)] * 2,
        compiler_params=_params("parallel", "parallel", "arbitrary"),
    )(fa, fa, fa, fb, fb, fb, *masks, head_g)


def _cast_once(w_ref, w16_ref):
    @pl.when(pl.program_id(0) == 0)
    def _():
        w16_ref[...] = w_ref[...].astype(BF16)


def _resident(shape):
    return pl.BlockSpec(shape, lambda i: (0,) * len(shape), pipeline_mode=pl.Buffered(1))


def _mix_out_kernel(ona_ref, odil_ref, wo_ref, h_ref, postg_ref, o_ref, wo16_ref):
    _cast_once(wo_ref, wo16_ref)
    heads = [ona_ref[0, h] for h in range(N_HEADS_NA)] + [odil_ref[0, h] for h in range(N_HEADS_DIL)]
    m = jnp.dot(jnp.concatenate(heads, axis=-1), wo16_ref[...], preferred_element_type=F32)
    o_ref[...] = h_ref[...] + _rms(m, postg_ref[...])


def _mix_out(o_na, o_dil, w_o, h, post_g, seq):
    t = h.shape[0]
    tiles_per_seq = seq // MIX_TM
    heads = lambda n: pl.BlockSpec((1, n, MIX_TM, HEAD_DIM), lambda i: (i // tiles_per_seq, 0, i % tiles_per_seq, 0))
    row = pl.BlockSpec((MIX_TM, D_MODEL), lambda i: (i, 0))
    const = lambda shape: pl.BlockSpec(shape, lambda i: (0, 0))
    return pl.pallas_call(
        _mix_out_kernel,
        name="mix_out",
        grid=(t // MIX_TM,),
        in_specs=[heads(N_HEADS_NA), heads(N_HEADS_DIL), _resident((D_MODEL, D_MODEL)), row, const((1, D_MODEL))],
        out_specs=row,
        out_shape=jax.ShapeDtypeStruct((t, D_MODEL), F32),
        scratch_shapes=[pltpu.VMEM((D_MODEL, D_MODEL), BF16)],
        compiler_params=_params("arbitrary"),
    )(o_na, o_dil, w_o, h, post_g)


def _ple_kernel(h_ref, p_ref, preg_ref, wg_ref, wp_ref, postg_ref, o_ref, wg16_ref, wp16_ref):
    _cast_once(wg_ref, wg16_ref)
    _cast_once(wp_ref, wp16_ref)
    h = h_ref[...]
    u = _rms(h, preg_ref[...]).astype(BF16)
    gate = jax.nn.sigmoid(jnp.dot(u, wg16_ref[...], preferred_element_type=F32))
    emb = jnp.dot(p_ref[...].astype(BF16), wp16_ref[...], preferred_element_type=F32)
    o_ref[...] = h + _rms(gate * emb, postg_ref[...])


def _ple(h, p, pre_g, w_gate, w_proj, post_g):
    t = h.shape[0]
    row = lambda width: pl.BlockSpec((PLE_TM, width), lambda i: (i, 0))
    const = lambda shape: pl.BlockSpec(shape, lambda i: (0, 0))
    return pl.pallas_call(
        _ple_kernel,
        name="ple",
        grid=(t // PLE_TM,),
        in_specs=[row(D_MODEL), row(PLE_DIM), const((1, D_MODEL)), _resident((D_MODEL, D_MODEL)),
                  _resident((PLE_DIM, D_MODEL)), const((1, D_MODEL))],
        out_specs=row(D_MODEL),
        out_shape=jax.ShapeDtypeStruct((t, D_MODEL), F32),
        scratch_shapes=[pltpu.VMEM((D_MODEL, D_MODEL), BF16), pltpu.VMEM((PLE_DIM, D_MODEL), BF16)],
        compiler_params=_params("arbitrary"),
    )(h, p, pre_g, w_gate, w_proj, post_g)


def kernel(x, p, ffn1_pre_g, ffn1_w_gate, ffn1_w_up, ffn1_w_down, ffn1_post_g, mix_pre_g, w_qkv, na_rpb, out_g, w_o, mix_post_g, ffn2_pre_g, ffn2_w_gate, ffn2_w_up, ffn2_w_down, ffn2_post_g, ple_pre_g, w_ple_gate, w_ple_proj, ple_post_g):
    batch, seq, d_model = x.shape
    depth = p.shape[0]
    assert d_model == D_MODEL and seq % (GRID_W * NA_KROWS) == 0
    tokens = batch * seq
    rows = seq // GRID_W
    cos_full, sin_signed = _rope_tables(seq)
    gain = lambda g: g.reshape(1, D_MODEL)
    w16 = lambda w: w.astype(BF16)

    h = x.reshape(tokens, D_MODEL)
    for i in range(depth):
        h = _ffn(h, gain(ffn1_pre_g[i]), w16(ffn1_w_gate[i]), w16(ffn1_w_up[i]), w16(ffn1_w_down[i]),
                 gain(ffn1_post_g[i]))
        qkv_na, u_mix = _qkv_na(h, gain(mix_pre_g[i]), w16(w_qkv[i][:, :3 * W_NA]), batch, seq)
        fa, fb = _qkv_dil(u_mix, w16(w_qkv[i][:, 3 * W_NA:]), cos_full, sin_signed, batch, seq)
        head_g = out_g[i].reshape(N_HEADS, 1, HEAD_DIM)
        o_na = _na(qkv_na, _na_bias_table(na_rpb[i], rows), head_g, batch, seq)
        o_dil = _dilated(fa, fb, head_g, batch, seq)
        h = _mix_out(o_na, o_dil, w_o[i], h, gain(mix_post_g[i]), seq)
        h = _ffn(h, gain(ffn2_pre_g[i]), w16(ffn2_w_gate[i]), w16(ffn2_w_up[i]), w16(ffn2_w_down[i]),
                 gain(ffn2_post_g[i]))
        h = _ple(h, p[i].reshape(tokens, PLE_DIM), gain(ple_pre_g[i]), w_ple_gate[i], w_ple_proj[i],
                 gain(ple_post_g[i]))
    return h.reshape(batch, seq, D_MODEL)
```

```python
import functools
import math

import jax
import jax.numpy as jnp
import numpy as np
from jax import lax
from jax.experimental import pallas as pl
from jax.experimental.pallas import tpu as pltpu

D_MODEL = 2048
D_FF = 5632
HEAD_DIM = 128
N_HEADS = 16
N_HEADS_NA = 4
N_HEADS_DIL = 12
W_NA = N_HEADS_NA * HEAD_DIM
W_DIL = N_HEADS_DIL * HEAD_DIM
GRID_W = 64
NA_ROWS = 8
NA_COLS = 16
DIL_PAIRS = ((128, 1), (512, 4), (2048, 16))
PLE_DIM = 256
ROPE_THETA = 10000.0
EPS = 1e-6
NEG = -1e30
SCALE = HEAD_DIM ** -0.5

F32 = jnp.float32
BF16 = jnp.bfloat16

VMEM_LIMIT_BYTES = 56 * 1024 * 1024

FFN_TM = 512
FFN_TF = 512
SIDE_ROWS = 16
QKV_TM = 1024
QKV_TN = 768
FOLD = 4
NA_QROWS = 8
NA_KROWS = 16
NA_SUB_ROWS = 128
DIL_TQ = 128
DIL_GROUP = 32
DIL_LAG = 2
DIL_HALF = 64
DIL_PARTS = 2
DIL_MERGE_ROWS = 1024
MIX_TM = 512
PLE_TM = 512


def _rms(x, g):
    return x * lax.rsqrt(jnp.mean(x * x, axis=-1, keepdims=True) + EPS) * g


def _params(*sem):
    return pltpu.CompilerParams(dimension_semantics=sem, vmem_limit_bytes=VMEM_LIMIT_BYTES)


def _normed_input(first, x_ref, g_ref, u_ref):
    if not first:
        return u_ref[...]
    u = _rms(x_ref[...], g_ref[...]).astype(BF16)
    u_ref[...] = u
    return u


def _first_or_later(step):
    j = pl.program_id(1)
    pl.when(j == 0)(functools.partial(step, True))
    pl.when(j > 0)(functools.partial(step, False))


def _skewed(n, lag, score, softmax, finish):
    scores, probs = {}, {}
    for t in range(n + 2 * lag):
        if t < n:
            scores[t] = score(t)
        if 0 <= t - lag < n:
            probs[t - lag] = softmax(scores.pop(t - lag))
        if 0 <= t - 2 * lag < n:
            finish(t - 2 * lag, probs.pop(t - 2 * lag))


def _ffn_kernel(x_ref, pre_g_ref, wg_ref, wu_ref, wd_ref, post_g_ref, *rest, side_blocks):
    n_side = len(side_blocks)
    side_in, o_ref, side_out, u_ref = rest[:n_side], rest[n_side], rest[n_side + 1:-1], rest[-1]
    j = pl.program_id(1)
    last = pl.num_programs(1) - 1
    flat_step = pl.program_id(0) * pl.num_programs(1) + j
    for w_ref, w16_ref, n_blocks in zip(side_in, side_out, side_blocks):
        @pl.when(flat_step < n_blocks)
        def _(w_ref=w_ref, w16_ref=w16_ref):
            w16_ref[...] = w_ref[...].astype(BF16)

    def step(first, final):
        u = _normed_input(first, x_ref, pre_g_ref, u_ref)
        g = jnp.dot(u, wg_ref[...], preferred_element_type=F32)
        v = jnp.dot(u, wu_ref[...], preferred_element_type=F32)
        mid = (g * jax.nn.sigmoid(g) * v).astype(BF16)
        acc = jnp.dot(mid, wd_ref[...], preferred_element_type=F32)
        if not first:
            acc = o_ref[...] + acc
        o_ref[...] = x_ref[...] + 0.5 * _rms(acc, post_g_ref[...]) if final else acc

    pl.when(j == 0)(functools.partial(step, True, False))
    pl.when(jnp.logical_and(j > 0, j < last))(functools.partial(step, False, False))
    pl.when(j == last)(functools.partial(step, False, True))


def _ffn(x, pre_g, w_gate, w_up, w_down, post_g, side=()):
    t = x.shape[0]
    n_j = D_FF // FFN_TF
    assert n_j >= 2
    side_blocks = tuple(w.shape[0] // SIDE_ROWS for w in side)
    assert all(w.shape[0] % SIDE_ROWS == 0 for w in side) and all(n <= t // FFN_TM * n_j for n in side_blocks)

    def side_spec(w, n_blocks):
        return pl.BlockSpec((SIDE_ROWS, w.shape[1]), lambda i, j: (jnp.minimum(i * n_j + j, n_blocks - 1), 0))

    side_specs = [side_spec(w, n) for w, n in zip(side, side_blocks)]
    out = pl.pallas_call(
        functools.partial(_ffn_kernel, side_blocks=side_blocks),
        name="ffn",
        grid=(t // FFN_TM, n_j),
        in_specs=[
            pl.BlockSpec((FFN_TM, D_MODEL), lambda i, j: (i, 0)),
            pl.BlockSpec((1, D_MODEL), lambda i, j: (0, 0)),
            pl.BlockSpec((D_MODEL, FFN_TF), lambda i, j: (0, j)),
            pl.BlockSpec((D_MODEL, FFN_TF), lambda i, j: (0, j)),
            pl.BlockSpec((FFN_TF, D_MODEL), lambda i, j: (j, 0)),
            pl.BlockSpec((1, D_MODEL), lambda i, j: (0, 0)),
        ] + side_specs,
        out_specs=[pl.BlockSpec((FFN_TM, D_MODEL), lambda i, j: (i, 0))] + side_specs,
        out_shape=[jax.ShapeDtypeStruct((t, D_MODEL), F32)] + [jax.ShapeDtypeStruct(w.shape, BF16) for w in side],
        scratch_shapes=[pltpu.VMEM((FFN_TM, D_MODEL), BF16)],
        compiler_params=_params("arbitrary", "arbitrary"),
    )(x, pre_g, w_gate, w_up, w_down, post_g, *side)
    return out[0], out[1:]


Q_SCALE_NA = SCALE
Q_SCALE_DIL = SCALE * math.log2(math.e)
_DIL_COL_STEPS = 3 * W_DIL // QKV_TN
_DIL_STEPS_PER_PART = W_DIL // QKV_TN
_DOT_N = 2 * HEAD_DIM


def _qkv_na_kernel(x_ref, g_ref, w_ref, colscale_ref, o_ref, u_ref):
    def step(first):
        u = _normed_input(first, x_ref, g_ref, u_ref)
        y = jnp.dot(u, w_ref[...], preferred_element_type=F32) * colscale_ref[...]
        for h in range(QKV_TN // HEAD_DIM):
            o_ref[0, h] = y[:, h * HEAD_DIM:(h + 1) * HEAD_DIM].astype(BF16)

    _first_or_later(step)


def _qkv_na(x, g, w, batch, seq):
    t = x.shape[0]
    tiles_per_seq = seq // QKV_TM
    colscale = jnp.concatenate([jnp.full((1, W_NA), Q_SCALE_NA, F32), jnp.ones((1, 2 * W_NA), F32)], axis=-1)
    return pl.pallas_call(
        _qkv_na_kernel,
        name="qkv_na",
        grid=(t // QKV_TM, 3 * W_NA // QKV_TN),
        in_specs=[
            pl.BlockSpec((QKV_TM, D_MODEL), lambda i, j: (i, 0)),
            pl.BlockSpec((1, D_MODEL), lambda i, j: (0, 0)),
            pl.BlockSpec((D_MODEL, QKV_TN), lambda i, j: (0, j)),
            pl.BlockSpec((1, QKV_TN), lambda i, j: (0, j)),
        ],
        out_specs=[pl.BlockSpec((1, QKV_TN // HEAD_DIM, QKV_TM, HEAD_DIM),
                                lambda i, j: (i // tiles_per_seq, j, i % tiles_per_seq, 0)),
                   pl.BlockSpec((QKV_TM, D_MODEL), lambda i, j: (i, 0))],
        out_shape=[jax.ShapeDtypeStruct((batch, 3 * N_HEADS_NA, seq, HEAD_DIM), BF16),
                   jax.ShapeDtypeStruct((t, D_MODEL), BF16)],
        compiler_params=_params("parallel", "arbitrary"),
    )(x, g, w, colscale)


def _qkv_dil_kernel(u_ref, w_ref, cos_ref, sin_ref, fa_ref, fb_ref, y_ref, y4_ref):
    j = pl.program_id(1)
    rotary = j < 2 * _DIL_STEPS_PER_PART
    scale = jnp.where(j < _DIL_STEPS_PER_PART, Q_SCALE_DIL, 1.0).astype(F32)

    u = u_ref[...]
    cos = cos_ref[...]
    sin = sin_ref[...]
    for pair in range(QKV_TN // _DOT_N):
        y2 = jnp.dot(u, w_ref[:, pair * _DOT_N:(pair + 1) * _DOT_N], preferred_element_type=F32)
        for half in range(_DOT_N // HEAD_DIM):
            h = pair * (_DOT_N // HEAD_DIM) + half
            y = y2[:, half * HEAD_DIM:(half + 1) * HEAD_DIM]
            y_ref[h] = jnp.where(rotary, y * cos + pltpu.roll(y, HEAD_DIM // 2, axis=1) * sin, y) * scale
            for r in range(FOLD):
                y4 = y_ref[h, pl.ds(r, QKV_TM // FOLD, stride=FOLD), :]
                fa_ref[0, h, r] = y4.astype(BF16)
                y4_ref[h, r] = y4
                for c in range(FOLD):
                    fb_ref[0, h, r + FOLD * c] = (
                        y4_ref[h, r, pl.ds(c, QKV_TM // FOLD ** 2, stride=FOLD), :].astype(BF16))


def _qkv_dil(u, w, cos_full, sin_signed, batch, seq):
    t = u.shape[0]
    tiles_per_seq = seq // QKV_TM
    first_col_step = 3 * W_NA // QKV_TN

    def fold_spec(f):
        return pl.BlockSpec((1, QKV_TN // HEAD_DIM, f, QKV_TM // f, HEAD_DIM),
                            lambda i, j: (i // tiles_per_seq, j, 0, i % tiles_per_seq, 0))

    return pl.pallas_call(
        _qkv_dil_kernel,
        name="qkv_dil",
        grid=(t // QKV_TM, _DIL_COL_STEPS),
        in_specs=[
            pl.BlockSpec((QKV_TM, D_MODEL), lambda i, j: (i, 0)),
            pl.BlockSpec((D_MODEL, QKV_TN), lambda i, j: (0, first_col_step + j)),
            pl.BlockSpec((QKV_TM, HEAD_DIM), lambda i, j: (i % tiles_per_seq, 0)),
            pl.BlockSpec((QKV_TM, HEAD_DIM), lambda i, j: (i % tiles_per_seq, 0)),
        ],
        out_specs=[fold_spec(f) for f in (FOLD, FOLD ** 2)],
        out_shape=[jax.ShapeDtypeStruct((batch, 3 * N_HEADS_DIL, f, seq // f, HEAD_DIM), BF16)
                   for f in (FOLD, FOLD ** 2)],
        scratch_shapes=[pltpu.VMEM((QKV_TN // HEAD_DIM, QKV_TM, HEAD_DIM), F32),
                        pltpu.VMEM((QKV_TN // HEAD_DIM, FOLD, QKV_TM // FOLD, HEAD_DIM), F32)],
        compiler_params=_params("parallel", "arbitrary"),
    )(u, w, cos_full, sin_signed)


def _rope_tables(seq):
    inv = jnp.float32(ROPE_THETA) ** (-jnp.arange(0, HEAD_DIM, 2, dtype=F32) / HEAD_DIM)
    ang = jnp.arange(seq, dtype=F32)[:, None] * inv[None, :]
    cos, sin = jnp.cos(ang), jnp.sin(ang)
    return jnp.concatenate([cos, cos], axis=-1), jnp.concatenate([-sin, sin], axis=-1)


def _na_tile_key_row_start(i, rows):
    return jnp.clip(i * NA_QROWS - NA_ROWS // 2, 0, rows - NA_KROWS)


def _na_kernel(q_ref, k_ref, v_ref, bias_ref, g_ref, o_ref, *, rows):
    i = pl.program_id(2)
    start = pl.multiple_of(_na_tile_key_row_start(i, rows) * GRID_W, GRID_W)
    nk = NA_KROWS * GRID_W
    k = k_ref[0, 0, pl.ds(start, nk), :]
    v = v_ref[0, 0, pl.ds(start, nk), :]
    sub = lambda n: slice(n * NA_SUB_ROWS, (n + 1) * NA_SUB_ROWS)

    def score(n):
        return lax.dot_general(q_ref[0, 0, sub(n), :], k, (((1,), (1,)), ((), ())),
                               preferred_element_type=F32) + bias_ref[0, 0, sub(n), :]

    def softmax(s):
        e = jnp.exp(s - jnp.max(s, axis=-1, keepdims=True))
        return e.astype(BF16), jnp.sum(e, axis=-1, keepdims=True)

    def finish(n, prob):
        e, den = prob
        o = jnp.dot(e, v, preferred_element_type=F32) / den
        o_ref[0, 0, sub(n), :] = _rms(o, g_ref[0]).astype(BF16)

    _skewed(NA_QROWS * GRID_W // NA_SUB_ROWS, 1, score, softmax, finish)


def _na_bias_table(rpb, rows):
    n_tiles = rows // NA_QROWS
    dr_idx = np.zeros((3, NA_QROWS, NA_KROWS), np.int32)
    row_ok = np.zeros((3, NA_QROWS, NA_KROWS), bool)
    for cls, tile in enumerate((0, 1, n_tiles - 1)):
        ks = int(np.clip(tile * NA_QROWS - NA_ROWS // 2, 0, rows - NA_KROWS))
        r = tile * NA_QROWS + np.arange(NA_QROWS)
        rs = np.clip(r - NA_ROWS // 2, 0, rows - NA_ROWS)
        kr = ks + np.arange(NA_KROWS)
        row_ok[cls] = (kr[None, :] >= rs[:, None]) & (kr[None, :] < rs[:, None] + NA_ROWS)
        dr_idx[cls] = np.clip(kr[None, :] - r[:, None] + NA_ROWS - 1, 0, 2 * NA_ROWS - 2)
    c = np.arange(GRID_W)
    qs = np.clip(c - NA_COLS // 2, 0, GRID_W - NA_COLS)
    col_ok = (c[None, :] >= qs[:, None]) & (c[None, :] < qs[:, None] + NA_COLS)
    n_dr, n_dc = 2 * NA_ROWS - 1, 2 * NA_COLS - 1
    period = 2 * GRID_W
    v = jnp.concatenate([rpb[..., NA_COLS - 1:], jnp.zeros((N_HEADS_NA, n_dr, period - n_dc), F32),
                         rpb[..., :NA_COLS - 1]], axis=-1).astype(F32)
    toe = jnp.tile(v, (1, 1, GRID_W))[..., :GRID_W * (period - 1)]
    toe = toe.reshape(N_HEADS_NA, n_dr, GRID_W, period - 1)[..., :GRID_W]
    slabs = jnp.where(col_ok[None, None], toe, NEG)
    slabs = jnp.concatenate([slabs, jnp.full((N_HEADS_NA, 1, GRID_W, GRID_W), NEG, F32)], axis=1)
    slab_idx = np.where(row_ok, dr_idx, n_dr)
    return pl.pallas_call(
        functools.partial(_na_bias_kernel, slab_idx=slab_idx),
        name="na_bias",
        grid=(N_HEADS_NA,),
        in_specs=[pl.BlockSpec((1, n_dr + 1, GRID_W, GRID_W), lambda h: (h, 0, 0, 0))],
        out_specs=pl.BlockSpec((3, 1, NA_QROWS * GRID_W, NA_KROWS * GRID_W), lambda h: (0, h, 0, 0)),
        out_shape=jax.ShapeDtypeStruct((3, N_HEADS_NA, NA_QROWS * GRID_W, NA_KROWS * GRID_W), F32),
        compiler_params=_params("parallel"),
    )(slabs)


def _na_bias_kernel(slabs_ref, o_ref, *, slab_idx):
    n_cls, n_q, n_k = slab_idx.shape
    for cls in range(n_cls):
        for rq in range(n_q):
            row = jnp.concatenate([slabs_ref[0, int(slab_idx[cls, rq, rk])] for rk in range(n_k)], axis=-1)
            o_ref[cls, 0, rq * GRID_W:(rq + 1) * GRID_W, :] = row


def _na(qkv, bias, head_g, batch, seq):
    rows = seq // GRID_W
    n_tiles = rows // NA_QROWS
    tq = NA_QROWS * GRID_W

    def bias_map(b, h, i):
        return (jnp.where(i == 0, 0, jnp.where(i == n_tiles - 1, 2, 1)), h, 0, 0)

    return pl.pallas_call(
        functools.partial(_na_kernel, rows=rows),
        name="na",
        grid=(batch, N_HEADS_NA, n_tiles),
        in_specs=[
            pl.BlockSpec((1, 1, tq, HEAD_DIM), lambda b, h, i: (b, h, i, 0)),
            pl.BlockSpec((1, 1, seq, HEAD_DIM), lambda b, h, i: (b, N_HEADS_NA + h, 0, 0)),
            pl.BlockSpec((1, 1, seq, HEAD_DIM), lambda b, h, i: (b, 2 * N_HEADS_NA + h, 0, 0)),
            pl.BlockSpec((1, 1, tq, NA_KROWS * GRID_W), bias_map),
            pl.BlockSpec((1, 1, HEAD_DIM), lambda b, h, i: (h, 0, 0)),
        ],
        out_specs=pl.BlockSpec((1, 1, tq, HEAD_DIM), lambda b, h, i: (b, h, i, 0)),
        out_shape=jax.ShapeDtypeStruct((batch, N_HEADS_NA, seq, HEAD_DIM), BF16),
        compiler_params=_params("parallel", "parallel", "arbitrary"),
    )(qkv, qkv, qkv, bias, head_g)


_DIL_WIN = DIL_TQ + 2 * DIL_HALF
_BRANCH_FOLD = tuple(max(dil, FOLD) for (_, dil) in DIL_PAIRS)


def _dil_branch(q_ref, k_ref, v_ref, mask_ref, os_ref, ls_ref, slot, part, *, dil, sub, length):
    fold = dil * sub
    qn, kn = DIL_TQ // sub, _DIL_WIN // sub
    n_i = length // qn
    n_p = n_i // DIL_PARTS
    row0 = part * (length // DIL_PARTS)

    def gather(ref, p, start, size):
        parts = [ref[0, 0, p + dil * c, pl.ds(start, size), :] for c in range(sub)]
        return parts[0] if sub == 1 else jnp.concatenate(parts, axis=0)

    def group(g, carry):
        tiles = []
        for n in range(DIL_GROUP):
            t = g * DIL_GROUP + n
            p = t // n_p
            i = part * n_p + t % n_p
            q0 = pl.multiple_of(i * qn, qn)
            k0 = pl.multiple_of(jnp.clip(q0 - DIL_HALF // sub, 0, length - kn), DIL_HALF // sub)
            edge = jnp.where(i == 0, 0, jnp.where(i == n_i - 1, 2, 1))
            tiles.append((p, q0, k0, edge))

        def score(n):
            p, q0, k0, edge = tiles[n]
            return lax.dot_general(gather(q_ref, p, q0, qn), gather(k_ref, p, k0, kn), (((1,), (1,)), ((), ())),
                                   preferred_element_type=F32) + mask_ref[edge]

        def softmax(s):
            m = jnp.max(s, axis=-1, keepdims=True)
            e = jnp.exp2(s - m)
            return m, e.astype(BF16), jnp.sum(e, axis=-1, keepdims=True)

        def finish(n, prob):
            p, q0, k0, edge = tiles[n]
            m, e, den = prob
            o = jnp.dot(e, gather(v_ref, p, k0, kn), preferred_element_type=F32) / den
            lse = jnp.broadcast_to(m + jnp.log2(den), (DIL_TQ, HEAD_DIM))
            for c in range(sub):
                rows = pl.ds(fold * (q0 - row0) + p + dil * c, qn, stride=fold)
                os_ref[slot, rows, :] = o[c * qn:(c + 1) * qn]
                ls_ref[slot, rows, :] = lse[c * qn:(c + 1) * qn]

        _skewed(DIL_GROUP, DIL_LAG, score, softmax, finish)
        return carry

    lax.fori_loop(0, dil * n_p // DIL_GROUP, group, 0)


def _dilated_kernel(qa_ref, ka_ref, va_ref, qb_ref, kb_ref, vb_ref, m0_ref, m1_ref, m2_ref, g_ref, o_ref,
                    os_ref, ls_ref, *, seq):
    part = pl.program_id(2)
    stored = {FOLD: (qa_ref, ka_ref, va_ref), FOLD ** 2: (qb_ref, kb_ref, vb_ref)}
    for slot, ((_, dil), fold, mask_ref) in enumerate(zip(DIL_PAIRS, _BRANCH_FOLD, (m0_ref, m1_ref, m2_ref))):
        _dil_branch(*stored[fold], mask_ref, os_ref, ls_ref, slot, part, dil=dil, sub=fold // dil,
                    length=seq // fold)

    def merge(c, carry):
        rows = pl.ds(pl.multiple_of(c * DIL_MERGE_ROWS, DIL_MERGE_ROWS), DIL_MERGE_ROWS)
        lses = [ls_ref[b, rows, :] for b in range(len(DIL_PAIRS))]
        mx = functools.reduce(jnp.maximum, lses)
        ws = [jnp.exp2(l - mx) for l in lses]
        num = functools.reduce(jnp.add, [w * os_ref[b, rows, :] for b, w in enumerate(ws)])
        o = num / functools.reduce(jnp.add, ws)
        o_ref[0, 0, rows, :] = _rms(o, g_ref[0]).astype(BF16)
        return carry

    lax.fori_loop(0, seq // DIL_PARTS // DIL_MERGE_ROWS, merge, 0)


def _dil_mask_table(sub, length):
    qn, kn = DIL_TQ // sub, _DIL_WIN // sub
    n_i = length // qn
    c = np.arange(sub)[:, None]
    out = []
    for i in (0, 1, n_i - 1):
        k0 = int(np.clip(i * qn - DIL_HALF // sub, 0, length - kn))
        qpos = ((i * qn + np.arange(qn))[None, :] * sub + c).reshape(-1)
        kpos = ((k0 + np.arange(kn))[None, :] * sub + c).reshape(-1)
        out.append(np.where(np.abs(kpos[None, :] - qpos[:, None]) <= DIL_HALF, 0.0, NEG))
    return jnp.asarray(np.stack(out), F32)


def _dilated(fa, fb, head_g, batch, seq):
    masks = []
    for (window, dil), fold in zip(DIL_PAIRS, _BRANCH_FOLD):
        sub, length = fold // dil, seq // fold
        assert window // (2 * dil) == DIL_HALF and fold % dil == 0 and DIL_TQ % sub == 0
        n_p = length // (DIL_TQ // sub) // DIL_PARTS
        assert n_p * DIL_PARTS * (DIL_TQ // sub) == length and (dil * n_p) % DIL_GROUP == 0 and n_p >= 1
        masks.append(_dil_mask_table(sub, length))

    def stored(f, first):
        return pl.BlockSpec((1, 1, f, seq // f, HEAD_DIM), lambda b, h, part: (b, first + h, 0, 0, 0))

    part_rows = seq // DIL_PARTS
    return pl.pallas_call(
        functools.partial(_dilated_kernel, seq=seq),
        name="dilated",
        grid=(batch, N_HEADS_DIL, DIL_PARTS),
        in_specs=[stored(f, first) for f in (FOLD, FOLD ** 2) for first in (0, N_HEADS_DIL, 2 * N_HEADS_DIL)]
        + [pl.BlockSpec((3, DIL_TQ, _DIL_WIN), lambda b, h, part: (0, 0, 0))] * len(DIL_PAIRS)
        + [pl.BlockSpec((1, 1, HEAD_DIM), lambda b, h, part: (N_HEADS_NA + h, 0, 0))],
        out_specs=pl.BlockSpec((1, 1, part_rows, HEAD_DIM), lambda b, h, part: (b, h, part, 0)),
        out_shape=jax.ShapeDtypeStruct((batch, N_HEADS_DIL, seq, HEAD_DIM), BF16),
        scratch_shapes=[pltpu.VMEM((len(DIL_PAIRS), part_rows, HEAD_DIM), F32)] * 2,
        compiler_params=_params("parallel", "parallel", "arbitrary"),
    )(fa, fa, fa, fb, fb, fb, *masks, head_g)


def _cast_once(w_ref, w16_ref):
    @pl.when(pl.program_id(0) == 0)
    def _():
        w16_ref[...] = w_ref[...].astype(BF16)


def _resident(shape):
    return pl.BlockSpec(shape, lambda i: (0,) * len(shape), pipeline_mode=pl.Buffered(1))


def _mix_out_kernel(ona_ref, odil_ref, wo_ref, h_ref, postg_ref, o_ref, wo16_ref):
    _cast_once(wo_ref, wo16_ref)
    heads = [ona_ref[0, h] for h in range(N_HEADS_NA)] + [odil_ref[0, h] for h in range(N_HEADS_DIL)]
    m = jnp.dot(jnp.concatenate(heads, axis=-1), wo16_ref[...], preferred_element_type=F32)
    o_ref[...] = h_ref[...] + _rms(m, postg_ref[...])


def _mix_out(o_na, o_dil, w_o, h, post_g, seq):
    t = h.shape[0]
    tiles_per_seq = seq // MIX_TM
    heads = lambda n: pl.BlockSpec((1, n, MIX_TM, HEAD_DIM), lambda i: (i // tiles_per_seq, 0, i % tiles_per_seq, 0))
    row = pl.BlockSpec((MIX_TM, D_MODEL), lambda i: (i, 0))
    const = lambda shape: pl.BlockSpec(shape, lambda i: (0, 0))
    return pl.pallas_call(
        _mix_out_kernel,
        name="mix_out",
        grid=(t // MIX_TM,),
        in_specs=[heads(N_HEADS_NA), heads(N_HEADS_DIL), _resident((D_MODEL, D_MODEL)), row, const((1, D_MODEL))],
        out_specs=row,
        out_shape=jax.ShapeDtypeStruct((t, D_MODEL), F32),
        scratch_shapes=[pltpu.VMEM((D_MODEL, D_MODEL), BF16)],
        compiler_params=_params("arbitrary"),
    )(o_na, o_dil, w_o, h, post_g)


def _ple_kernel(h_ref, p_ref, preg_ref, wg_ref, wp_ref, postg_ref, o_ref, wg16_ref, wp16_ref):
    _cast_once(wg_ref, wg16_ref)
    _cast_once(wp_ref, wp16_ref)
    h = h_ref[...]
    u = _rms(h, preg_ref[...]).astype(BF16)
    gate = jax.nn.sigmoid(jnp.dot(u, wg16_ref[...], preferred_element_type=F32))
    emb = jnp.dot(p_ref[...].astype(BF16), wp16_ref[...], preferred_element_type=F32)
    o_ref[...] = h + _rms(gate * emb, postg_ref[...])


def _ple(h, p, pre_g, w_gate, w_proj, post_g):
    t = h.shape[0]
    row = lambda width: pl.BlockSpec((PLE_TM, width), lambda i: (i, 0))
    const = lambda shape: pl.BlockSpec(shape, lambda i: (0, 0))
    return pl.pallas_call(
        _ple_kernel,
        name="ple",
        grid=(t // PLE_TM,),
        in_specs=[row(D_MODEL), row(PLE_DIM), const((1, D_MODEL)), _resident((D_MODEL, D_MODEL)),
                  _resident((PLE_DIM, D_MODEL)), const((1, D_MODEL))],
        out_specs=row(D_MODEL),
        out_shape=jax.ShapeDtypeStruct((t, D_MODEL), F32),
        scratch_shapes=[pltpu.VMEM((D_MODEL, D_MODEL), BF16), pltpu.VMEM((PLE_DIM, D_MODEL), BF16)],
        compiler_params=_params("arbitrary"),
    )(h, p, pre_g, w_gate, w_proj, post_g)


def kernel(x, p, ffn1_pre_g, ffn1_w_gate, ffn1_w_up, ffn1_w_down, ffn1_post_g, mix_pre_g, w_qkv, na_rpb, out_g, w_o, mix_post_g, ffn2_pre_g, ffn2_w_gate, ffn2_w_up, ffn2_w_down, ffn2_post_g, ple_pre_g, w_ple_gate, w_ple_proj, ple_post_g):
    batch, seq, d_model = x.shape
    depth = p.shape[0]
    assert d_model == D_MODEL and seq % (GRID_W * NA_KROWS) == 0
    tokens = batch * seq
    rows = seq // GRID_W
    cos_full, sin_signed = _rope_tables(seq)
    gain = lambda g: g.reshape(1, D_MODEL)
    w16 = lambda w: w.astype(BF16)

    h = x.reshape(tokens, D_MODEL)
    for i in range(depth):
        h, (w_qkv16, w_gate16, w_up16, w_down16) = _ffn(
            h, gain(ffn1_pre_g[i]), w16(ffn1_w_gate[i]), w16(ffn1_w_up[i]), w16(ffn1_w_down[i]),
            gain(ffn1_post_g[i]), side=(w_qkv[i], ffn2_w_gate[i], ffn2_w_up[i], ffn2_w_down[i]))
        qkv_na, u_mix = _qkv_na(h, gain(mix_pre_g[i]), w_qkv16, batch, seq)
        fa, fb = _qkv_dil(u_mix, w_qkv16, cos_full, sin_signed, batch, seq)
        head_g = out_g[i].reshape(N_HEADS, 1, HEAD_DIM)
        o_na = _na(qkv_na, _na_bias_table(na_rpb[i], rows), head_g, batch, seq)
        o_dil = _dilated(fa, fb, head_g, batch, seq)
        h = _mix_out(o_na, o_dil, w_o[i], h, gain(mix_post_g[i]), seq)
        h, _ = _ffn(h, gain(ffn2_pre_g[i]), w_gate16, w_up16, w_down16, gain(ffn2_post_g[i]))
        h = _ple(h, p[i].reshape(tokens, PLE_DIM), gain(ple_pre_g[i]), w_ple_gate[i], w_ple_proj[i],
                 gain(ple_post_g[i]))
    return h.reshape(batch, seq, D_MODEL)
```

```python
import functools
import math

import jax
import jax.numpy as jnp
import numpy as np
from jax import lax
from jax.experimental import pallas as pl
from jax.experimental.pallas import tpu as pltpu

D_MODEL = 2048
D_FF = 5632
HEAD_DIM = 128
N_HEADS = 16
N_HEADS_NA = 4
N_HEADS_DIL = 12
W_NA = N_HEADS_NA * HEAD_DIM
W_DIL = N_HEADS_DIL * HEAD_DIM
GRID_W = 64
NA_ROWS = 8
NA_COLS = 16
DIL_PAIRS = ((128, 1), (512, 4), (2048, 16))
PLE_DIM = 256
ROPE_THETA = 10000.0
EPS = 1e-6
NEG = -1e30
SCALE = HEAD_DIM ** -0.5

F32 = jnp.float32
BF16 = jnp.bfloat16

VMEM_LIMIT_BYTES = 56 * 1024 * 1024
FFN_VMEM_LIMIT_BYTES = 62 * 1024 * 1024

FFN_TM = 1024
FFN_TF = 256
SIDE_ROWS = 16
QKV_TM = 1024
QKV_TN = 768
FOLD = 4
NA_QROWS = 8
NA_KROWS = 16
NA_SUB_ROWS = 128
NA_LAG = 1
DIL_TQ = 128
DIL_GROUP = 32
DIL_LAG = 2
DIL_HALF = 64
DIL_PARTS = 2
DIL_MERGE_ROWS = 1024
MIX_TM = 512
PLE_TM = 512


def _rms(x, g):
    return x * lax.rsqrt(jnp.mean(x * x, axis=-1, keepdims=True) + EPS) * g


def _params(*sem, vmem_limit_bytes=VMEM_LIMIT_BYTES):
    return pltpu.CompilerParams(dimension_semantics=sem, vmem_limit_bytes=vmem_limit_bytes)


def _normed_input(first, x_ref, g_ref, u_ref):
    if not first:
        return u_ref[...]
    u = _rms(x_ref[...], g_ref[...]).astype(BF16)
    u_ref[...] = u
    return u


def _first_or_later(step):
    j = pl.program_id(1)
    pl.when(j == 0)(functools.partial(step, True))
    pl.when(j > 0)(functools.partial(step, False))


def _skewed(n, lag, score, softmax, finish):
    scores, probs = {}, {}
    for t in range(n + 2 * lag):
        if t < n:
            scores[t] = score(t)
        if 0 <= t - lag < n:
            probs[t - lag] = softmax(scores.pop(t - lag))
        if 0 <= t - 2 * lag < n:
            finish(t - 2 * lag, probs.pop(t - 2 * lag))


def _ffn_kernel(x_ref, pre_g_ref, wg_ref, wu_ref, wd_ref, post_g_ref, *rest, side_blocks):
    n_side = len(side_blocks)
    side_in, o_ref, side_out, u_ref = rest[:n_side], rest[n_side], rest[n_side + 1:-1], rest[-1]
    j = pl.program_id(1)
    last = pl.num_programs(1) - 1
    flat_step = pl.program_id(0) * pl.num_programs(1) + j
    for w_ref, w16_ref, n_blocks in zip(side_in, side_out, side_blocks):
        @pl.when(flat_step < n_blocks)
        def _(w_ref=w_ref, w16_ref=w16_ref):
            w16_ref[...] = w_ref[...].astype(BF16)

    def step(first, final):
        u = _normed_input(first, x_ref, pre_g_ref, u_ref)
        g = jnp.dot(u, wg_ref[...].astype(BF16), preferred_element_type=F32)
        v = jnp.dot(u, wu_ref[...].astype(BF16), preferred_element_type=F32)
        mid = (g * jax.nn.sigmoid(g) * v).astype(BF16)
        acc = jnp.dot(mid, wd_ref[...].astype(BF16), preferred_element_type=F32)
        if not first:
            acc = o_ref[...] + acc
        o_ref[...] = x_ref[...] + 0.5 * _rms(acc, post_g_ref[...]) if final else acc

    pl.when(j == 0)(functools.partial(step, True, False))
    pl.when(jnp.logical_and(j > 0, j < last))(functools.partial(step, False, False))
    pl.when(j == last)(functools.partial(step, False, True))


def _ffn(x, pre_g, w_gate, w_up, w_down, post_g, side=()):
    t = x.shape[0]
    n_j = D_FF // FFN_TF
    assert n_j >= 2
    side_blocks = tuple(w.shape[0] // SIDE_ROWS for w in side)
    assert all(w.shape[0] % SIDE_ROWS == 0 for w in side) and all(n <= t // FFN_TM * n_j for n in side_blocks)

    def side_spec(w, n_blocks):
        return pl.BlockSpec((SIDE_ROWS, w.shape[1]), lambda i, j: (jnp.minimum(i * n_j + j, n_blocks - 1), 0))

    side_specs = [side_spec(w, n) for w, n in zip(side, side_blocks)]
    out = pl.pallas_call(
        functools.partial(_ffn_kernel, side_blocks=side_blocks),
        name="ffn",
        grid=(t // FFN_TM, n_j),
        in_specs=[
            pl.BlockSpec((FFN_TM, D_MODEL), lambda i, j: (i, 0)),
            pl.BlockSpec((1, D_MODEL), lambda i, j: (0, 0)),
            pl.BlockSpec((D_MODEL, FFN_TF), lambda i, j: (0, j)),
            pl.BlockSpec((D_MODEL, FFN_TF), lambda i, j: (0, j)),
            pl.BlockSpec((FFN_TF, D_MODEL), lambda i, j: (j, 0)),
            pl.BlockSpec((1, D_MODEL), lambda i, j: (0, 0)),
        ] + side_specs,
        out_specs=[pl.BlockSpec((FFN_TM, D_MODEL), lambda i, j: (i, 0))] + side_specs,
        out_shape=[jax.ShapeDtypeStruct((t, D_MODEL), F32)] + [jax.ShapeDtypeStruct(w.shape, BF16) for w in side],
        scratch_shapes=[pltpu.VMEM((FFN_TM, D_MODEL), BF16)],
        compiler_params=_params("arbitrary", "arbitrary", vmem_limit_bytes=FFN_VMEM_LIMIT_BYTES),
    )(x, pre_g, w_gate, w_up, w_down, post_g, *side)
    return out[0], out[1:]


Q_SCALE_NA = SCALE
Q_SCALE_DIL = SCALE * math.log2(math.e)
_DIL_COL_STEPS = 3 * W_DIL // QKV_TN
_DIL_STEPS_PER_PART = W_DIL // QKV_TN
_DOT_N = 2 * HEAD_DIM


def _qkv_na_kernel(x_ref, g_ref, w_ref, colscale_ref, o_ref, u_ref):
    def step(first):
        u = _normed_input(first, x_ref, g_ref, u_ref)
        y = jnp.dot(u, w_ref[...], preferred_element_type=F32) * colscale_ref[...]
        for h in range(QKV_TN // HEAD_DIM):
            o_ref[0, h] = y[:, h * HEAD_DIM:(h + 1) * HEAD_DIM].astype(BF16)

    _first_or_later(step)


def _qkv_na(x, g, w, batch, seq):
    t = x.shape[0]
    tiles_per_seq = seq // QKV_TM
    colscale = jnp.concatenate([jnp.full((1, W_NA), Q_SCALE_NA, F32), jnp.ones((1, 2 * W_NA), F32)], axis=-1)
    return pl.pallas_call(
        _qkv_na_kernel,
        name="qkv_na",
        grid=(t // QKV_TM, 3 * W_NA // QKV_TN),
        in_specs=[
            pl.BlockSpec((QKV_TM, D_MODEL), lambda i, j: (i, 0)),
            pl.BlockSpec((1, D_MODEL), lambda i, j: (0, 0)),
            pl.BlockSpec((D_MODEL, QKV_TN), lambda i, j: (0, j)),
            pl.BlockSpec((1, QKV_TN), lambda i, j: (0, j)),
        ],
        out_specs=[pl.BlockSpec((1, QKV_TN // HEAD_DIM, QKV_TM, HEAD_DIM),
                                lambda i, j: (i // tiles_per_seq, j, i % tiles_per_seq, 0)),
                   pl.BlockSpec((QKV_TM, D_MODEL), lambda i, j: (i, 0))],
        out_shape=[jax.ShapeDtypeStruct((batch, 3 * N_HEADS_NA, seq, HEAD_DIM), BF16),
                   jax.ShapeDtypeStruct((t, D_MODEL), BF16)],
        compiler_params=_params("parallel", "arbitrary"),
    )(x, g, w, colscale)


def _qkv_dil_kernel(u_ref, w_ref, cos_ref, sin_ref, fa_ref, fb_ref, y_ref, y4_ref):
    j = pl.program_id(1)
    rotary = j < 2 * _DIL_STEPS_PER_PART
    scale = jnp.where(j < _DIL_STEPS_PER_PART, Q_SCALE_DIL, 1.0).astype(F32)

    u = u_ref[...]
    cos = cos_ref[...]
    sin = sin_ref[...]
    for pair in range(QKV_TN // _DOT_N):
        y2 = jnp.dot(u, w_ref[:, pair * _DOT_N:(pair + 1) * _DOT_N], preferred_element_type=F32)
        for half in range(_DOT_N // HEAD_DIM):
            h = pair * (_DOT_N // HEAD_DIM) + half
            y = y2[:, half * HEAD_DIM:(half + 1) * HEAD_DIM]
            y_ref[h] = jnp.where(rotary, y * cos + pltpu.roll(y, HEAD_DIM // 2, axis=1) * sin, y) * scale
            for r in range(FOLD):
                y4 = y_ref[h, pl.ds(r, QKV_TM // FOLD, stride=FOLD), :]
                fa_ref[0, h, r] = y4.astype(BF16)
                y4_ref[h, r] = y4
                for c in range(FOLD):
                    fb_ref[0, h, r + FOLD * c] = (
                        y4_ref[h, r, pl.ds(c, QKV_TM // FOLD ** 2, stride=FOLD), :].astype(BF16))


def _qkv_dil(u, w, cos_full, sin_signed, batch, seq):
    t = u.shape[0]
    tiles_per_seq = seq // QKV_TM
    first_col_step = 3 * W_NA // QKV_TN

    def fold_spec(f):
        return pl.BlockSpec((1, QKV_TN // HEAD_DIM, f, QKV_TM // f, HEAD_DIM),
                            lambda i, j: (i // tiles_per_seq, j, 0, i % tiles_per_seq, 0))

    return pl.pallas_call(
        _qkv_dil_kernel,
        name="qkv_dil",
        grid=(t // QKV_TM, _DIL_COL_STEPS),
        in_specs=[
            pl.BlockSpec((QKV_TM, D_MODEL), lambda i, j: (i, 0)),
            pl.BlockSpec((D_MODEL, QKV_TN), lambda i, j: (0, first_col_step + j)),
            pl.BlockSpec((QKV_TM, HEAD_DIM), lambda i, j: (i % tiles_per_seq, 0)),
            pl.BlockSpec((QKV_TM, HEAD_DIM), lambda i, j: (i % tiles_per_seq, 0)),
        ],
        out_specs=[fold_spec(f) for f in (FOLD, FOLD ** 2)],
        out_shape=[jax.ShapeDtypeStruct((batch, 3 * N_HEADS_DIL, f, seq // f, HEAD_DIM), BF16)
                   for f in (FOLD, FOLD ** 2)],
        scratch_shapes=[pltpu.VMEM((QKV_TN // HEAD_DIM, QKV_TM, HEAD_DIM), F32),
                        pltpu.VMEM((QKV_TN // HEAD_DIM, FOLD, QKV_TM // FOLD, HEAD_DIM), F32)],
        compiler_params=_params("parallel", "arbitrary"),
    )(u, w, cos_full, sin_signed)


def _rope_tables(seq):
    inv = jnp.float32(ROPE_THETA) ** (-jnp.arange(0, HEAD_DIM, 2, dtype=F32) / HEAD_DIM)
    ang = jnp.arange(seq, dtype=F32)[:, None] * inv[None, :]
    cos, sin = jnp.cos(ang), jnp.sin(ang)
    return jnp.concatenate([cos, cos], axis=-1), jnp.concatenate([-sin, sin], axis=-1)


def _na_tile_key_row_start(i, rows):
    return jnp.clip(i * NA_QROWS - NA_ROWS // 2, 0, rows - NA_KROWS)


def _na_kernel(q_ref, k_ref, v_ref, bias_ref, g_ref, o_ref, *, rows):
    i = pl.program_id(2)
    start = pl.multiple_of(_na_tile_key_row_start(i, rows) * GRID_W, GRID_W)
    nk = NA_KROWS * GRID_W
    k = k_ref[0, 0, pl.ds(start, nk), :]
    v = v_ref[0, 0, pl.ds(start, nk), :]
    sub = lambda n: slice(n * NA_SUB_ROWS, (n + 1) * NA_SUB_ROWS)

    def score(n):
        return lax.dot_general(q_ref[0, 0, sub(n), :], k, (((1,), (1,)), ((), ())),
                               preferred_element_type=F32) + bias_ref[0, 0, sub(n), :]

    def softmax(s):
        e = jnp.exp(s - jnp.max(s, axis=-1, keepdims=True))
        return e.astype(BF16), jnp.sum(e, axis=-1, keepdims=True)

    def finish(n, prob):
        e, den = prob
        o = jnp.dot(e, v, preferred_element_type=F32) / den
        o_ref[0, 0, sub(n), :] = _rms(o, g_ref[0]).astype(BF16)

    _skewed(NA_QROWS * GRID_W // NA_SUB_ROWS, NA_LAG, score, softmax, finish)


def _na_bias_table(rpb, rows):
    n_tiles = rows // NA_QROWS

    def row_pairs(tile):
        ks = int(np.clip(tile * NA_QROWS - NA_ROWS // 2, 0, rows - NA_KROWS))
        r = tile * NA_QROWS + np.arange(NA_QROWS)
        rs = np.clip(r - NA_ROWS // 2, 0, rows - NA_ROWS)
        kr = ks + np.arange(NA_KROWS)
        ok = (kr[None, :] >= rs[:, None]) & (kr[None, :] < rs[:, None] + NA_ROWS)
        assert (ok.sum(axis=1) == NA_ROWS).all()
        return ok, np.clip(kr[None, :] - r[:, None] + NA_ROWS - 1, 0, 2 * NA_ROWS - 2)

    row_ok, dr_idx = (np.stack(a) for a in zip(*(row_pairs(tile) for tile in (0, 1, n_tiles - 1))))
    for tile in range(1, n_tiles - 1):
        ok, dr = row_pairs(tile)
        assert (ok == row_ok[1]).all() and (np.where(ok, dr, 0) == np.where(ok, dr_idx[1], 0)).all()
    c = np.arange(GRID_W)
    qs = np.clip(c - NA_COLS // 2, 0, GRID_W - NA_COLS)
    col_ok = (c[None, :] >= qs[:, None]) & (c[None, :] < qs[:, None] + NA_COLS)
    n_dr, n_dc = 2 * NA_ROWS - 1, 2 * NA_COLS - 1
    period = 2 * GRID_W
    v = jnp.concatenate([rpb[..., NA_COLS - 1:], jnp.zeros((N_HEADS_NA, n_dr, period - n_dc), F32),
                         rpb[..., :NA_COLS - 1]], axis=-1).astype(F32)
    toe = jnp.tile(v, (1, 1, GRID_W))[..., :GRID_W * (period - 1)]
    toe = toe.reshape(N_HEADS_NA, n_dr, GRID_W, period - 1)[..., :GRID_W]
    slabs = jnp.where(col_ok[None, None], toe, NEG)
    slabs = jnp.concatenate([slabs, jnp.full((N_HEADS_NA, 1, GRID_W, GRID_W), NEG, F32)], axis=1)
    slab_idx = np.where(row_ok, dr_idx, n_dr)
    return pl.pallas_call(
        functools.partial(_na_bias_kernel, slab_idx=slab_idx),
        name="na_bias",
        grid=(N_HEADS_NA,),
        in_specs=[pl.BlockSpec((1, n_dr + 1, GRID_W, GRID_W), lambda h: (h, 0, 0, 0))],
        out_specs=pl.BlockSpec((3, 1, NA_QROWS * GRID_W, NA_KROWS * GRID_W), lambda h: (0, h, 0, 0)),
        out_shape=jax.ShapeDtypeStruct((3, N_HEADS_NA, NA_QROWS * GRID_W, NA_KROWS * GRID_W), F32),
        compiler_params=_params("parallel"),
    )(slabs)


def _na_bias_kernel(slabs_ref, o_ref, *, slab_idx):
    n_cls, n_q, n_k = slab_idx.shape
    for cls in range(n_cls):
        for rq in range(n_q):
            row = jnp.concatenate([slabs_ref[0, int(slab_idx[cls, rq, rk])] for rk in range(n_k)], axis=-1)
            o_ref[cls, 0, rq * GRID_W:(rq + 1) * GRID_W, :] = row


def _na(qkv, bias, head_g, batch, seq):
    rows = seq // GRID_W
    n_tiles = rows // NA_QROWS
    tq = NA_QROWS * GRID_W

    def bias_map(b, h, i):
        return (jnp.where(i == 0, 0, jnp.where(i == n_tiles - 1, 2, 1)), h, 0, 0)

    return pl.pallas_call(
        functools.partial(_na_kernel, rows=rows),
        name="na",
        grid=(batch, N_HEADS_NA, n_tiles),
        in_specs=[
            pl.BlockSpec((1, 1, tq, HEAD_DIM), lambda b, h, i: (b, h, i, 0)),
            pl.BlockSpec((1, 1, seq, HEAD_DIM), lambda b, h, i: (b, N_HEADS_NA + h, 0, 0)),
            pl.BlockSpec((1, 1, seq, HEAD_DIM), lambda b, h, i: (b, 2 * N_HEADS_NA + h, 0, 0)),
            pl.BlockSpec((1, 1, tq, NA_KROWS * GRID_W), bias_map),
            pl.BlockSpec((1, 1, HEAD_DIM), lambda b, h, i: (h, 0, 0)),
        ],
        out_specs=pl.BlockSpec((1, 1, tq, HEAD_DIM), lambda b, h, i: (b, h, i, 0)),
        out_shape=jax.ShapeDtypeStruct((batch, N_HEADS_NA, seq, HEAD_DIM), BF16),
        compiler_params=_params("parallel", "parallel", "arbitrary"),
    )(qkv, qkv, qkv, bias, head_g)


_DIL_WIN = DIL_TQ + 2 * DIL_HALF
_BRANCH_FOLD = tuple(max(dil, FOLD) for (_, dil) in DIL_PAIRS)


def _dil_branch(q_ref, k_ref, v_ref, mask_ref, os_ref, ls_ref, slot, part, *, dil, sub, length):
    fold = dil * sub
    qn, kn = DIL_TQ // sub, _DIL_WIN // sub
    n_i = length // qn
    n_p = n_i // DIL_PARTS
    row0 = part * (length // DIL_PARTS)

    def gather(ref, p, start, size):
        parts = [ref[0, 0, p + dil * c, pl.ds(start, size), :] for c in range(sub)]
        return parts[0] if sub == 1 else jnp.concatenate(parts, axis=0)

    def group(g, carry):
        tiles = []
        for n in range(DIL_GROUP):
            t = g * DIL_GROUP + n
            p = t // n_p
            i = part * n_p + t % n_p
            q0 = pl.multiple_of(i * qn, qn)
            k0 = pl.multiple_of(jnp.clip(q0 - DIL_HALF // sub, 0, length - kn), DIL_HALF // sub)
            edge = jnp.where(i == 0, 0, jnp.where(i == n_i - 1, 2, 1))
            tiles.append((p, q0, k0, edge))

        def score(n):
            p, q0, k0, edge = tiles[n]
            return lax.dot_general(gather(q_ref, p, q0, qn), gather(k_ref, p, k0, kn), (((1,), (1,)), ((), ())),
                                   preferred_element_type=F32) + mask_ref[edge]

        def softmax(s):
            m = jnp.max(s, axis=-1, keepdims=True)
            e = jnp.exp2(s - m)
            return m, e.astype(BF16), jnp.sum(e, axis=-1, keepdims=True)

        def finish(n, prob):
            p, q0, k0, edge = tiles[n]
            m, e, den = prob
            o = jnp.dot(e, gather(v_ref, p, k0, kn), preferred_element_type=F32) / den
            lse = jnp.broadcast_to(m + jnp.log2(den), (DIL_TQ, HEAD_DIM))
            for c in range(sub):
                rows = pl.ds(fold * (q0 - row0) + p + dil * c, qn, stride=fold)
                os_ref[slot, rows, :] = o[c * qn:(c + 1) * qn]
                ls_ref[slot, rows, :] = lse[c * qn:(c + 1) * qn]

        _skewed(DIL_GROUP, DIL_LAG, score, softmax, finish)
        return carry

    lax.fori_loop(0, dil * n_p // DIL_GROUP, group, 0)


def _dilated_kernel(qa_ref, ka_ref, va_ref, qb_ref, kb_ref, vb_ref, m0_ref, m1_ref, m2_ref, g_ref, o_ref,
                    os_ref, ls_ref, *, seq):
    part = pl.program_id(2)
    stored = {FOLD: (qa_ref, ka_ref, va_ref), FOLD ** 2: (qb_ref, kb_ref, vb_ref)}
    for slot, ((_, dil), fold, mask_ref) in enumerate(zip(DIL_PAIRS, _BRANCH_FOLD, (m0_ref, m1_ref, m2_ref))):
        _dil_branch(*stored[fold], mask_ref, os_ref, ls_ref, slot, part, dil=dil, sub=fold // dil,
                    length=seq // fold)

    def merge(c, carry):
        rows = pl.ds(pl.multiple_of(c * DIL_MERGE_ROWS, DIL_MERGE_ROWS), DIL_MERGE_ROWS)
        lses = [ls_ref[b, rows, :] for b in range(len(DIL_PAIRS))]
        mx = functools.reduce(jnp.maximum, lses)
        ws = [jnp.exp2(l - mx) for l in lses]
        num = functools.reduce(jnp.add, [w * os_ref[b, rows, :] for b, w in enumerate(ws)])
        o = num / functools.reduce(jnp.add, ws)
        o_ref[0, 0, rows, :] = _rms(o, g_ref[0]).astype(BF16)
        return carry

    lax.fori_loop(0, seq // DIL_PARTS // DIL_MERGE_ROWS, merge, 0)


def _dil_mask_table(sub, length):
    qn, kn = DIL_TQ // sub, _DIL_WIN // sub
    n_i = length // qn
    c = np.arange(sub)[:, None]
    out = []
    for i in (0, 1, n_i - 1):
        k0 = int(np.clip(i * qn - DIL_HALF // sub, 0, length - kn))
        qpos = ((i * qn + np.arange(qn))[None, :] * sub + c).reshape(-1)
        kpos = ((k0 + np.arange(kn))[None, :] * sub + c).reshape(-1)
        out.append(np.where(np.abs(kpos[None, :] - qpos[:, None]) <= DIL_HALF, 0.0, NEG))
    return jnp.asarray(np.stack(out), F32)


def _dilated(fa, fb, head_g, batch, seq):
    masks = []
    for (window, dil), fold in zip(DIL_PAIRS, _BRANCH_FOLD):
        sub, length = fold // dil, seq // fold
        assert window // (2 * dil) == DIL_HALF and fold % dil == 0 and DIL_TQ % sub == 0
        n_p = length // (DIL_TQ // sub) // DIL_PARTS
        assert n_p * DIL_PARTS * (DIL_TQ // sub) == length and (dil * n_p) % DIL_GROUP == 0 and n_p >= 1
        masks.append(_dil_mask_table(sub, length))

    def stored(f, first):
        return pl.BlockSpec((1, 1, f, seq // f, HEAD_DIM), lambda b, h, part: (b, first + h, 0, 0, 0))

    part_rows = seq // DIL_PARTS
    return pl.pallas_call(
        functools.partial(_dilated_kernel, seq=seq),
        name="dilated",
        grid=(batch, N_HEADS_DIL, DIL_PARTS),
        in_specs=[stored(f, first) for f in (FOLD, FOLD ** 2) for first in (0, N_HEADS_DIL, 2 * N_HEADS_DIL)]
        + [pl.BlockSpec((3, DIL_TQ, _DIL_WIN), lambda b, h, part: (0, 0, 0))] * len(DIL_PAIRS)
        + [pl.BlockSpec((1, 1, HEAD_DIM), lambda b, h, part: (N_HEADS_NA + h, 0, 0))],
        out_specs=pl.BlockSpec((1, 1, part_rows, HEAD_DIM), lambda b, h, part: (b, h, part, 0)),
        out_shape=jax.ShapeDtypeStruct((batch, N_HEADS_DIL, seq, HEAD_DIM), BF16),
        scratch_shapes=[pltpu.VMEM((len(DIL_PAIRS), part_rows, HEAD_DIM), F32)] * 2,
        compiler_params=_params("parallel", "parallel", "arbitrary"),
    )(fa, fa, fa, fb, fb, fb, *masks, head_g)


def _cast_once(w_ref, w16_ref):
    @pl.when(pl.program_id(0) == 0)
    def _():
        w16_ref[...] = w_ref[...].astype(BF16)


def _resident(shape):
    return pl.BlockSpec(shape, lambda i: (0,) * len(shape), pipeline_mode=pl.Buffered(1))


def _mix_out_kernel(ona_ref, odil_ref, wo_ref, h_ref, postg_ref, o_ref, wo16_ref):
    _cast_once(wo_ref, wo16_ref)
    heads = [ona_ref[0, h] for h in range(N_HEADS_NA)] + [odil_ref[0, h] for h in range(N_HEADS_DIL)]
    m = jnp.dot(jnp.concatenate(heads, axis=-1), wo16_ref[...], preferred_element_type=F32)
    o_ref[...] = h_ref[...] + _rms(m, postg_ref[...])


def _mix_out(o_na, o_dil, w_o, h, post_g, seq):
    t = h.shape[0]
    tiles_per_seq = seq // MIX_TM
    heads = lambda n: pl.BlockSpec((1, n, MIX_TM, HEAD_DIM), lambda i: (i // tiles_per_seq, 0, i % tiles_per_seq, 0))
    row = pl.BlockSpec((MIX_TM, D_MODEL), lambda i: (i, 0))
    const = lambda shape: pl.BlockSpec(shape, lambda i: (0, 0))
    return pl.pallas_call(
        _mix_out_kernel,
        name="mix_out",
        grid=(t // MIX_TM,),
        in_specs=[heads(N_HEADS_NA), heads(N_HEADS_DIL), _resident((D_MODEL, D_MODEL)), row, const((1, D_MODEL))],
        out_specs=row,
        out_shape=jax.ShapeDtypeStruct((t, D_MODEL), F32),
        scratch_shapes=[pltpu.VMEM((D_MODEL, D_MODEL), BF16)],
        compiler_params=_params("arbitrary"),
    )(o_na, o_dil, w_o, h, post_g)


def _ple_kernel(h_ref, p_ref, preg_ref, wg_ref, wp_ref, postg_ref, o_ref, wg16_ref, wp16_ref):
    _cast_once(wg_ref, wg16_ref)
    _cast_once(wp_ref, wp16_ref)
    h = h_ref[...]
    u = _rms(h, preg_ref[...]).astype(BF16)
    gate = jax.nn.sigmoid(jnp.dot(u, wg16_ref[...], preferred_element_type=F32))
    emb = jnp.dot(p_ref[...].astype(BF16), wp16_ref[...], preferred_element_type=F32)
    o_ref[...] = h + _rms(gate * emb, postg_ref[...])


def _ple(h, p, pre_g, w_gate, w_proj, post_g):
    t = h.shape[0]
    row = lambda width: pl.BlockSpec((PLE_TM, width), lambda i: (i, 0))
    const = lambda shape: pl.BlockSpec(shape, lambda i: (0, 0))
    return pl.pallas_call(
        _ple_kernel,
        name="ple",
        grid=(t // PLE_TM,),
        in_specs=[row(D_MODEL), row(PLE_DIM), const((1, D_MODEL)), _resident((D_MODEL, D_MODEL)),
                  _resident((PLE_DIM, D_MODEL)), const((1, D_MODEL))],
        out_specs=row(D_MODEL),
        out_shape=jax.ShapeDtypeStruct((t, D_MODEL), F32),
        scratch_shapes=[pltpu.VMEM((D_MODEL, D_MODEL), BF16), pltpu.VMEM((PLE_DIM, D_MODEL), BF16)],
        compiler_params=_params("arbitrary"),
    )(h, p, pre_g, w_gate, w_proj, post_g)


def kernel(x, p, ffn1_pre_g, ffn1_w_gate, ffn1_w_up, ffn1_w_down, ffn1_post_g, mix_pre_g, w_qkv, na_rpb, out_g, w_o, mix_post_g, ffn2_pre_g, ffn2_w_gate, ffn2_w_up, ffn2_w_down, ffn2_post_g, ple_pre_g, w_ple_gate, w_ple_proj, ple_post_g):
    batch, seq, d_model = x.shape
    depth = p.shape[0]
    assert d_model == D_MODEL and seq % (GRID_W * NA_QROWS) == 0 and seq // GRID_W >= NA_KROWS
    tokens = batch * seq
    rows = seq // GRID_W
    cos_full, sin_signed = _rope_tables(seq)
    gain = lambda g: g.reshape(1, D_MODEL)

    h = x.reshape(tokens, D_MODEL)
    for i in range(depth):
        h, (w_qkv16,) = _ffn(h, gain(ffn1_pre_g[i]), ffn1_w_gate[i], ffn1_w_up[i], ffn1_w_down[i],
                             gain(ffn1_post_g[i]), side=(w_qkv[i],))
        qkv_na, u_mix = _qkv_na(h, gain(mix_pre_g[i]), w_qkv16, batch, seq)
        fa, fb = _qkv_dil(u_mix, w_qkv16, cos_full, sin_signed, batch, seq)
        head_g = out_g[i].reshape(N_HEADS, 1, HEAD_DIM)
        o_na = _na(qkv_na, _na_bias_table(na_rpb[i], rows), head_g, batch, seq)
        o_dil = _dilated(fa, fb, head_g, batch, seq)
        h = _mix_out(o_na, o_dil, w_o[i], h, gain(mix_post_g[i]), seq)
        h, _ = _ffn(h, gain(ffn2_pre_g[i]), ffn2_w_gate[i], ffn2_w_up[i], ffn2_w_down[i], gain(ffn2_post_g[i]))
        h = _ple(h, p[i].reshape(tokens, PLE_DIM), gain(ple_pre_g[i]), w_ple_gate[i], w_ple_proj[i],
                 gain(ple_post_g[i]))
    return h.reshape(batch, seq, D_MODEL)
```

```python
import functools
import math

import jax
import jax.numpy as jnp
import numpy as np
from jax import lax
from jax.experimental import pallas as pl
from jax.experimental.pallas import tpu as pltpu

D_MODEL = 2048
D_FF = 5632
HEAD_DIM = 128
N_HEADS = 16
N_HEADS_NA = 4
N_HEADS_DIL = 12
W_NA = N_HEADS_NA * HEAD_DIM
W_DIL = N_HEADS_DIL * HEAD_DIM
GRID_W = 64
NA_ROWS = 8
NA_COLS = 16
DIL_PAIRS = ((128, 1), (512, 4), (2048, 16))
PLE_DIM = 256
ROPE_THETA = 10000.0
EPS = 1e-6
NEG = -1e30
SCALE = HEAD_DIM ** -0.5

F32 = jnp.float32
BF16 = jnp.bfloat16

VMEM_LIMIT_BYTES = 56 * 1024 * 1024
FFN_VMEM_LIMIT_BYTES = 62 * 1024 * 1024

FFN_TM = 1024
FFN_TF = 256
SIDE_ROWS = 16
QKV_TM = 1024
QKV_DIL_TM = 1024
QKV_TN = 768
FOLD = 4
NA_QROWS = 8
NA_KROWS = 16
NA_STEP_TILES = 8
NA_SUB_ROWS = 128
NA_LAG = 1
DIL_TQ = 128
DIL_GROUP = 32
DIL_LAG = 2
DIL_HALF = 64
DIL_PARTS = 2
DIL_MERGE_ROWS = 1024
MIX_TM = 512
PLE_TM = 512


def _rms(x, g):
    return x * lax.rsqrt(jnp.mean(x * x, axis=-1, keepdims=True) + EPS) * g


def _params(*sem, vmem_limit_bytes=VMEM_LIMIT_BYTES):
    return pltpu.CompilerParams(dimension_semantics=sem, vmem_limit_bytes=vmem_limit_bytes)


def _normed_input(first, x_ref, g_ref, u_ref):
    if not first:
        return u_ref[...]
    u = _rms(x_ref[...], g_ref[...]).astype(BF16)
    u_ref[...] = u
    return u


def _first_or_later(step):
    j = pl.program_id(1)
    pl.when(j == 0)(functools.partial(step, True))
    pl.when(j > 0)(functools.partial(step, False))


def _skewed(n, lag, score, softmax, finish):
    scores, probs = {}, {}
    for t in range(n + 2 * lag):
        if t < n:
            scores[t] = score(t)
        if 0 <= t - lag < n:
            probs[t - lag] = softmax(scores.pop(t - lag))
        if 0 <= t - 2 * lag < n:
            finish(t - 2 * lag, probs.pop(t - 2 * lag))


def _ffn_kernel(x_ref, pre_g_ref, wg_ref, wu_ref, wd_ref, post_g_ref, *rest, side_blocks):
    n_side = len(side_blocks)
    side_in, o_ref, side_out, u_ref = rest[:n_side], rest[n_side], rest[n_side + 1:-1], rest[-1]
    j = pl.program_id(1)
    last = pl.num_programs(1) - 1
    flat_step = pl.program_id(0) * pl.num_programs(1) + j
    for w_ref, w16_ref, n_blocks in zip(side_in, side_out, side_blocks):
        @pl.when(flat_step < n_blocks)
        def _(w_ref=w_ref, w16_ref=w16_ref):
            w16_ref[...] = w_ref[...].astype(BF16)

    def step(first, final):
        u = _normed_input(first, x_ref, pre_g_ref, u_ref)
        g = jnp.dot(u, wg_ref[...].astype(BF16), preferred_element_type=F32)
        v = jnp.dot(u, wu_ref[...].astype(BF16), preferred_element_type=F32)
        mid = (g * jax.nn.sigmoid(g) * v).astype(BF16)
        acc = jnp.dot(mid, wd_ref[...].astype(BF16), preferred_element_type=F32)
        if not first:
            acc = o_ref[...] + acc
        o_ref[...] = x_ref[...] + 0.5 * _rms(acc, post_g_ref[...]) if final else acc

    pl.when(j == 0)(functools.partial(step, True, False))
    pl.when(jnp.logical_and(j > 0, j < last))(functools.partial(step, False, False))
    pl.when(j == last)(functools.partial(step, False, True))


def _ffn(x, pre_g, w_gate, w_up, w_down, post_g, side=()):
    t = x.shape[0]
    n_j = D_FF // FFN_TF
    assert n_j >= 2
    side_blocks = tuple(w.shape[0] // SIDE_ROWS for w in side)
    assert all(w.shape[0] % SIDE_ROWS == 0 for w in side) and all(n <= t // FFN_TM * n_j for n in side_blocks)

    def side_spec(w, n_blocks):
        return pl.BlockSpec((SIDE_ROWS, w.shape[1]), lambda i, j: (jnp.minimum(i * n_j + j, n_blocks - 1), 0))

    side_specs = [side_spec(w, n) for w, n in zip(side, side_blocks)]
    out = pl.pallas_call(
        functools.partial(_ffn_kernel, side_blocks=side_blocks),
        name="ffn",
        grid=(t // FFN_TM, n_j),
        in_specs=[
            pl.BlockSpec((FFN_TM, D_MODEL), lambda i, j: (i, 0)),
            pl.BlockSpec((1, D_MODEL), lambda i, j: (0, 0)),
            pl.BlockSpec((D_MODEL, FFN_TF), lambda i, j: (0, j)),
            pl.BlockSpec((D_MODEL, FFN_TF), lambda i, j: (0, j)),
            pl.BlockSpec((FFN_TF, D_MODEL), lambda i, j: (j, 0)),
            pl.BlockSpec((1, D_MODEL), lambda i, j: (0, 0)),
        ] + side_specs,
        out_specs=[pl.BlockSpec((FFN_TM, D_MODEL), lambda i, j: (i, 0))] + side_specs,
        out_shape=[jax.ShapeDtypeStruct((t, D_MODEL), F32)] + [jax.ShapeDtypeStruct(w.shape, BF16) for w in side],
        scratch_shapes=[pltpu.VMEM((FFN_TM, D_MODEL), BF16)],
        compiler_params=_params("arbitrary", "arbitrary", vmem_limit_bytes=FFN_VMEM_LIMIT_BYTES),
    )(x, pre_g, w_gate, w_up, w_down, post_g, *side)
    return out[0], out[1:]


Q_SCALE_NA = SCALE
Q_SCALE_DIL = SCALE * math.log2(math.e)
_DIL_COL_STEPS = 3 * W_DIL // QKV_TN
_DIL_STEPS_PER_PART = W_DIL // QKV_TN
_DOT_N = 2 * HEAD_DIM


def _qkv_na_kernel(x_ref, g_ref, w_ref, colscale_ref, o_ref, u_ref):
    def step(first):
        u = _normed_input(first, x_ref, g_ref, u_ref)
        y = jnp.dot(u, w_ref[...], preferred_element_type=F32) * colscale_ref[...]
        for h in range(QKV_TN // HEAD_DIM):
            o_ref[0, h] = y[:, h * HEAD_DIM:(h + 1) * HEAD_DIM].astype(BF16)

    _first_or_later(step)


def _qkv_na(x, g, w, batch, seq):
    t = x.shape[0]
    tiles_per_seq = seq // QKV_TM
    colscale = jnp.concatenate([jnp.full((1, W_NA), Q_SCALE_NA, F32), jnp.ones((1, 2 * W_NA), F32)], axis=-1)
    return pl.pallas_call(
        _qkv_na_kernel,
        name="qkv_na",
        grid=(t // QKV_TM, 3 * W_NA // QKV_TN),
        in_specs=[
            pl.BlockSpec((QKV_TM, D_MODEL), lambda i, j: (i, 0)),
            pl.BlockSpec((1, D_MODEL), lambda i, j: (0, 0)),
            pl.BlockSpec((D_MODEL, QKV_TN), lambda i, j: (0, j)),
            pl.BlockSpec((1, QKV_TN), lambda i, j: (0, j)),
        ],
        out_specs=[pl.BlockSpec((1, QKV_TN // HEAD_DIM, QKV_TM, HEAD_DIM),
                                lambda i, j: (i // tiles_per_seq, j, i % tiles_per_seq, 0)),
                   pl.BlockSpec((QKV_TM, D_MODEL), lambda i, j: (i, 0))],
        out_shape=[jax.ShapeDtypeStruct((batch, 3 * N_HEADS_NA, seq, HEAD_DIM), BF16),
                   jax.ShapeDtypeStruct((t, D_MODEL), BF16)],
        compiler_params=_params("parallel", "arbitrary"),
    )(x, g, w, colscale)


def _qkv_dil_kernel(u_ref, w_ref, cos_ref, sin_ref, fa_ref, fb_ref, y_ref, y4_ref):
    j = pl.program_id(1)
    rotary = j < 2 * _DIL_STEPS_PER_PART
    scale = jnp.where(j < _DIL_STEPS_PER_PART, Q_SCALE_DIL, 1.0).astype(F32)

    u = u_ref[...]
    cos = cos_ref[...]
    sin = sin_ref[...]
    for pair in range(QKV_TN // _DOT_N):
        y2 = jnp.dot(u, w_ref[:, pair * _DOT_N:(pair + 1) * _DOT_N], preferred_element_type=F32)
        for half in range(_DOT_N // HEAD_DIM):
            h = pair * (_DOT_N // HEAD_DIM) + half
            y = y2[:, half * HEAD_DIM:(half + 1) * HEAD_DIM]
            y_ref[h] = jnp.where(rotary, y * cos + pltpu.roll(y, HEAD_DIM // 2, axis=1) * sin, y) * scale
            for r in range(FOLD):
                y4 = y_ref[h, pl.ds(r, QKV_DIL_TM // FOLD, stride=FOLD), :]
                fa_ref[0, h, r] = y4.astype(BF16)
                y4_ref[h, r] = y4
                for c in range(FOLD):
                    fb_ref[0, h, r + FOLD * c] = (
                        y4_ref[h, r, pl.ds(c, QKV_DIL_TM // FOLD ** 2, stride=FOLD), :].astype(BF16))


def _qkv_dil(u, w, cos_full, sin_signed, batch, seq):
    t = u.shape[0]
    tm = QKV_DIL_TM
    tiles_per_seq = seq // tm
    first_col_step = 3 * W_NA // QKV_TN

    def fold_spec(f):
        return pl.BlockSpec((1, QKV_TN // HEAD_DIM, f, tm // f, HEAD_DIM),
                            lambda i, j: (i // tiles_per_seq, j, 0, i % tiles_per_seq, 0))

    return pl.pallas_call(
        _qkv_dil_kernel,
        name="qkv_dil",
        grid=(t // tm, _DIL_COL_STEPS),
        in_specs=[
            pl.BlockSpec((tm, D_MODEL), lambda i, j: (i, 0)),
            pl.BlockSpec((D_MODEL, QKV_TN), lambda i, j: (0, first_col_step + j)),
            pl.BlockSpec((tm, HEAD_DIM), lambda i, j: (i % tiles_per_seq, 0)),
            pl.BlockSpec((tm, HEAD_DIM), lambda i, j: (i % tiles_per_seq, 0)),
        ],
        out_specs=[fold_spec(f) for f in (FOLD, FOLD ** 2)],
        out_shape=[jax.ShapeDtypeStruct((batch, 3 * N_HEADS_DIL, f, seq // f, HEAD_DIM), BF16)
                   for f in (FOLD, FOLD ** 2)],
        scratch_shapes=[pltpu.VMEM((QKV_TN // HEAD_DIM, tm, HEAD_DIM), F32),
                        pltpu.VMEM((QKV_TN // HEAD_DIM, FOLD, tm // FOLD, HEAD_DIM), F32)],
        compiler_params=_params("parallel", "arbitrary"),
    )(u, w, cos_full, sin_signed)


def _rope_tables(seq):
    inv = jnp.float32(ROPE_THETA) ** (-jnp.arange(0, HEAD_DIM, 2, dtype=F32) / HEAD_DIM)
    ang = jnp.arange(seq, dtype=F32)[:, None] * inv[None, :]
    cos, sin = jnp.cos(ang), jnp.sin(ang)
    return jnp.concatenate([cos, cos], axis=-1), jnp.concatenate([-sin, sin], axis=-1)


def _na_tile_key_row_start(i, rows):
    return jnp.clip(i * NA_QROWS - NA_ROWS // 2, 0, rows - NA_KROWS)


def _na_kernel(q_ref, k_ref, v_ref, bias_ref, g_ref, o_ref, *, rows):
    i = pl.program_id(2)
    nk = NA_KROWS * GRID_W
    n_tiles = rows // NA_QROWS
    subs_per_tile = NA_QROWS * GRID_W // NA_SUB_ROWS
    tiles = [i * NA_STEP_TILES + t for t in range(NA_STEP_TILES)]
    starts = [pl.multiple_of(_na_tile_key_row_start(tile, rows) * GRID_W, GRID_W) for tile in tiles]
    classes = [jnp.where(tile == 0, 0, jnp.where(tile == n_tiles - 1, 2, 1)) for tile in tiles]
    sub = lambda n: slice(n * NA_SUB_ROWS, (n + 1) * NA_SUB_ROWS)

    def score(n):
        t, m = divmod(n, subs_per_tile)
        k = k_ref[0, 0, pl.ds(starts[t], nk), :]
        return lax.dot_general(q_ref[0, 0, sub(n), :], k, (((1,), (1,)), ((), ())),
                               preferred_element_type=F32) + bias_ref[classes[t], 0, sub(m), :]

    def softmax(s):
        e = jnp.exp(s - jnp.max(s, axis=-1, keepdims=True))
        return e.astype(BF16), jnp.sum(e, axis=-1, keepdims=True)

    def finish(n, prob):
        e, den = prob
        v = v_ref[0, 0, pl.ds(starts[n // subs_per_tile], nk), :]
        o = jnp.dot(e, v, preferred_element_type=F32) / den
        o_ref[0, 0, sub(n), :] = _rms(o, g_ref[0]).astype(BF16)

    _skewed(NA_STEP_TILES * subs_per_tile, NA_LAG, score, softmax, finish)


def _na_bias_table(rpb, rows):
    n_tiles = rows // NA_QROWS

    def row_pairs(tile):
        ks = int(np.clip(tile * NA_QROWS - NA_ROWS // 2, 0, rows - NA_KROWS))
        r = tile * NA_QROWS + np.arange(NA_QROWS)
        rs = np.clip(r - NA_ROWS // 2, 0, rows - NA_ROWS)
        kr = ks + np.arange(NA_KROWS)
        ok = (kr[None, :] >= rs[:, None]) & (kr[None, :] < rs[:, None] + NA_ROWS)
        assert (ok.sum(axis=1) == NA_ROWS).all()
        return ok, np.clip(kr[None, :] - r[:, None] + NA_ROWS - 1, 0, 2 * NA_ROWS - 2)

    row_ok, dr_idx = (np.stack(a) for a in zip(*(row_pairs(tile) for tile in (0, 1, n_tiles - 1))))
    for tile in range(1, n_tiles - 1):
        ok, dr = row_pairs(tile)
        assert (ok == row_ok[1]).all() and (np.where(ok, dr, 0) == np.where(ok, dr_idx[1], 0)).all()
    c = np.arange(GRID_W)
    qs = np.clip(c - NA_COLS // 2, 0, GRID_W - NA_COLS)
    col_ok = (c[None, :] >= qs[:, None]) & (c[None, :] < qs[:, None] + NA_COLS)
    n_dr, n_dc = 2 * NA_ROWS - 1, 2 * NA_COLS - 1
    period = 2 * GRID_W
    v = jnp.concatenate([rpb[..., NA_COLS - 1:], jnp.zeros((N_HEADS_NA, n_dr, period - n_dc), F32),
                         rpb[..., :NA_COLS - 1]], axis=-1).astype(F32)
    toe = jnp.tile(v, (1, 1, GRID_W))[..., :GRID_W * (period - 1)]
    toe = toe.reshape(N_HEADS_NA, n_dr, GRID_W, period - 1)[..., :GRID_W]
    slabs = jnp.where(col_ok[None, None], toe, NEG)
    slabs = jnp.concatenate([slabs, jnp.full((N_HEADS_NA, 1, GRID_W, GRID_W), NEG, F32)], axis=1)
    slab_idx = np.where(row_ok, dr_idx, n_dr)
    return pl.pallas_call(
        functools.partial(_na_bias_kernel, slab_idx=slab_idx),
        name="na_bias",
        grid=(N_HEADS_NA,),
        in_specs=[pl.BlockSpec((1, n_dr + 1, GRID_W, GRID_W), lambda h: (h, 0, 0, 0))],
        out_specs=pl.BlockSpec((3, 1, NA_QROWS * GRID_W, NA_KROWS * GRID_W), lambda h: (0, h, 0, 0)),
        out_shape=jax.ShapeDtypeStruct((3, N_HEADS_NA, NA_QROWS * GRID_W, NA_KROWS * GRID_W), F32),
        compiler_params=_params("parallel"),
    )(slabs)


def _na_bias_kernel(slabs_ref, o_ref, *, slab_idx):
    n_cls, n_q, n_k = slab_idx.shape
    for cls in range(n_cls):
        for rq in range(n_q):
            row = jnp.concatenate([slabs_ref[0, int(slab_idx[cls, rq, rk])] for rk in range(n_k)], axis=-1)
            o_ref[cls, 0, rq * GRID_W:(rq + 1) * GRID_W, :] = row


def _na(qkv, bias, head_g, batch, seq):
    rows = seq // GRID_W
    n_tiles = rows // NA_QROWS
    assert n_tiles % NA_STEP_TILES == 0
    tq = NA_QROWS * GRID_W

    return pl.pallas_call(
        functools.partial(_na_kernel, rows=rows),
        name="na",
        grid=(batch, N_HEADS_NA, n_tiles // NA_STEP_TILES),
        in_specs=[
            pl.BlockSpec((1, 1, NA_STEP_TILES * tq, HEAD_DIM), lambda b, h, i: (b, h, i, 0)),
            pl.BlockSpec((1, 1, seq, HEAD_DIM), lambda b, h, i: (b, N_HEADS_NA + h, 0, 0)),
            pl.BlockSpec((1, 1, seq, HEAD_DIM), lambda b, h, i: (b, 2 * N_HEADS_NA + h, 0, 0)),
            pl.BlockSpec((3, 1, tq, NA_KROWS * GRID_W), lambda b, h, i: (0, h, 0, 0)),
            pl.BlockSpec((1, 1, HEAD_DIM), lambda b, h, i: (h, 0, 0)),
        ],
        out_specs=pl.BlockSpec((1, 1, NA_STEP_TILES * tq, HEAD_DIM), lambda b, h, i: (b, h, i, 0)),
        out_shape=jax.ShapeDtypeStruct((batch, N_HEADS_NA, seq, HEAD_DIM), BF16),
        compiler_params=_params("parallel", "parallel", "arbitrary"),
    )(qkv, qkv, qkv, bias, head_g)


_DIL_WIN = DIL_TQ + 2 * DIL_HALF
_BRANCH_FOLD = tuple(max(dil, FOLD) for (_, dil) in DIL_PAIRS)


def _dil_branch(q_ref, k_ref, v_ref, mask_ref, os_ref, ls_ref, slot, part, *, dil, sub, length):
    fold = dil * sub
    qn, kn = DIL_TQ // sub, _DIL_WIN // sub
    n_i = length // qn
    n_p = n_i // DIL_PARTS
    row0 = part * (length // DIL_PARTS)

    def gather(ref, p, start, size):
        parts = [ref[0, 0, p + dil * c, pl.ds(start, size), :] for c in range(sub)]
        return parts[0] if sub == 1 else jnp.concatenate(parts, axis=0)

    def group(g, carry):
        tiles = []
        for n in range(DIL_GROUP):
            t = g * DIL_GROUP + n
            p = t // n_p
            i = part * n_p + t % n_p
            q0 = pl.multiple_of(i * qn, qn)
            k0 = pl.multiple_of(jnp.clip(q0 - DIL_HALF // sub, 0, length - kn), DIL_HALF // sub)
            edge = jnp.where(i == 0, 0, jnp.where(i == n_i - 1, 2, 1))
            tiles.append((p, q0, k0, edge))

        def score(n):
            p, q0, k0, edge = tiles[n]
            return lax.dot_general(gather(q_ref, p, q0, qn), gather(k_ref, p, k0, kn), (((1,), (1,)), ((), ())),
                                   preferred_element_type=F32) + mask_ref[edge]

        def softmax(s):
            m = jnp.max(s, axis=-1, keepdims=True)
            e = jnp.exp2(s - m)
            return m, e.astype(BF16), jnp.sum(e, axis=-1, keepdims=True)

        def finish(n, prob):
            p, q0, k0, edge = tiles[n]
            m, e, den = prob
            o = jnp.dot(e, gather(v_ref, p, k0, kn), preferred_element_type=F32) / den
            lse = jnp.broadcast_to(m + jnp.log2(den), (DIL_TQ, HEAD_DIM))
            for c in range(sub):
                rows = pl.ds(fold * (q0 - row0) + p + dil * c, qn, stride=fold)
                os_ref[slot, rows, :] = o[c * qn:(c + 1) * qn]
                ls_ref[slot, rows, :] = lse[c * qn:(c + 1) * qn]

        _skewed(DIL_GROUP, DIL_LAG, score, softmax, finish)
        return carry

    lax.fori_loop(0, dil * n_p // DIL_GROUP, group, 0)


def _dilated_kernel(qa_ref, ka_ref, va_ref, qb_ref, kb_ref, vb_ref, m0_ref, m1_ref, m2_ref, g_ref, o_ref,
                    os_ref, ls_ref, *, seq):
    part = pl.program_id(2)
    stored = {FOLD: (qa_ref, ka_ref, va_ref), FOLD ** 2: (qb_ref, kb_ref, vb_ref)}
    for slot, ((_, dil), fold, mask_ref) in enumerate(zip(DIL_PAIRS, _BRANCH_FOLD, (m0_ref, m1_ref, m2_ref))):
        _dil_branch(*stored[fold], mask_ref, os_ref, ls_ref, slot, part, dil=dil, sub=fold // dil,
                    length=seq // fold)

    def merge(c, carry):
        rows = pl.ds(pl.multiple_of(c * DIL_MERGE_ROWS, DIL_MERGE_ROWS), DIL_MERGE_ROWS)
        lses = [ls_ref[b, rows, :] for b in range(len(DIL_PAIRS))]
        mx = functools.reduce(jnp.maximum, lses)
        ws = [jnp.exp2(l - mx) for l in lses]
        num = functools.reduce(jnp.add, [w * os_ref[b, rows, :] for b, w in enumerate(ws)])
        o = num / functools.reduce(jnp.add, ws)
        o_ref[0, 0, rows, :] = _rms(o, g_ref[0]).astype(BF16)
        return carry

    lax.fori_loop(0, seq // DIL_PARTS // DIL_MERGE_ROWS, merge, 0)


def _dil_mask_table(sub, length):
    qn, kn = DIL_TQ // sub, _DIL_WIN // sub
    n_i = length // qn
    c = np.arange(sub)[:, None]
    out = []
    for i in (0, 1, n_i - 1):
        k0 = int(np.clip(i * qn - DIL_HALF // sub, 0, length - kn))
        qpos = ((i * qn + np.arange(qn))[None, :] * sub + c).reshape(-1)
        kpos = ((k0 + np.arange(kn))[None, :] * sub + c).reshape(-1)
        out.append(np.where(np.abs(kpos[None, :] - qpos[:, None]) <= DIL_HALF, 0.0, NEG))
    return jnp.asarray(np.stack(out), F32)


def _dilated(fa, fb, head_g, batch, seq):
    masks = []
    for (window, dil), fold in zip(DIL_PAIRS, _BRANCH_FOLD):
        sub, length = fold // dil, seq // fold
        assert window // (2 * dil) == DIL_HALF and fold % dil == 0 and DIL_TQ % sub == 0
        n_p = length // (DIL_TQ // sub) // DIL_PARTS
        assert n_p * DIL_PARTS * (DIL_TQ // sub) == length and (dil * n_p) % DIL_GROUP == 0 and n_p >= 1
        masks.append(_dil_mask_table(sub, length))

    def stored(f, first):
        return pl.BlockSpec((1, 1, f, seq // f, HEAD_DIM), lambda b, h, part: (b, first + h, 0, 0, 0))

    part_rows = seq // DIL_PARTS
    return pl.pallas_call(
        functools.partial(_dilated_kernel, seq=seq),
        name="dilated",
        grid=(batch, N_HEADS_DIL, DIL_PARTS),
        in_specs=[stored(f, first) for f in (FOLD, FOLD ** 2) for first in (0, N_HEADS_DIL, 2 * N_HEADS_DIL)]
        + [pl.BlockSpec((3, DIL_TQ, _DIL_WIN), lambda b, h, part: (0, 0, 0))] * len(DIL_PAIRS)
        + [pl.BlockSpec((1, 1, HEAD_DIM), lambda b, h, part: (N_HEADS_NA + h, 0, 0))],
        out_specs=pl.BlockSpec((1, 1, part_rows, HEAD_DIM), lambda b, h, part: (b, h, part, 0)),
        out_shape=jax.ShapeDtypeStruct((batch, N_HEADS_DIL, seq, HEAD_DIM), BF16),
        scratch_shapes=[pltpu.VMEM((len(DIL_PAIRS), part_rows, HEAD_DIM), F32)] * 2,
        compiler_params=_params("parallel", "parallel", "arbitrary"),
    )(fa, fa, fa, fb, fb, fb, *masks, head_g)


def _cast_once(w_ref, w16_ref):
    @pl.when(pl.program_id(0) == 0)
    def _():
        w16_ref[...] = w_ref[...].astype(BF16)


def _resident(shape):
    return pl.BlockSpec(shape, lambda i: (0,) * len(shape), pipeline_mode=pl.Buffered(1))


def _mix_out_kernel(ona_ref, odil_ref, wo_ref, h_ref, postg_ref, o_ref, wo16_ref):
    _cast_once(wo_ref, wo16_ref)
    heads = [ona_ref[0, h] for h in range(N_HEADS_NA)] + [odil_ref[0, h] for h in range(N_HEADS_DIL)]
    m = jnp.dot(jnp.concatenate(heads, axis=-1), wo16_ref[...], preferred_element_type=F32)
    o_ref[...] = h_ref[...] + _rms(m, postg_ref[...])


def _mix_out(o_na, o_dil, w_o, h, post_g, seq):
    t = h.shape[0]
    tiles_per_seq = seq // MIX_TM
    heads = lambda n: pl.BlockSpec((1, n, MIX_TM, HEAD_DIM), lambda i: (i // tiles_per_seq, 0, i % tiles_per_seq, 0))
    row = pl.BlockSpec((MIX_TM, D_MODEL), lambda i: (i, 0))
    const = lambda shape: pl.BlockSpec(shape, lambda i: (0, 0))
    return pl.pallas_call(
        _mix_out_kernel,
        name="mix_out",
        grid=(t // MIX_TM,),
        in_specs=[heads(N_HEADS_NA), heads(N_HEADS_DIL), _resident((D_MODEL, D_MODEL)), row, const((1, D_MODEL))],
        out_specs=row,
        out_shape=jax.ShapeDtypeStruct((t, D_MODEL), F32),
        scratch_shapes=[pltpu.VMEM((D_MODEL, D_MODEL), BF16)],
        compiler_params=_params("arbitrary"),
    )(o_na, o_dil, w_o, h, post_g)


def _ple_kernel(h_ref, p_ref, preg_ref, wg_ref, wp_ref, postg_ref, o_ref, wg16_ref, wp16_ref):
    _cast_once(wg_ref, wg16_ref)
    _cast_once(wp_ref, wp16_ref)
    h = h_ref[...]
    u = _rms(h, preg_ref[...]).astype(BF16)
    gate = jax.nn.sigmoid(jnp.dot(u, wg16_ref[...], preferred_element_type=F32))
    emb = jnp.dot(p_ref[...].astype(BF16), wp16_ref[...], preferred_element_type=F32)
    o_ref[...] = h + _rms(gate * emb, postg_ref[...])


def _ple(h, p, pre_g, w_gate, w_proj, post_g):
    t = h.shape[0]
    row = lambda width: pl.BlockSpec((PLE_TM, width), lambda i: (i, 0))
    const = lambda shape: pl.BlockSpec(shape, lambda i: (0, 0))
    return pl.pallas_call(
        _ple_kernel,
        name="ple",
        grid=(t // PLE_TM,),
        in_specs=[row(D_MODEL), row(PLE_DIM), const((1, D_MODEL)), _resident((D_MODEL, D_MODEL)),
                  _resident((PLE_DIM, D_MODEL)), const((1, D_MODEL))],
        out_specs=row(D_MODEL),
        out_shape=jax.ShapeDtypeStruct((t, D_MODEL), F32),
        scratch_shapes=[pltpu.VMEM((D_MODEL, D_MODEL), BF16), pltpu.VMEM((PLE_DIM, D_MODEL), BF16)],
        compiler_params=_params("arbitrary"),
    )(h, p, pre_g, w_gate, w_proj, post_g)


def kernel(x, p, ffn1_pre_g, ffn1_w_gate, ffn1_w_up, ffn1_w_down, ffn1_post_g, mix_pre_g, w_qkv, na_rpb, out_g, w_o, mix_post_g, ffn2_pre_g, ffn2_w_gate, ffn2_w_up, ffn2_w_down, ffn2_post_g, ple_pre_g, w_ple_gate, w_ple_proj, ple_post_g):
    batch, seq, d_model = x.shape
    depth = p.shape[0]
    assert d_model == D_MODEL and seq % (GRID_W * NA_QROWS) == 0 and seq // GRID_W >= NA_KROWS
    tokens = batch * seq
    rows = seq // GRID_W
    cos_full, sin_signed = _rope_tables(seq)
    gain = lambda g: g.reshape(1, D_MODEL)

    h = x.reshape(tokens, D_MODEL)
    for i in range(depth):
        h, (w_qkv16,) = _ffn(h, gain(ffn1_pre_g[i]), ffn1_w_gate[i], ffn1_w_up[i], ffn1_w_down[i],
                             gain(ffn1_post_g[i]), side=(w_qkv[i],))
        qkv_na, u_mix = _qkv_na(h, gain(mix_pre_g[i]), w_qkv16, batch, seq)
        fa, fb = _qkv_dil(u_mix, w_qkv16, cos_full, sin_signed, batch, seq)
        head_g = out_g[i].reshape(N_HEADS, 1, HEAD_DIM)
        o_na = _na(qkv_na, _na_bias_table(na_rpb[i], rows), head_g, batch, seq)
        o_dil = _dilated(fa, fb, head_g, batch, seq)
        h = _mix_out(o_na, o_dil, w_o[i], h, gain(mix_post_g[i]), seq)
        h, _ = _ffn(h, gain(ffn2_pre_g[i]), ffn2_w_gate[i], ffn2_w_up[i], ffn2_w_down[i], gain(ffn2_post_g[i]))
        h = _ple(h, p[i].reshape(tokens, PLE_DIM), gain(ple_pre_g[i]), w_ple_gate[i], w_ple_proj[i],
                 gain(ple_post_g[i]))
    return h.reshape(batch, seq, D_MODEL)
```

```python
import functools
import math

import jax
import jax.numpy as jnp
import numpy as np
from jax import lax
from jax.experimental import pallas as pl
from jax.experimental.pallas import tpu as pltpu

D_MODEL = 2048
D_FF = 5632
HEAD_DIM = 128
N_HEADS = 16
N_HEADS_NA = 4
N_HEADS_DIL = 12
W_NA = N_HEADS_NA * HEAD_DIM
W_DIL = N_HEADS_DIL * HEAD_DIM
GRID_W = 64
NA_ROWS = 8
NA_COLS = 16
DIL_PAIRS = ((128, 1), (512, 4), (2048, 16))
PLE_DIM = 256
ROPE_THETA = 10000.0
EPS = 1e-6
NEG = -1e30
SCALE = HEAD_DIM ** -0.5

F32 = jnp.float32
BF16 = jnp.bfloat16

VMEM_LIMIT_BYTES = 56 * 1024 * 1024
FFN_VMEM_LIMIT_BYTES = 62 * 1024 * 1024

FFN_TM = 1024
FFN_TF = 256
SIDE_ROWS = 16
QKV_TM = 1024
QKV_TN = 768
FOLD = 4
NA_QROWS = 8
NA_KROWS = 16
NA_STEP_TILES = 8
NA_SUB_ROWS = 128
NA_LAG = 1
DIL_TQ = 128
DIL_GROUP = 32
DIL_LAG = 2
DIL_HALF = 64
DIL_PARTS = 2
DIL_MERGE_ROWS = 1024
MIX_TM = 512
PLE_TM = 512


def _rms(x, g):
    return x * lax.rsqrt(jnp.mean(x * x, axis=-1, keepdims=True) + EPS) * g


def _params(*sem, vmem_limit_bytes=VMEM_LIMIT_BYTES):
    return pltpu.CompilerParams(dimension_semantics=sem, vmem_limit_bytes=vmem_limit_bytes)


def _normed_input(first, x_ref, g_ref, u_ref):
    if not first:
        return u_ref[...]
    u = _rms(x_ref[...], g_ref[...]).astype(BF16)
    u_ref[...] = u
    return u


def _first_or_later(step):
    j = pl.program_id(1)
    pl.when(j == 0)(functools.partial(step, True))
    pl.when(j > 0)(functools.partial(step, False))


def _skewed(n, lag, score, softmax, finish):
    scores, probs = {}, {}
    for t in range(n + 2 * lag):
        if t < n:
            scores[t] = score(t)
        if 0 <= t - lag < n:
            probs[t - lag] = softmax(scores.pop(t - lag))
        if 0 <= t - 2 * lag < n:
            finish(t - 2 * lag, probs.pop(t - 2 * lag))


def _ffn_kernel(x_ref, pre_g_ref, wg_ref, wu_ref, wd_ref, post_g_ref, *rest, side_blocks):
    n_side = len(side_blocks)
    side_in, o_ref, side_out, u_ref = rest[:n_side], rest[n_side], rest[n_side + 1:-1], rest[-1]
    j = pl.program_id(1)
    last = pl.num_programs(1) - 1
    flat_step = pl.program_id(0) * pl.num_programs(1) + j
    for w_ref, w16_ref, n_blocks in zip(side_in, side_out, side_blocks):
        @pl.when(flat_step < n_blocks)
        def _(w_ref=w_ref, w16_ref=w16_ref):
            w16_ref[...] = w_ref[...].astype(BF16)

    def step(first, final):
        u = _normed_input(first, x_ref, pre_g_ref, u_ref)
        g = jnp.dot(u, wg_ref[...].astype(BF16), preferred_element_type=F32)
        v = jnp.dot(u, wu_ref[...].astype(BF16), preferred_element_type=F32)
        mid = (g * jax.nn.sigmoid(g) * v).astype(BF16)
        acc = jnp.dot(mid, wd_ref[...].astype(BF16), preferred_element_type=F32)
        if not first:
            acc = o_ref[...] + acc
        o_ref[...] = x_ref[...] + 0.5 * _rms(acc, post_g_ref[...]) if final else acc

    pl.when(j == 0)(functools.partial(step, True, False))
    pl.when(jnp.logical_and(j > 0, j < last))(functools.partial(step, False, False))
    pl.when(j == last)(functools.partial(step, False, True))


def _ffn(x, pre_g, w_gate, w_up, w_down, post_g, side=()):
    t = x.shape[0]
    n_j = D_FF // FFN_TF
    assert n_j >= 2
    side_blocks = tuple(w.shape[0] // SIDE_ROWS for w in side)
    assert all(w.shape[0] % SIDE_ROWS == 0 for w in side) and all(n <= t // FFN_TM * n_j for n in side_blocks)

    def side_spec(w, n_blocks):
        return pl.BlockSpec((SIDE_ROWS, w.shape[1]), lambda i, j: (jnp.minimum(i * n_j + j, n_blocks - 1), 0))

    side_specs = [side_spec(w, n) for w, n in zip(side, side_blocks)]
    out = pl.pallas_call(
        functools.partial(_ffn_kernel, side_blocks=side_blocks),
        name="ffn",
        grid=(t // FFN_TM, n_j),
        in_specs=[
            pl.BlockSpec((FFN_TM, D_MODEL), lambda i, j: (i, 0)),
            pl.BlockSpec((1, D_MODEL), lambda i, j: (0, 0)),
            pl.BlockSpec((D_MODEL, FFN_TF), lambda i, j: (0, j)),
            pl.BlockSpec((D_MODEL, FFN_TF), lambda i, j: (0, j)),
            pl.BlockSpec((FFN_TF, D_MODEL), lambda i, j: (j, 0)),
            pl.BlockSpec((1, D_MODEL), lambda i, j: (0, 0)),
        ] + side_specs,
        out_specs=[pl.BlockSpec((FFN_TM, D_MODEL), lambda i, j: (i, 0))] + side_specs,
        out_shape=[jax.ShapeDtypeStruct((t, D_MODEL), F32)] + [jax.ShapeDtypeStruct(w.shape, BF16) for w in side],
        scratch_shapes=[pltpu.VMEM((FFN_TM, D_MODEL), BF16)],
        compiler_params=_params("arbitrary", "arbitrary", vmem_limit_bytes=FFN_VMEM_LIMIT_BYTES),
    )(x, pre_g, w_gate, w_up, w_down, post_g, *side)
    return out[0], out[1:]


Q_SCALE_NA = SCALE
Q_SCALE_DIL = SCALE * math.log2(math.e)
_DIL_COL_STEPS = 3 * W_DIL // QKV_TN
_DIL_STEPS_PER_PART = W_DIL // QKV_TN
_DOT_N = 2 * HEAD_DIM


def _qkv_na_kernel(x_ref, g_ref, w_ref, colscale_ref, o_ref, u_ref):
    def step(first):
        u = _normed_input(first, x_ref, g_ref, u_ref)
        y = jnp.dot(u, w_ref[...], preferred_element_type=F32) * colscale_ref[...]
        for h in range(QKV_TN // HEAD_DIM):
            o_ref[0, h] = y[:, h * HEAD_DIM:(h + 1) * HEAD_DIM].astype(BF16)

    _first_or_later(step)


def _qkv_na(x, g, w, batch, seq):
    t = x.shape[0]
    tiles_per_seq = seq // QKV_TM
    colscale = jnp.concatenate([jnp.full((1, W_NA), Q_SCALE_NA, F32), jnp.ones((1, 2 * W_NA), F32)], axis=-1)
    return pl.pallas_call(
        _qkv_na_kernel,
        name="qkv_na",
        grid=(t // QKV_TM, 3 * W_NA // QKV_TN),
        in_specs=[
            pl.BlockSpec((QKV_TM, D_MODEL), lambda i, j: (i, 0)),
            pl.BlockSpec((1, D_MODEL), lambda i, j: (0, 0)),
            pl.BlockSpec((D_MODEL, QKV_TN), lambda i, j: (0, j)),
            pl.BlockSpec((1, QKV_TN), lambda i, j: (0, j)),
        ],
        out_specs=[pl.BlockSpec((1, QKV_TN // HEAD_DIM, QKV_TM, HEAD_DIM),
                                lambda i, j: (i // tiles_per_seq, j, i % tiles_per_seq, 0)),
                   pl.BlockSpec((QKV_TM, D_MODEL), lambda i, j: (i, 0))],
        out_shape=[jax.ShapeDtypeStruct((batch, 3 * N_HEADS_NA, seq, HEAD_DIM), BF16),
                   jax.ShapeDtypeStruct((t, D_MODEL), BF16)],
        compiler_params=_params("parallel", "arbitrary"),
    )(x, g, w, colscale)


def _qkv_dil_kernel(u_ref, w_ref, cos_ref, sin_ref, fa_ref, fb_ref, y_ref, y4_ref):
    def step(rotary, scale):
        u = u_ref[...]
        for pair in range(QKV_TN // _DOT_N):
            y2 = jnp.dot(u, w_ref[:, pair * _DOT_N:(pair + 1) * _DOT_N], preferred_element_type=F32)
            for half in range(_DOT_N // HEAD_DIM):
                h = pair * (_DOT_N // HEAD_DIM) + half
                y = y2[:, half * HEAD_DIM:(half + 1) * HEAD_DIM]
                if rotary:
                    y = y * cos_ref[...] + pltpu.roll(y, HEAD_DIM // 2, axis=1) * sin_ref[...]
                y_ref[h] = y if scale is None else y * scale
                for r in range(FOLD):
                    y4 = y_ref[h, pl.ds(r, QKV_TM // FOLD, stride=FOLD), :]
                    fa_ref[0, h, r] = y4.astype(BF16)
                    y4_ref[h, r] = y4
                    for c in range(FOLD):
                        fb_ref[0, h, r + FOLD * c] = (
                            y4_ref[h, r, pl.ds(c, QKV_TM // FOLD ** 2, stride=FOLD), :].astype(BF16))

    part = pl.program_id(1) // _DIL_STEPS_PER_PART
    pl.when(part == 0)(functools.partial(step, True, Q_SCALE_DIL))
    pl.when(part == 1)(functools.partial(step, True, None))
    pl.when(part == 2)(functools.partial(step, False, None))


def _qkv_dil(u, w, cos_full, sin_signed, batch, seq):
    t = u.shape[0]
    tm = QKV_TM
    tiles_per_seq = seq // tm
    first_col_step = 3 * W_NA // QKV_TN

    def fold_spec(f):
        return pl.BlockSpec((1, QKV_TN // HEAD_DIM, f, tm // f, HEAD_DIM),
                            lambda i, j: (i // tiles_per_seq, j, 0, i % tiles_per_seq, 0))

    return pl.pallas_call(
        _qkv_dil_kernel,
        name="qkv_dil",
        grid=(t // tm, _DIL_COL_STEPS),
        in_specs=[
            pl.BlockSpec((tm, D_MODEL), lambda i, j: (i, 0)),
            pl.BlockSpec((D_MODEL, QKV_TN), lambda i, j: (0, first_col_step + j)),
            pl.BlockSpec((tm, HEAD_DIM), lambda i, j: (i % tiles_per_seq, 0)),
            pl.BlockSpec((tm, HEAD_DIM), lambda i, j: (i % tiles_per_seq, 0)),
        ],
        out_specs=[fold_spec(f) for f in (FOLD, FOLD ** 2)],
        out_shape=[jax.ShapeDtypeStruct((batch, 3 * N_HEADS_DIL, f, seq // f, HEAD_DIM), BF16)
                   for f in (FOLD, FOLD ** 2)],
        scratch_shapes=[pltpu.VMEM((QKV_TN // HEAD_DIM, tm, HEAD_DIM), F32),
                        pltpu.VMEM((QKV_TN // HEAD_DIM, FOLD, tm // FOLD, HEAD_DIM), F32)],
        compiler_params=_params("parallel", "arbitrary"),
    )(u, w, cos_full, sin_signed)


def _rope_tables(seq):
    inv = jnp.float32(ROPE_THETA) ** (-jnp.arange(0, HEAD_DIM, 2, dtype=F32) / HEAD_DIM)
    ang = jnp.arange(seq, dtype=F32)[:, None] * inv[None, :]
    cos, sin = jnp.cos(ang), jnp.sin(ang)
    return jnp.concatenate([cos, cos], axis=-1), jnp.concatenate([-sin, sin], axis=-1)


def _na_tile_key_row_start(i, rows):
    return jnp.clip(i * NA_QROWS - NA_ROWS // 2, 0, rows - NA_KROWS)


def _na_kernel(q_ref, k_ref, v_ref, bias_ref, g_ref, o_ref, *, rows):
    i = pl.program_id(2)
    nk = NA_KROWS * GRID_W
    n_tiles = rows // NA_QROWS
    subs_per_tile = NA_QROWS * GRID_W // NA_SUB_ROWS
    tiles = [i * NA_STEP_TILES + t for t in range(NA_STEP_TILES)]
    starts = [pl.multiple_of(_na_tile_key_row_start(tile, rows) * GRID_W, GRID_W) for tile in tiles]
    classes = [jnp.where(tile == 0, 0, jnp.where(tile == n_tiles - 1, 2, 1)) for tile in tiles]
    sub = lambda n: slice(n * NA_SUB_ROWS, (n + 1) * NA_SUB_ROWS)

    def score(n):
        t, m = divmod(n, subs_per_tile)
        k = k_ref[0, 0, pl.ds(starts[t], nk), :]
        return lax.dot_general(q_ref[0, 0, sub(n), :], k, (((1,), (1,)), ((), ())),
                               preferred_element_type=F32) + bias_ref[classes[t], 0, sub(m), :]

    def softmax(s):
        e = jnp.exp(s - jnp.max(s, axis=-1, keepdims=True))
        return e.astype(BF16), jnp.sum(e, axis=-1, keepdims=True)

    def finish(n, prob):
        e, den = prob
        v = v_ref[0, 0, pl.ds(starts[n // subs_per_tile], nk), :]
        o = jnp.dot(e, v, preferred_element_type=F32) / den
        o_ref[0, 0, sub(n), :] = _rms(o, g_ref[0]).astype(BF16)

    _skewed(NA_STEP_TILES * subs_per_tile, NA_LAG, score, softmax, finish)


def _na_bias_table(rpb, rows):
    n_tiles = rows // NA_QROWS

    def row_pairs(tile):
        ks = int(np.clip(tile * NA_QROWS - NA_ROWS // 2, 0, rows - NA_KROWS))
        r = tile * NA_QROWS + np.arange(NA_QROWS)
        rs = np.clip(r - NA_ROWS // 2, 0, rows - NA_ROWS)
        kr = ks + np.arange(NA_KROWS)
        ok = (kr[None, :] >= rs[:, None]) & (kr[None, :] < rs[:, None] + NA_ROWS)
        assert (ok.sum(axis=1) == NA_ROWS).all()
        return ok, np.clip(kr[None, :] - r[:, None] + NA_ROWS - 1, 0, 2 * NA_ROWS - 2)

    row_ok, dr_idx = (np.stack(a) for a in zip(*(row_pairs(tile) for tile in (0, 1, n_tiles - 1))))
    for tile in range(1, n_tiles - 1):
        ok, dr = row_pairs(tile)
        assert (ok == row_ok[1]).all() and (np.where(ok, dr, 0) == np.where(ok, dr_idx[1], 0)).all()
    c = np.arange(GRID_W)
    qs = np.clip(c - NA_COLS // 2, 0, GRID_W - NA_COLS)
    col_ok = (c[None, :] >= qs[:, None]) & (c[None, :] < qs[:, None] + NA_COLS)
    n_dr, n_dc = 2 * NA_ROWS - 1, 2 * NA_COLS - 1
    period = 2 * GRID_W
    v = jnp.concatenate([rpb[..., NA_COLS - 1:], jnp.zeros((N_HEADS_NA, n_dr, period - n_dc), F32),
                         rpb[..., :NA_COLS - 1]], axis=-1).astype(F32)
    toe = jnp.tile(v, (1, 1, GRID_W))[..., :GRID_W * (period - 1)]
    toe = toe.reshape(N_HEADS_NA, n_dr, GRID_W, period - 1)[..., :GRID_W]
    slabs = jnp.where(col_ok[None, None], toe, NEG)
    slabs = jnp.concatenate([slabs, jnp.full((N_HEADS_NA, 1, GRID_W, GRID_W), NEG, F32)], axis=1)
    slab_idx = np.where(row_ok, dr_idx, n_dr)
    return pl.pallas_call(
        functools.partial(_na_bias_kernel, slab_idx=slab_idx),
        name="na_bias",
        grid=(N_HEADS_NA,),
        in_specs=[pl.BlockSpec((1, n_dr + 1, GRID_W, GRID_W), lambda h: (h, 0, 0, 0))],
        out_specs=pl.BlockSpec((3, 1, NA_QROWS * GRID_W, NA_KROWS * GRID_W), lambda h: (0, h, 0, 0)),
        out_shape=jax.ShapeDtypeStruct((3, N_HEADS_NA, NA_QROWS * GRID_W, NA_KROWS * GRID_W), F32),
        compiler_params=_params("parallel"),
    )(slabs)


def _na_bias_kernel(slabs_ref, o_ref, *, slab_idx):
    n_cls, n_q, n_k = slab_idx.shape
    for cls in range(n_cls):
        for rq in range(n_q):
            row = jnp.concatenate([slabs_ref[0, int(slab_idx[cls, rq, rk])] for rk in range(n_k)], axis=-1)
            o_ref[cls, 0, rq * GRID_W:(rq + 1) * GRID_W, :] = row


def _na(qkv, bias, head_g, batch, seq):
    rows = seq // GRID_W
    n_tiles = rows // NA_QROWS
    assert n_tiles % NA_STEP_TILES == 0
    tq = NA_QROWS * GRID_W

    return pl.pallas_call(
        functools.partial(_na_kernel, rows=rows),
        name="na",
        grid=(batch, N_HEADS_NA, n_tiles // NA_STEP_TILES),
        in_specs=[
            pl.BlockSpec((1, 1, NA_STEP_TILES * tq, HEAD_DIM), lambda b, h, i: (b, h, i, 0)),
            pl.BlockSpec((1, 1, seq, HEAD_DIM), lambda b, h, i: (b, N_HEADS_NA + h, 0, 0)),
            pl.BlockSpec((1, 1, seq, HEAD_DIM), lambda b, h, i: (b, 2 * N_HEADS_NA + h, 0, 0)),
            pl.BlockSpec((3, 1, tq, NA_KROWS * GRID_W), lambda b, h, i: (0, h, 0, 0)),
            pl.BlockSpec((1, 1, HEAD_DIM), lambda b, h, i: (h, 0, 0)),
        ],
        out_specs=pl.BlockSpec((1, 1, NA_STEP_TILES * tq, HEAD_DIM), lambda b, h, i: (b, h, i, 0)),
        out_shape=jax.ShapeDtypeStruct((batch, N_HEADS_NA, seq, HEAD_DIM), BF16),
        compiler_params=_params("parallel", "parallel", "arbitrary"),
    )(qkv, qkv, qkv, bias, head_g)


_DIL_WIN = DIL_TQ + 2 * DIL_HALF
_BRANCH_FOLD = tuple(max(dil, FOLD) for (_, dil) in DIL_PAIRS)


def _dil_branch(q_ref, k_ref, v_ref, mask_ref, os_ref, ls_ref, slot, part, *, dil, sub, length):
    fold = dil * sub
    qn, kn = DIL_TQ // sub, _DIL_WIN // sub
    n_i = length // qn
    n_p = n_i // DIL_PARTS
    row0 = part * (length // DIL_PARTS)

    def gather(ref, p, start, size):
        parts = [ref[0, 0, p + dil * c, pl.ds(start, size), :] for c in range(sub)]
        return parts[0] if sub == 1 else jnp.concatenate(parts, axis=0)

    def group(g, carry):
        tiles = []
        for n in range(DIL_GROUP):
            t = g * DIL_GROUP + n
            p = t // n_p
            i = part * n_p + t % n_p
            q0 = pl.multiple_of(i * qn, qn)
            k0 = pl.multiple_of(jnp.clip(q0 - DIL_HALF // sub, 0, length - kn), DIL_HALF // sub)
            edge = jnp.where(i == 0, 0, jnp.where(i == n_i - 1, 2, 1))
            tiles.append((p, q0, k0, edge))

        def score(n):
            p, q0, k0, edge = tiles[n]
            return lax.dot_general(gather(q_ref, p, q0, qn), gather(k_ref, p, k0, kn), (((1,), (1,)), ((), ())),
                                   preferred_element_type=F32) + mask_ref[edge]

        def softmax(s):
            m = jnp.max(s, axis=-1, keepdims=True)
            e = jnp.exp2(s - m)
            return m, e.astype(BF16), jnp.sum(e, axis=-1, keepdims=True)

        def finish(n, prob):
            p, q0, k0, edge = tiles[n]
            m, e, den = prob
            o = jnp.dot(e, gather(v_ref, p, k0, kn), preferred_element_type=F32) / den
            lse = jnp.broadcast_to(m + jnp.log2(den), (DIL_TQ, HEAD_DIM))
            for c in range(sub):
                rows = pl.ds(fold * (q0 - row0) + p + dil * c, qn, stride=fold)
                os_ref[slot, rows, :] = o[c * qn:(c + 1) * qn]
                ls_ref[slot, rows, :] = lse[c * qn:(c + 1) * qn]

        _skewed(DIL_GROUP, DIL_LAG, score, softmax, finish)
        return carry

    lax.fori_loop(0, dil * n_p // DIL_GROUP, group, 0)


def _dilated_kernel(qa_ref, ka_ref, va_ref, qb_ref, kb_ref, vb_ref, m0_ref, m1_ref, m2_ref, g_ref, o_ref,
                    os_ref, ls_ref, *, seq):
    part = pl.program_id(2)
    stored = {FOLD: (qa_ref, ka_ref, va_ref), FOLD ** 2: (qb_ref, kb_ref, vb_ref)}
    for slot, ((_, dil), fold, mask_ref) in enumerate(zip(DIL_PAIRS, _BRANCH_FOLD, (m0_ref, m1_ref, m2_ref))):
        _dil_branch(*stored[fold], mask_ref, os_ref, ls_ref, slot, part, dil=dil, sub=fold // dil,
                    length=seq // fold)

    def merge(c, carry):
        rows = pl.ds(pl.multiple_of(c * DIL_MERGE_ROWS, DIL_MERGE_ROWS), DIL_MERGE_ROWS)
        lses = [ls_ref[b, rows, :] for b in range(len(DIL_PAIRS))]
        mx = functools.reduce(jnp.maximum, lses)
        ws = [jnp.exp2(l - mx) for l in lses]
        num = functools.reduce(jnp.add, [w * os_ref[b, rows, :] for b, w in enumerate(ws)])
        o = num / functools.reduce(jnp.add, ws)
        o_ref[0, 0, rows, :] = _rms(o, g_ref[0]).astype(BF16)
        return carry

    lax.fori_loop(0, seq // DIL_PARTS // DIL_MERGE_ROWS, merge, 0)


def _dil_mask_table(sub, length):
    qn, kn = DIL_TQ // sub, _DIL_WIN // sub
    n_i = length // qn
    c = np.arange(sub)[:, None]
    out = []
    for i in (0, 1, n_i - 1):
        k0 = int(np.clip(i * qn - DIL_HALF // sub, 0, length - kn))
        qpos = ((i * qn + np.arange(qn))[None, :] * sub + c).reshape(-1)
        kpos = ((k0 + np.arange(kn))[None, :] * sub + c).reshape(-1)
        out.append(np.where(np.abs(kpos[None, :] - qpos[:, None]) <= DIL_HALF, 0.0, NEG))
    return jnp.asarray(np.stack(out), F32)


def _dilated(fa, fb, head_g, batch, seq):
    masks = []
    for (window, dil), fold in zip(DIL_PAIRS, _BRANCH_FOLD):
        sub, length = fold // dil, seq // fold
        assert window // (2 * dil) == DIL_HALF and fold % dil == 0 and DIL_TQ % sub == 0
        n_p = length // (DIL_TQ // sub) // DIL_PARTS
        assert n_p * DIL_PARTS * (DIL_TQ // sub) == length and (dil * n_p) % DIL_GROUP == 0 and n_p >= 1
        masks.append(_dil_mask_table(sub, length))

    def stored(f, first):
        return pl.BlockSpec((1, 1, f, seq // f, HEAD_DIM), lambda b, h, part: (b, first + h, 0, 0, 0))

    part_rows = seq // DIL_PARTS
    return pl.pallas_call(
        functools.partial(_dilated_kernel, seq=seq),
        name="dilated",
        grid=(batch, N_HEADS_DIL, DIL_PARTS),
        in_specs=[stored(f, first) for f in (FOLD, FOLD ** 2) for first in (0, N_HEADS_DIL, 2 * N_HEADS_DIL)]
        + [pl.BlockSpec((3, DIL_TQ, _DIL_WIN), lambda b, h, part: (0, 0, 0))] * len(DIL_PAIRS)
        + [pl.BlockSpec((1, 1, HEAD_DIM), lambda b, h, part: (N_HEADS_NA + h, 0, 0))],
        out_specs=pl.BlockSpec((1, 1, part_rows, HEAD_DIM), lambda b, h, part: (b, h, part, 0)),
        out_shape=jax.ShapeDtypeStruct((batch, N_HEADS_DIL, seq, HEAD_DIM), BF16),
        scratch_shapes=[pltpu.VMEM((len(DIL_PAIRS), part_rows, HEAD_DIM), F32)] * 2,
        compiler_params=_params("parallel", "parallel", "arbitrary"),
    )(fa, fa, fa, fb, fb, fb, *masks, head_g)


def _cast_once(w_ref, w16_ref):
    @pl.when(pl.program_id(0) == 0)
    def _():
        w16_ref[...] = w_ref[...].astype(BF16)


def _resident(shape):
    return pl.BlockSpec(shape, lambda i: (0,) * len(shape), pipeline_mode=pl.Buffered(1))


def _mix_out_kernel(ona_ref, odil_ref, wo_ref, h_ref, postg_ref, o_ref, wo16_ref):
    _cast_once(wo_ref, wo16_ref)
    heads = [ona_ref[0, h] for h in range(N_HEADS_NA)] + [odil_ref[0, h] for h in range(N_HEADS_DIL)]
    m = jnp.dot(jnp.concatenate(heads, axis=-1), wo16_ref[...], preferred_element_type=F32)
    o_ref[...] = h_ref[...] + _rms(m, postg_ref[...])


def _mix_out(o_na, o_dil, w_o, h, post_g, seq):
    t = h.shape[0]
    tiles_per_seq = seq // MIX_TM
    heads = lambda n: pl.BlockSpec((1, n, MIX_TM, HEAD_DIM), lambda i: (i // tiles_per_seq, 0, i % tiles_per_seq, 0))
    row = pl.BlockSpec((MIX_TM, D_MODEL), lambda i: (i, 0))
    const = lambda shape: pl.BlockSpec(shape, lambda i: (0, 0))
    return pl.pallas_call(
        _mix_out_kernel,
        name="mix_out",
        grid=(t // MIX_TM,),
        in_specs=[heads(N_HEADS_NA), heads(N_HEADS_DIL), _resident((D_MODEL, D_MODEL)), row, const((1, D_MODEL))],
        out_specs=row,
        out_shape=jax.ShapeDtypeStruct((t, D_MODEL), F32),
        scratch_shapes=[pltpu.VMEM((D_MODEL, D_MODEL), BF16)],
        compiler_params=_params("arbitrary"),
    )(o_na, o_dil, w_o, h, post_g)


def _ple_kernel(h_ref, p_ref, preg_ref, wg_ref, wp_ref, postg_ref, o_ref, wg16_ref, wp16_ref):
    _cast_once(wg_ref, wg16_ref)
    _cast_once(wp_ref, wp16_ref)
    h = h_ref[...]
    u = _rms(h, preg_ref[...]).astype(BF16)
    gate = jax.nn.sigmoid(jnp.dot(u, wg16_ref[...], preferred_element_type=F32))
    emb = jnp.dot(p_ref[...].astype(BF16), wp16_ref[...], preferred_element_type=F32)
    o_ref[...] = h + _rms(gate * emb, postg_ref[...])


def _ple(h, p, pre_g, w_gate, w_proj, post_g):
    t = h.shape[0]
    row = lambda width: pl.BlockSpec((PLE_TM, width), lambda i: (i, 0))
    const = lambda shape: pl.BlockSpec(shape, lambda i: (0, 0))
    return pl.pallas_call(
        _ple_kernel,
        name="ple",
        grid=(t // PLE_TM,),
        in_specs=[row(D_MODEL), row(PLE_DIM), const((1, D_MODEL)), _resident((D_MODEL, D_MODEL)),
                  _resident((PLE_DIM, D_MODEL)), const((1, D_MODEL))],
        out_specs=row(D_MODEL),
        out_shape=jax.ShapeDtypeStruct((t, D_MODEL), F32),
        scratch_shapes=[pltpu.VMEM((D_MODEL, D_MODEL), BF16), pltpu.VMEM((PLE_DIM, D_MODEL), BF16)],
        compiler_params=_params("arbitrary"),
    )(h, p, pre_g, w_gate, w_proj, post_g)


def kernel(x, p, ffn1_pre_g, ffn1_w_gate, ffn1_w_up, ffn1_w_down, ffn1_post_g, mix_pre_g, w_qkv, na_rpb, out_g, w_o, mix_post_g, ffn2_pre_g, ffn2_w_gate, ffn2_w_up, ffn2_w_down, ffn2_post_g, ple_pre_g, w_ple_gate, w_ple_proj, ple_post_g):
    batch, seq, d_model = x.shape
    depth = p.shape[0]
    assert d_model == D_MODEL and seq % (GRID_W * NA_QROWS) == 0 and seq // GRID_W >= NA_KROWS
    tokens = batch * seq
    rows = seq // GRID_W
    cos_full, sin_signed = _rope_tables(seq)
    gain = lambda g: g.reshape(1, D_MODEL)

    h = x.reshape(tokens, D_MODEL)
    for i in range(depth):
        h, (w_qkv16,) = _ffn(h, gain(ffn1_pre_g[i]), ffn1_w_gate[i], ffn1_w_up[i], ffn1_w_down[i],
                             gain(ffn1_post_g[i]), side=(w_qkv[i],))
        qkv_na, u_mix = _qkv_na(h, gain(mix_pre_g[i]), w_qkv16, batch, seq)
        fa, fb = _qkv_dil(u_mix, w_qkv16, cos_full, sin_signed, batch, seq)
        head_g = out_g[i].reshape(N_HEADS, 1, HEAD_DIM)
        o_na = _na(qkv_na, _na_bias_table(na_rpb[i], rows), head_g, batch, seq)
        o_dil = _dilated(fa, fb, head_g, batch, seq)
        h = _mix_out(o_na, o_dil, w_o[i], h, gain(mix_post_g[i]), seq)
        h, _ = _ffn(h, gain(ffn2_pre_g[i]), ffn2_w_gate[i], ffn2_w_up[i], ffn2_w_down[i], gain(ffn2_post_g[i]))
        h = _ple(h, p[i].reshape(tokens, PLE_DIM), gain(ple_pre_g[i]), w_ple_gate[i], w_ple_proj[i],
                 gain(ple_post_g[i]))
    return h.reshape(batch, seq, D_MODEL)
```

```python
import functools
import math

import jax
import jax.numpy as jnp
import numpy as np
from jax import lax
from jax.experimental import pallas as pl
from jax.experimental.pallas import tpu as pltpu

D_MODEL = 2048
D_FF = 5632
HEAD_DIM = 128
N_HEADS = 16
N_HEADS_NA = 4
N_HEADS_DIL = 12
W_NA = N_HEADS_NA * HEAD_DIM
W_DIL = N_HEADS_DIL * HEAD_DIM
GRID_W = 64
NA_ROWS = 8
NA_COLS = 16
DIL_PAIRS = ((128, 1), (512, 4), (2048, 16))
PLE_DIM = 256
ROPE_THETA = 10000.0
EPS = 1e-6
NEG = -1e30
SCALE = HEAD_DIM ** -0.5

F32 = jnp.float32
BF16 = jnp.bfloat16

VMEM_LIMIT_BYTES = 56 * 1024 * 1024
FFN_VMEM_LIMIT_BYTES = 62 * 1024 * 1024

FFN_TM = 1024
FFN_TF = 256
SIDE_ROWS = 16
QKV_TM = 1024
QKV_TN = 768
QKV_NA_TN = 1536
FOLD = 4
NA_QROWS = 8
NA_KROWS = 16
NA_STEP_TILES = 8
NA_SUB_ROWS = 128
NA_LAG = 1
DIL_TQ = 128
DIL_GROUP = 32
DIL_LAG = 2
DIL_HALF = 64
DIL_PARTS = 2
DIL_MERGE_ROWS = 1024
MIX_TM = 512
PLE_TM = 512


def _rms(x, g):
    return x * lax.rsqrt(jnp.mean(x * x, axis=-1, keepdims=True) + EPS) * g


def _params(*sem, vmem_limit_bytes=VMEM_LIMIT_BYTES):
    return pltpu.CompilerParams(dimension_semantics=sem, vmem_limit_bytes=vmem_limit_bytes)


def _normed_input(first, x_ref, g_ref, u_ref):
    if not first:
        return u_ref[...]
    u = _rms(x_ref[...], g_ref[...]).astype(BF16)
    u_ref[...] = u
    return u


def _first_or_later(step):
    j = pl.program_id(1)
    pl.when(j == 0)(functools.partial(step, True))
    pl.when(j > 0)(functools.partial(step, False))


def _skewed(n, lag, score, softmax, finish):
    scores, probs = {}, {}
    for t in range(n + 2 * lag):
        if t < n:
            scores[t] = score(t)
        if 0 <= t - lag < n:
            probs[t - lag] = softmax(scores.pop(t - lag))
        if 0 <= t - 2 * lag < n:
            finish(t - 2 * lag, probs.pop(t - 2 * lag))


def _ffn_kernel(x_ref, pre_g_ref, wg_ref, wu_ref, wd_ref, post_g_ref, *rest, side_blocks):
    n_side = len(side_blocks)
    side_in, o_ref, side_out, u_ref = rest[:n_side], rest[n_side], rest[n_side + 1:-1], rest[-1]
    j = pl.program_id(1)
    last = pl.num_programs(1) - 1
    flat_step = pl.program_id(0) * pl.num_programs(1) + j
    for w_ref, w16_ref, n_blocks in zip(side_in, side_out, side_blocks):
        @pl.when(flat_step < n_blocks)
        def _(w_ref=w_ref, w16_ref=w16_ref):
            w16_ref[...] = w_ref[...].astype(BF16)

    def step(first, final):
        u = _normed_input(first, x_ref, pre_g_ref, u_ref)
        g = jnp.dot(u, wg_ref[...].astype(BF16), preferred_element_type=F32)
        v = jnp.dot(u, wu_ref[...].astype(BF16), preferred_element_type=F32)
        mid = (g * jax.nn.sigmoid(g) * v).astype(BF16)
        acc = jnp.dot(mid, wd_ref[...].astype(BF16), preferred_element_type=F32)
        if not first:
            acc = o_ref[...] + acc
        o_ref[...] = x_ref[...] + 0.5 * _rms(acc, post_g_ref[...]) if final else acc

    pl.when(j == 0)(functools.partial(step, True, False))
    pl.when(jnp.logical_and(j > 0, j < last))(functools.partial(step, False, False))
    pl.when(j == last)(functools.partial(step, False, True))


def _ffn(x, pre_g, w_gate, w_up, w_down, post_g, side=()):
    t = x.shape[0]
    n_j = D_FF // FFN_TF
    assert n_j >= 2
    side_blocks = tuple(w.shape[0] // SIDE_ROWS for w in side)
    assert all(w.shape[0] % SIDE_ROWS == 0 for w in side) and all(n <= t // FFN_TM * n_j for n in side_blocks)

    def side_spec(w, n_blocks):
        return pl.BlockSpec((SIDE_ROWS, w.shape[1]), lambda i, j: (jnp.minimum(i * n_j + j, n_blocks - 1), 0))

    side_specs = [side_spec(w, n) for w, n in zip(side, side_blocks)]
    out = pl.pallas_call(
        functools.partial(_ffn_kernel, side_blocks=side_blocks),
        name="ffn",
        grid=(t // FFN_TM, n_j),
        in_specs=[
            pl.BlockSpec((FFN_TM, D_MODEL), lambda i, j: (i, 0)),
            pl.BlockSpec((1, D_MODEL), lambda i, j: (0, 0)),
            pl.BlockSpec((D_MODEL, FFN_TF), lambda i, j: (0, j)),
            pl.BlockSpec((D_MODEL, FFN_TF), lambda i, j: (0, j)),
            pl.BlockSpec((FFN_TF, D_MODEL), lambda i, j: (j, 0)),
            pl.BlockSpec((1, D_MODEL), lambda i, j: (0, 0)),
        ] + side_specs,
        out_specs=[pl.BlockSpec((FFN_TM, D_MODEL), lambda i, j: (i, 0))] + side_specs,
        out_shape=[jax.ShapeDtypeStruct((t, D_MODEL), F32)] + [jax.ShapeDtypeStruct(w.shape, BF16) for w in side],
        scratch_shapes=[pltpu.VMEM((FFN_TM, D_MODEL), BF16)],
        compiler_params=_params("arbitrary", "arbitrary", vmem_limit_bytes=FFN_VMEM_LIMIT_BYTES),
    )(x, pre_g, w_gate, w_up, w_down, post_g, *side)
    return out[0], out[1:]


Q_SCALE_NA = SCALE
Q_SCALE_DIL = SCALE * math.log2(math.e)
_DIL_COL_STEPS = 3 * W_DIL // QKV_TN
_DIL_STEPS_PER_PART = W_DIL // QKV_TN
_DOT_N = 2 * HEAD_DIM


def _qkv_na_kernel(x_ref, g_ref, w_ref, colscale_ref, o_ref, u_ref):
    def step(first):
        u = _normed_input(first, x_ref, g_ref, u_ref)
        y = jnp.dot(u, w_ref[...], preferred_element_type=F32) * colscale_ref[...]
        for h in range(QKV_NA_TN // HEAD_DIM):
            o_ref[0, h] = y[:, h * HEAD_DIM:(h + 1) * HEAD_DIM].astype(BF16)

    _first_or_later(step)


def _qkv_na(x, g, w, batch, seq):
    t = x.shape[0]
    tiles_per_seq = seq // QKV_TM
    colscale = jnp.concatenate([jnp.full((1, W_NA), Q_SCALE_NA, F32), jnp.ones((1, 2 * W_NA), F32)], axis=-1)
    return pl.pallas_call(
        _qkv_na_kernel,
        name="qkv_na",
        grid=(t // QKV_TM, 3 * W_NA // QKV_NA_TN),
        in_specs=[
            pl.BlockSpec((QKV_TM, D_MODEL), lambda i, j: (i, 0)),
            pl.BlockSpec((1, D_MODEL), lambda i, j: (0, 0)),
            pl.BlockSpec((D_MODEL, QKV_NA_TN), lambda i, j: (0, j)),
            pl.BlockSpec((1, QKV_NA_TN), lambda i, j: (0, j)),
        ],
        out_specs=[pl.BlockSpec((1, QKV_NA_TN // HEAD_DIM, QKV_TM, HEAD_DIM),
                                lambda i, j: (i // tiles_per_seq, j, i % tiles_per_seq, 0)),
                   pl.BlockSpec((QKV_TM, D_MODEL), lambda i, j: (i, 0))],
        out_shape=[jax.ShapeDtypeStruct((batch, 3 * N_HEADS_NA, seq, HEAD_DIM), BF16),
                   jax.ShapeDtypeStruct((t, D_MODEL), BF16)],
        compiler_params=_params("parallel", "arbitrary"),
    )(x, g, w, colscale)


def _qkv_dil_kernel(u_ref, w_ref, cos_ref, sin_ref, fa_ref, fb_ref, y_ref, y4_ref):
    def step(rotary, scale):
        u = u_ref[...]
        for pair in range(QKV_TN // _DOT_N):
            y2 = jnp.dot(u, w_ref[:, pair * _DOT_N:(pair + 1) * _DOT_N], preferred_element_type=F32)
            for half in range(_DOT_N // HEAD_DIM):
                h = pair * (_DOT_N // HEAD_DIM) + half
                y = y2[:, half * HEAD_DIM:(half + 1) * HEAD_DIM]
                if rotary:
                    y = y * cos_ref[...] + pltpu.roll(y, HEAD_DIM // 2, axis=1) * sin_ref[...]
                y_ref[h] = y if scale is None else y * scale
                for r in range(FOLD):
                    y4 = y_ref[h, pl.ds(r, QKV_TM // FOLD, stride=FOLD), :]
                    fa_ref[0, h, r] = y4.astype(BF16)
                    y4_ref[h, r] = y4
                    for c in range(FOLD):
                        fb_ref[0, h, r + FOLD * c] = (
                            y4_ref[h, r, pl.ds(c, QKV_TM // FOLD ** 2, stride=FOLD), :].astype(BF16))

    part = pl.program_id(1) // _DIL_STEPS_PER_PART
    pl.when(part == 0)(functools.partial(step, True, Q_SCALE_DIL))
    pl.when(part == 1)(functools.partial(step, True, None))
    pl.when(part == 2)(functools.partial(step, False, None))


def _qkv_dil(u, w, cos_full, sin_signed, batch, seq):
    t = u.shape[0]
    tm = QKV_TM
    tiles_per_seq = seq // tm
    first_col_step = 3 * W_NA // QKV_TN

    def fold_spec(f):
        return pl.BlockSpec((1, QKV_TN // HEAD_DIM, f, tm // f, HEAD_DIM),
                            lambda i, j: (i // tiles_per_seq, j, 0, i % tiles_per_seq, 0))

    return pl.pallas_call(
        _qkv_dil_kernel,
        name="qkv_dil",
        grid=(t // tm, _DIL_COL_STEPS),
        in_specs=[
            pl.BlockSpec((tm, D_MODEL), lambda i, j: (i, 0)),
            pl.BlockSpec((D_MODEL, QKV_TN), lambda i, j: (0, first_col_step + j)),
            pl.BlockSpec((tm, HEAD_DIM), lambda i, j: (i % tiles_per_seq, 0)),
            pl.BlockSpec((tm, HEAD_DIM), lambda i, j: (i % tiles_per_seq, 0)),
        ],
        out_specs=[fold_spec(f) for f in (FOLD, FOLD ** 2)],
        out_shape=[jax.ShapeDtypeStruct((batch, 3 * N_HEADS_DIL, f, seq // f, HEAD_DIM), BF16)
                   for f in (FOLD, FOLD ** 2)],
        scratch_shapes=[pltpu.VMEM((QKV_TN // HEAD_DIM, tm, HEAD_DIM), F32),
                        pltpu.VMEM((QKV_TN // HEAD_DIM, FOLD, tm // FOLD, HEAD_DIM), F32)],
        compiler_params=_params("parallel", "arbitrary"),
    )(u, w, cos_full, sin_signed)


def _rope_tables(seq):
    inv = jnp.float32(ROPE_THETA) ** (-jnp.arange(0, HEAD_DIM, 2, dtype=F32) / HEAD_DIM)
    ang = jnp.arange(seq, dtype=F32)[:, None] * inv[None, :]
    cos, sin = jnp.cos(ang), jnp.sin(ang)
    return jnp.concatenate([cos, cos], axis=-1), jnp.concatenate([-sin, sin], axis=-1)


def _na_tile_key_row_start(i, rows):
    return jnp.clip(i * NA_QROWS - NA_ROWS // 2, 0, rows - NA_KROWS)


def _na_kernel(q_ref, k_ref, v_ref, bias_ref, g_ref, o_ref, *, rows):
    i = pl.program_id(2)
    nk = NA_KROWS * GRID_W
    n_tiles = rows // NA_QROWS
    subs_per_tile = NA_QROWS * GRID_W // NA_SUB_ROWS
    tiles = [i * NA_STEP_TILES + t for t in range(NA_STEP_TILES)]
    starts = [pl.multiple_of(_na_tile_key_row_start(tile, rows) * GRID_W, GRID_W) for tile in tiles]
    classes = [jnp.where(tile == 0, 0, jnp.where(tile == n_tiles - 1, 2, 1)) for tile in tiles]
    sub = lambda n: slice(n * NA_SUB_ROWS, (n + 1) * NA_SUB_ROWS)

    def score(n):
        t, m = divmod(n, subs_per_tile)
        k = k_ref[0, 0, pl.ds(starts[t], nk), :]
        return lax.dot_general(q_ref[0, 0, sub(n), :], k, (((1,), (1,)), ((), ())),
                               preferred_element_type=F32) + bias_ref[classes[t], 0, sub(m), :]

    def softmax(s):
        e = jnp.exp(s - jnp.max(s, axis=-1, keepdims=True))
        return e.astype(BF16), jnp.sum(e, axis=-1, keepdims=True)

    def finish(n, prob):
        e, den = prob
        v = v_ref[0, 0, pl.ds(starts[n // subs_per_tile], nk), :]
        o = jnp.dot(e, v, preferred_element_type=F32) / den
        o_ref[0, 0, sub(n), :] = _rms(o, g_ref[0]).astype(BF16)

    _skewed(NA_STEP_TILES * subs_per_tile, NA_LAG, score, softmax, finish)


def _na_bias_table(rpb, rows):
    n_tiles = rows // NA_QROWS

    def row_pairs(tile):
        ks = int(np.clip(tile * NA_QROWS - NA_ROWS // 2, 0, rows - NA_KROWS))
        r = tile * NA_QROWS + np.arange(NA_QROWS)
        rs = np.clip(r - NA_ROWS // 2, 0, rows - NA_ROWS)
        kr = ks + np.arange(NA_KROWS)
        ok = (kr[None, :] >= rs[:, None]) & (kr[None, :] < rs[:, None] + NA_ROWS)
        assert (ok.sum(axis=1) == NA_ROWS).all()
        return ok, np.clip(kr[None, :] - r[:, None] + NA_ROWS - 1, 0, 2 * NA_ROWS - 2)

    row_ok, dr_idx = (np.stack(a) for a in zip(*(row_pairs(tile) for tile in (0, 1, n_tiles - 1))))
    for tile in range(1, n_tiles - 1):
        ok, dr = row_pairs(tile)
        assert (ok == row_ok[1]).all() and (np.where(ok, dr, 0) == np.where(ok, dr_idx[1], 0)).all()
    c = np.arange(GRID_W)
    qs = np.clip(c - NA_COLS // 2, 0, GRID_W - NA_COLS)
    col_ok = (c[None, :] >= qs[:, None]) & (c[None, :] < qs[:, None] + NA_COLS)
    n_dr, n_dc = 2 * NA_ROWS - 1, 2 * NA_COLS - 1
    period = 2 * GRID_W
    v = jnp.concatenate([rpb[..., NA_COLS - 1:], jnp.zeros((N_HEADS_NA, n_dr, period - n_dc), F32),
                         rpb[..., :NA_COLS - 1]], axis=-1).astype(F32)
    toe = jnp.tile(v, (1, 1, GRID_W))[..., :GRID_W * (period - 1)]
    toe = toe.reshape(N_HEADS_NA, n_dr, GRID_W, period - 1)[..., :GRID_W]
    slabs = jnp.where(col_ok[None, None], toe, NEG)
    slabs = jnp.concatenate([slabs, jnp.full((N_HEADS_NA, 1, GRID_W, GRID_W), NEG, F32)], axis=1)
    slab_idx = np.where(row_ok, dr_idx, n_dr)
    return pl.pallas_call(
        functools.partial(_na_bias_kernel, slab_idx=slab_idx),
        name="na_bias",
        grid=(N_HEADS_NA,),
        in_specs=[pl.BlockSpec((1, n_dr + 1, GRID_W, GRID_W), lambda h: (h, 0, 0, 0))],
        out_specs=pl.BlockSpec((3, 1, NA_QROWS * GRID_W, NA_KROWS * GRID_W), lambda h: (0, h, 0, 0)),
        out_shape=jax.ShapeDtypeStruct((3, N_HEADS_NA, NA_QROWS * GRID_W, NA_KROWS * GRID_W), F32),
        compiler_params=_params("parallel"),
    )(slabs)


def _na_bias_kernel(slabs_ref, o_ref, *, slab_idx):
    n_cls, n_q, n_k = slab_idx.shape
    for cls in range(n_cls):
        for rq in range(n_q):
            row = jnp.concatenate([slabs_ref[0, int(slab_idx[cls, rq, rk])] for rk in range(n_k)], axis=-1)
            o_ref[cls, 0, rq * GRID_W:(rq + 1) * GRID_W, :] = row


def _na(qkv, bias, head_g, batch, seq):
    rows = seq // GRID_W
    n_tiles = rows // NA_QROWS
    assert n_tiles % NA_STEP_TILES == 0
    tq = NA_QROWS * GRID_W

    return pl.pallas_call(
        functools.partial(_na_kernel, rows=rows),
        name="na",
        grid=(batch, N_HEADS_NA, n_tiles // NA_STEP_TILES),
        in_specs=[
            pl.BlockSpec((1, 1, NA_STEP_TILES * tq, HEAD_DIM), lambda b, h, i: (b, h, i, 0)),
            pl.BlockSpec((1, 1, seq, HEAD_DIM), lambda b, h, i: (b, N_HEADS_NA + h, 0, 0)),
            pl.BlockSpec((1, 1, seq, HEAD_DIM), lambda b, h, i: (b, 2 * N_HEADS_NA + h, 0, 0)),
            pl.BlockSpec((3, 1, tq, NA_KROWS * GRID_W), lambda b, h, i: (0, h, 0, 0)),
            pl.BlockSpec((1, 1, HEAD_DIM), lambda b, h, i: (h, 0, 0)),
        ],
        out_specs=pl.BlockSpec((1, 1, NA_STEP_TILES * tq, HEAD_DIM), lambda b, h, i: (b, h, i, 0)),
        out_shape=jax.ShapeDtypeStruct((batch, N_HEADS_NA, seq, HEAD_DIM), BF16),
        compiler_params=_params("parallel", "parallel", "arbitrary"),
    )(qkv, qkv, qkv, bias, head_g)


_DIL_WIN = DIL_TQ + 2 * DIL_HALF
_BRANCH_FOLD = tuple(max(dil, FOLD) for (_, dil) in DIL_PAIRS)


def _dil_branch(q_ref, k_ref, v_ref, mask_ref, os_ref, ls_ref, slot, part, *, dil, sub, length):
    fold = dil * sub
    qn, kn = DIL_TQ // sub, _DIL_WIN // sub
    n_i = length // qn
    n_p = n_i // DIL_PARTS
    row0 = part * (length // DIL_PARTS)

    def gather(ref, p, start, size):
        parts = [ref[0, 0, p + dil * c, pl.ds(start, size), :] for c in range(sub)]
        return parts[0] if sub == 1 else jnp.concatenate(parts, axis=0)

    def group(g, carry):
        tiles = []
        for n in range(DIL_GROUP):
            t = g * DIL_GROUP + n
            p = t // n_p
            i = part * n_p + t % n_p
            q0 = pl.multiple_of(i * qn, qn)
            k0 = pl.multiple_of(jnp.clip(q0 - DIL_HALF // sub, 0, length - kn), DIL_HALF // sub)
            edge = jnp.where(i == 0, 0, jnp.where(i == n_i - 1, 2, 1))
            tiles.append((p, q0, k0, edge))

        def score(n):
            p, q0, k0, edge = tiles[n]
            return lax.dot_general(gather(q_ref, p, q0, qn), gather(k_ref, p, k0, kn), (((1,), (1,)), ((), ())),
                                   preferred_element_type=F32) + mask_ref[edge]

        def softmax(s):
            m = jnp.max(s, axis=-1, keepdims=True)
            e = jnp.exp2(s - m)
            return m, e.astype(BF16), jnp.sum(e, axis=-1, keepdims=True)

        def finish(n, prob):
            p, q0, k0, edge = tiles[n]
            m, e, den = prob
            o = jnp.dot(e, gather(v_ref, p, k0, kn), preferred_element_type=F32) / den
            lse = jnp.broadcast_to(m + jnp.log2(den), (DIL_TQ, HEAD_DIM))
            for c in range(sub):
                rows = pl.ds(fold * (q0 - row0) + p + dil * c, qn, stride=fold)
                os_ref[slot, rows, :] = o[c * qn:(c + 1) * qn]
                ls_ref[slot, rows, :] = lse[c * qn:(c + 1) * qn]

        _skewed(DIL_GROUP, DIL_LAG, score, softmax, finish)
        return carry

    lax.fori_loop(0, dil * n_p // DIL_GROUP, group, 0)


def _dilated_kernel(qa_ref, ka_ref, va_ref, qb_ref, kb_ref, vb_ref, m0_ref, m1_ref, m2_ref, g_ref, o_ref,
                    os_ref, ls_ref, *, seq):
    part = pl.program_id(2)
    stored = {FOLD: (qa_ref, ka_ref, va_ref), FOLD ** 2: (qb_ref, kb_ref, vb_ref)}
    for slot, ((_, dil), fold, mask_ref) in enumerate(zip(DIL_PAIRS, _BRANCH_FOLD, (m0_ref, m1_ref, m2_ref))):
        _dil_branch(*stored[fold], mask_ref, os_ref, ls_ref, slot, part, dil=dil, sub=fold // dil,
                    length=seq // fold)

    def merge(c, carry):
        rows = pl.ds(pl.multiple_of(c * DIL_MERGE_ROWS, DIL_MERGE_ROWS), DIL_MERGE_ROWS)
        lses = [ls_ref[b, rows, :] for b in range(len(DIL_PAIRS))]
        mx = functools.reduce(jnp.maximum, lses)
        ws = [jnp.exp2(l - mx) for l in lses]
        num = functools.reduce(jnp.add, [w * os_ref[b, rows, :] for b, w in enumerate(ws)])
        o = num / functools.reduce(jnp.add, ws)
        o_ref[0, 0, rows, :] = _rms(o, g_ref[0]).astype(BF16)
        return carry

    lax.fori_loop(0, seq // DIL_PARTS // DIL_MERGE_ROWS, merge, 0)


def _dil_mask_table(sub, length):
    qn, kn = DIL_TQ // sub, _DIL_WIN // sub
    n_i = length // qn
    c = np.arange(sub)[:, None]
    out = []
    for i in (0, 1, n_i - 1):
        k0 = int(np.clip(i * qn - DIL_HALF // sub, 0, length - kn))
        qpos = ((i * qn + np.arange(qn))[None, :] * sub + c).reshape(-1)
        kpos = ((k0 + np.arange(kn))[None, :] * sub + c).reshape(-1)
        out.append(np.where(np.abs(kpos[None, :] - qpos[:, None]) <= DIL_HALF, 0.0, NEG))
    return jnp.asarray(np.stack(out), F32)


def _dilated(fa, fb, head_g, batch, seq):
    masks = []
    for (window, dil), fold in zip(DIL_PAIRS, _BRANCH_FOLD):
        sub, length = fold // dil, seq // fold
        assert window // (2 * dil) == DIL_HALF and fold % dil == 0 and DIL_TQ % sub == 0
        n_p = length // (DIL_TQ // sub) // DIL_PARTS
        assert n_p * DIL_PARTS * (DIL_TQ // sub) == length and (dil * n_p) % DIL_GROUP == 0 and n_p >= 1
        masks.append(_dil_mask_table(sub, length))

    def stored(f, first):
        return pl.BlockSpec((1, 1, f, seq // f, HEAD_DIM), lambda b, h, part: (b, first + h, 0, 0, 0))

    part_rows = seq // DIL_PARTS
    return pl.pallas_call(
        functools.partial(_dilated_kernel, seq=seq),
        name="dilated",
        grid=(batch, N_HEADS_DIL, DIL_PARTS),
        in_specs=[stored(f, first) for f in (FOLD, FOLD ** 2) for first in (0, N_HEADS_DIL, 2 * N_HEADS_DIL)]
        + [pl.BlockSpec((3, DIL_TQ, _DIL_WIN), lambda b, h, part: (0, 0, 0))] * len(DIL_PAIRS)
        + [pl.BlockSpec((1, 1, HEAD_DIM), lambda b, h, part: (N_HEADS_NA + h, 0, 0))],
        out_specs=pl.BlockSpec((1, 1, part_rows, HEAD_DIM), lambda b, h, part: (b, h, part, 0)),
        out_shape=jax.ShapeDtypeStruct((batch, N_HEADS_DIL, seq, HEAD_DIM), BF16),
        scratch_shapes=[pltpu.VMEM((len(DIL_PAIRS), part_rows, HEAD_DIM), F32)] * 2,
        compiler_params=_params("parallel", "parallel", "arbitrary"),
    )(fa, fa, fa, fb, fb, fb, *masks, head_g)


def _cast_once(w_ref, w16_ref):
    @pl.when(pl.program_id(0) == 0)
    def _():
        w16_ref[...] = w_ref[...].astype(BF16)


def _resident(shape):
    return pl.BlockSpec(shape, lambda i: (0,) * len(shape), pipeline_mode=pl.Buffered(1))


def _mix_out_kernel(ona_ref, odil_ref, wo_ref, h_ref, postg_ref, o_ref, wo16_ref):
    _cast_once(wo_ref, wo16_ref)
    heads = [ona_ref[0, h] for h in range(N_HEADS_NA)] + [odil_ref[0, h] for h in range(N_HEADS_DIL)]
    m = jnp.dot(jnp.concatenate(heads, axis=-1), wo16_ref[...], preferred_element_type=F32)
    o_ref[...] = h_ref[...] + _rms(m, postg_ref[...])


def _mix_out(o_na, o_dil, w_o, h, post_g, seq):
    t = h.shape[0]
    tiles_per_seq = seq // MIX_TM
    heads = lambda n: pl.BlockSpec((1, n, MIX_TM, HEAD_DIM), lambda i: (i // tiles_per_seq, 0, i % tiles_per_seq, 0))
    row = pl.BlockSpec((MIX_TM, D_MODEL), lambda i: (i, 0))
    const = lambda shape: pl.BlockSpec(shape, lambda i: (0, 0))
    return pl.pallas_call(
        _mix_out_kernel,
        name="mix_out",
        grid=(t // MIX_TM,),
        in_specs=[heads(N_HEADS_NA), heads(N_HEADS_DIL), _resident((D_MODEL, D_MODEL)), row, const((1, D_MODEL))],
        out_specs=row,
        out_shape=jax.ShapeDtypeStruct((t, D_MODEL), F32),
        scratch_shapes=[pltpu.VMEM((D_MODEL, D_MODEL), BF16)],
        compiler_params=_params("arbitrary"),
    )(o_na, o_dil, w_o, h, post_g)


def _ple_kernel(h_ref, p_ref, preg_ref, wg_ref, wp_ref, postg_ref, o_ref, wg16_ref, wp16_ref):
    _cast_once(wg_ref, wg16_ref)
    _cast_once(wp_ref, wp16_ref)
    h = h_ref[...]
    u = _rms(h, preg_ref[...]).astype(BF16)
    gate = jax.nn.sigmoid(jnp.dot(u, wg16_ref[...], preferred_element_type=F32))
    emb = jnp.dot(p_ref[...].astype(BF16), wp16_ref[...], preferred_element_type=F32)
    o_ref[...] = h + _rms(gate * emb, postg_ref[...])


def _ple(h, p, pre_g, w_gate, w_proj, post_g):
    t = h.shape[0]
    row = lambda width: pl.BlockSpec((PLE_TM, width), lambda i: (i, 0))
    const = lambda shape: pl.BlockSpec(shape, lambda i: (0, 0))
    return pl.pallas_call(
        _ple_kernel,
        name="ple",
        grid=(t // PLE_TM,),
        in_specs=[row(D_MODEL), row(PLE_DIM), const((1, D_MODEL)), _resident((D_MODEL, D_MODEL)),
                  _resident((PLE_DIM, D_MODEL)), const((1, D_MODEL))],
        out_specs=row(D_MODEL),
        out_shape=jax.ShapeDtypeStruct((t, D_MODEL), F32),
        scratch_shapes=[pltpu.VMEM((D_MODEL, D_MODEL), BF16), pltpu.VMEM((PLE_DIM, D_MODEL), BF16)],
        compiler_params=_params("arbitrary"),
    )(h, p, pre_g, w_gate, w_proj, post_g)


def kernel(x, p, ffn1_pre_g, ffn1_w_gate, ffn1_w_up, ffn1_w_down, ffn1_post_g, mix_pre_g, w_qkv, na_rpb, out_g, w_o, mix_post_g, ffn2_pre_g, ffn2_w_gate, ffn2_w_up, ffn2_w_down, ffn2_post_g, ple_pre_g, w_ple_gate, w_ple_proj, ple_post_g):
    batch, seq, d_model = x.shape
    depth = p.shape[0]
    assert d_model == D_MODEL and seq % (GRID_W * NA_QROWS) == 0 and seq // GRID_W >= NA_KROWS
    tokens = batch * seq
    rows = seq // GRID_W
    cos_full, sin_signed = _rope_tables(seq)
    gain = lambda g: g.reshape(1, D_MODEL)

    h = x.reshape(tokens, D_MODEL)
    for i in range(depth):
        h, (w_qkv16,) = _ffn(h, gain(ffn1_pre_g[i]), ffn1_w_gate[i], ffn1_w_up[i], ffn1_w_down[i],
                             gain(ffn1_post_g[i]), side=(w_qkv[i],))
        qkv_na, u_mix = _qkv_na(h, gain(mix_pre_g[i]), w_qkv16, batch, seq)
        fa, fb = _qkv_dil(u_mix, w_qkv16, cos_full, sin_signed, batch, seq)
        head_g = out_g[i].reshape(N_HEADS, 1, HEAD_DIM)
        o_na = _na(qkv_na, _na_bias_table(na_rpb[i], rows), head_g, batch, seq)
        o_dil = _dilated(fa, fb, head_g, batch, seq)
        h = _mix_out(o_na, o_dil, w_o[i], h, gain(mix_post_g[i]), seq)
        h, _ = _ffn(h, gain(ffn2_pre_g[i]), ffn2_w_gate[i], ffn2_w_up[i], ffn2_w_down[i], gain(ffn2_post_g[i]))
        h = _ple(h, p[i].reshape(tokens, PLE_DIM), gain(ple_pre_g[i]), w_ple_gate[i], w_ple_proj[i],
                 gain(ple_post_g[i]))
    return h.reshape(batch, seq, D_MODEL)
```

```python
import functools
import math

import jax
import jax.numpy as jnp
import numpy as np
from jax import lax
from jax.experimental import pallas as pl
from jax.experimental.pallas import tpu as pltpu

D_MODEL = 2048
D_FF = 5632
HEAD_DIM = 128
N_HEADS = 16
N_HEADS_NA = 4
N_HEADS_DIL = 12
W_NA = N_HEADS_NA * HEAD_DIM
W_DIL = N_HEADS_DIL * HEAD_DIM
GRID_W = 64
NA_ROWS = 8
NA_COLS = 16
DIL_PAIRS = ((128, 1), (512, 4), (2048, 16))
PLE_DIM = 256
ROPE_THETA = 10000.0
EPS = 1e-6
NEG = -1e30
SCALE = HEAD_DIM ** -0.5

F32 = jnp.float32
BF16 = jnp.bfloat16

VMEM_LIMIT_BYTES = 56 * 1024 * 1024
FFN_VMEM_LIMIT_BYTES = 62 * 1024 * 1024

FFN_TM = 1024
FFN_TF = 256
SIDE_ROWS = 16
QKV_TM = 1024
QKV_TN = 1536
QKV_NA_TN = 1536
FOLD = 4
NA_QROWS = 8
NA_KROWS = 16
NA_STEP_TILES = 8
NA_SUB_ROWS = 128
NA_LAG = 1
DIL_TQ = 128
DIL_GROUP = 32
DIL_LAG = 2
DIL_HALF = 64
DIL_PARTS = 2
DIL_MERGE_ROWS = 1024
MIX_TM = 512
PLE_TM = 512


def _rms(x, g):
    return x * lax.rsqrt(jnp.mean(x * x, axis=-1, keepdims=True) + EPS) * g


def _params(*sem, vmem_limit_bytes=VMEM_LIMIT_BYTES):
    return pltpu.CompilerParams(dimension_semantics=sem, vmem_limit_bytes=vmem_limit_bytes)


def _normed_input(first, x_ref, g_ref, u_ref):
    if not first:
        return u_ref[...]
    u = _rms(x_ref[...], g_ref[...]).astype(BF16)
    u_ref[...] = u
    return u


def _first_or_later(step):
    j = pl.program_id(1)
    pl.when(j == 0)(functools.partial(step, True))
    pl.when(j > 0)(functools.partial(step, False))


def _skewed(n, lag, score, softmax, finish):
    scores, probs = {}, {}
    for t in range(n + 2 * lag):
        if t < n:
            scores[t] = score(t)
        if 0 <= t - lag < n:
            probs[t - lag] = softmax(scores.pop(t - lag))
        if 0 <= t - 2 * lag < n:
            finish(t - 2 * lag, probs.pop(t - 2 * lag))


def _ffn_kernel(x_ref, pre_g_ref, wg_ref, wu_ref, wd_ref, post_g_ref, *rest, side_blocks):
    n_side = len(side_blocks)
    side_in, o_ref, side_out, u_ref = rest[:n_side], rest[n_side], rest[n_side + 1:-1], rest[-1]
    j = pl.program_id(1)
    last = pl.num_programs(1) - 1
    flat_step = pl.program_id(0) * pl.num_programs(1) + j
    for w_ref, w16_ref, n_blocks in zip(side_in, side_out, side_blocks):
        @pl.when(flat_step < n_blocks)
        def _(w_ref=w_ref, w16_ref=w16_ref):
            w16_ref[...] = w_ref[...].astype(BF16)

    def step(first, final):
        u = _normed_input(first, x_ref, pre_g_ref, u_ref)
        g = jnp.dot(u, wg_ref[...].astype(BF16), preferred_element_type=F32)
        v = jnp.dot(u, wu_ref[...].astype(BF16), preferred_element_type=F32)
        mid = (g * jax.nn.sigmoid(g) * v).astype(BF16)
        acc = jnp.dot(mid, wd_ref[...].astype(BF16), preferred_element_type=F32)
        if not first:
            acc = o_ref[...] + acc
        o_ref[...] = x_ref[...] + 0.5 * _rms(acc, post_g_ref[...]) if final else acc

    pl.when(j == 0)(functools.partial(step, True, False))
    pl.when(jnp.logical_and(j > 0, j < last))(functools.partial(step, False, False))
    pl.when(j == last)(functools.partial(step, False, True))


def _ffn(x, pre_g, w_gate, w_up, w_down, post_g, side=()):
    t = x.shape[0]
    n_j = D_FF // FFN_TF
    assert n_j >= 2
    side_blocks = tuple(w.shape[0] // SIDE_ROWS for w in side)
    assert all(w.shape[0] % SIDE_ROWS == 0 for w in side) and all(n <= t // FFN_TM * n_j for n in side_blocks)

    def side_spec(w, n_blocks):
        return pl.BlockSpec((SIDE_ROWS, w.shape[1]), lambda i, j: (jnp.minimum(i * n_j + j, n_blocks - 1), 0))

    side_specs = [side_spec(w, n) for w, n in zip(side, side_blocks)]
    out = pl.pallas_call(
        functools.partial(_ffn_kernel, side_blocks=side_blocks),
        name="ffn",
        grid=(t // FFN_TM, n_j),
        in_specs=[
            pl.BlockSpec((FFN_TM, D_MODEL), lambda i, j: (i, 0)),
            pl.BlockSpec((1, D_MODEL), lambda i, j: (0, 0)),
            pl.BlockSpec((D_MODEL, FFN_TF), lambda i, j: (0, j)),
            pl.BlockSpec((D_MODEL, FFN_TF), lambda i, j: (0, j)),
            pl.BlockSpec((FFN_TF, D_MODEL), lambda i, j: (j, 0)),
            pl.BlockSpec((1, D_MODEL), lambda i, j: (0, 0)),
        ] + side_specs,
        out_specs=[pl.BlockSpec((FFN_TM, D_MODEL), lambda i, j: (i, 0))] + side_specs,
        out_shape=[jax.ShapeDtypeStruct((t, D_MODEL), F32)] + [jax.ShapeDtypeStruct(w.shape, BF16) for w in side],
        scratch_shapes=[pltpu.VMEM((FFN_TM, D_MODEL), BF16)],
        compiler_params=_params("arbitrary", "arbitrary", vmem_limit_bytes=FFN_VMEM_LIMIT_BYTES),
    )(x, pre_g, w_gate, w_up, w_down, post_g, *side)
    return out[0], out[1:]


Q_SCALE_NA = SCALE
Q_SCALE_DIL = SCALE * math.log2(math.e)
_DIL_COL_STEPS = 3 * W_DIL // QKV_TN
_DIL_STEPS_PER_PART = W_DIL // QKV_TN
_DOT_N = 2 * HEAD_DIM


def _qkv_na_kernel(x_ref, g_ref, w_ref, colscale_ref, o_ref, u_ref):
    def step(first):
        u = _normed_input(first, x_ref, g_ref, u_ref)
        y = jnp.dot(u, w_ref[...], preferred_element_type=F32) * colscale_ref[...]
        for h in range(QKV_NA_TN // HEAD_DIM):
            o_ref[0, h] = y[:, h * HEAD_DIM:(h + 1) * HEAD_DIM].astype(BF16)

    _first_or_later(step)


def _qkv_na(x, g, w, batch, seq):
    t = x.shape[0]
    tiles_per_seq = seq // QKV_TM
    colscale = jnp.concatenate([jnp.full((1, W_NA), Q_SCALE_NA, F32), jnp.ones((1, 2 * W_NA), F32)], axis=-1)
    return pl.pallas_call(
        _qkv_na_kernel,
        name="qkv_na",
        grid=(t // QKV_TM, 3 * W_NA // QKV_NA_TN),
        in_specs=[
            pl.BlockSpec((QKV_TM, D_MODEL), lambda i, j: (i, 0)),
            pl.BlockSpec((1, D_MODEL), lambda i, j: (0, 0)),
            pl.BlockSpec((D_MODEL, QKV_NA_TN), lambda i, j: (0, j)),
            pl.BlockSpec((1, QKV_NA_TN), lambda i, j: (0, j)),
        ],
        out_specs=[pl.BlockSpec((1, QKV_NA_TN // HEAD_DIM, QKV_TM, HEAD_DIM),
                                lambda i, j: (i // tiles_per_seq, j, i % tiles_per_seq, 0)),
                   pl.BlockSpec((QKV_TM, D_MODEL), lambda i, j: (i, 0))],
        out_shape=[jax.ShapeDtypeStruct((batch, 3 * N_HEADS_NA, seq, HEAD_DIM), BF16),
                   jax.ShapeDtypeStruct((t, D_MODEL), BF16)],
        compiler_params=_params("parallel", "arbitrary"),
    )(x, g, w, colscale)


def _qkv_dil_kernel(u_ref, w_ref, cos_ref, sin_ref, fa_ref, fb_ref, y_ref, y4_ref):
    def step(rotary, scale):
        u = u_ref[...]
        for pair in range(QKV_TN // _DOT_N):
            y2 = jnp.dot(u, w_ref[:, pair * _DOT_N:(pair + 1) * _DOT_N], preferred_element_type=F32)
            for half in range(_DOT_N // HEAD_DIM):
                h = pair * (_DOT_N // HEAD_DIM) + half
                y = y2[:, half * HEAD_DIM:(half + 1) * HEAD_DIM]
                if rotary:
                    y = y * cos_ref[...] + pltpu.roll(y, HEAD_DIM // 2, axis=1) * sin_ref[...]
                y_ref[h] = y if scale is None else y * scale
                for r in range(FOLD):
                    y4 = y_ref[h, pl.ds(r, QKV_TM // FOLD, stride=FOLD), :]
                    fa_ref[0, h, r] = y4.astype(BF16)
                    y4_ref[h, r] = y4
                    for c in range(FOLD):
                        fb_ref[0, h, r + FOLD * c] = (
                            y4_ref[h, r, pl.ds(c, QKV_TM // FOLD ** 2, stride=FOLD), :].astype(BF16))

    part = pl.program_id(1) // _DIL_STEPS_PER_PART
    pl.when(part == 0)(functools.partial(step, True, Q_SCALE_DIL))
    pl.when(part == 1)(functools.partial(step, True, None))
    pl.when(part == 2)(functools.partial(step, False, None))


def _qkv_dil(u, w, cos_full, sin_signed, batch, seq):
    t = u.shape[0]
    tm = QKV_TM
    tiles_per_seq = seq // tm
    first_col_step = 3 * W_NA // QKV_TN

    def fold_spec(f):
        return pl.BlockSpec((1, QKV_TN // HEAD_DIM, f, tm // f, HEAD_DIM),
                            lambda i, j: (i // tiles_per_seq, j, 0, i % tiles_per_seq, 0))

    return pl.pallas_call(
        _qkv_dil_kernel,
        name="qkv_dil",
        grid=(t // tm, _DIL_COL_STEPS),
        in_specs=[
            pl.BlockSpec((tm, D_MODEL), lambda i, j: (i, 0)),
            pl.BlockSpec((D_MODEL, QKV_TN), lambda i, j: (0, first_col_step + j)),
            pl.BlockSpec((tm, HEAD_DIM), lambda i, j: (i % tiles_per_seq, 0)),
            pl.BlockSpec((tm, HEAD_DIM), lambda i, j: (i % tiles_per_seq, 0)),
        ],
        out_specs=[fold_spec(f) for f in (FOLD, FOLD ** 2)],
        out_shape=[jax.ShapeDtypeStruct((batch, 3 * N_HEADS_DIL, f, seq // f, HEAD_DIM), BF16)
                   for f in (FOLD, FOLD ** 2)],
        scratch_shapes=[pltpu.VMEM((QKV_TN // HEAD_DIM, tm, HEAD_DIM), F32),
                        pltpu.VMEM((QKV_TN // HEAD_DIM, FOLD, tm // FOLD, HEAD_DIM), F32)],
        compiler_params=_params("parallel", "arbitrary"),
    )(u, w, cos_full, sin_signed)


def _rope_tables(seq):
    inv = jnp.float32(ROPE_THETA) ** (-jnp.arange(0, HEAD_DIM, 2, dtype=F32) / HEAD_DIM)
    ang = jnp.arange(seq, dtype=F32)[:, None] * inv[None, :]
    cos, sin = jnp.cos(ang), jnp.sin(ang)
    return jnp.concatenate([cos, cos], axis=-1), jnp.concatenate([-sin, sin], axis=-1)


def _na_tile_key_row_start(i, rows):
    return jnp.clip(i * NA_QROWS - NA_ROWS // 2, 0, rows - NA_KROWS)


def _na_kernel(q_ref, k_ref, v_ref, bias_ref, g_ref, o_ref, *, rows):
    i = pl.program_id(2)
    nk = NA_KROWS * GRID_W
    n_tiles = rows // NA_QROWS
    subs_per_tile = NA_QROWS * GRID_W // NA_SUB_ROWS
    tiles = [i * NA_STEP_TILES + t for t in range(NA_STEP_TILES)]
    starts = [pl.multiple_of(_na_tile_key_row_start(tile, rows) * GRID_W, GRID_W) for tile in tiles]
    classes = [jnp.where(tile == 0, 0, jnp.where(tile == n_tiles - 1, 2, 1)) for tile in tiles]
    sub = lambda n: slice(n * NA_SUB_ROWS, (n + 1) * NA_SUB_ROWS)

    def score(n):
        t, m = divmod(n, subs_per_tile)
        k = k_ref[0, 0, pl.ds(starts[t], nk), :]
        return lax.dot_general(q_ref[0, 0, sub(n), :], k, (((1,), (1,)), ((), ())),
                               preferred_element_type=F32) + bias_ref[classes[t], 0, sub(m), :]

    def softmax(s):
        e = jnp.exp(s - jnp.max(s, axis=-1, keepdims=True))
        return e.astype(BF16), jnp.sum(e, axis=-1, keepdims=True)

    def finish(n, prob):
        e, den = prob
        v = v_ref[0, 0, pl.ds(starts[n // subs_per_tile], nk), :]
        o = jnp.dot(e, v, preferred_element_type=F32) / den
        o_ref[0, 0, sub(n), :] = _rms(o, g_ref[0]).astype(BF16)

    _skewed(NA_STEP_TILES * subs_per_tile, NA_LAG, score, softmax, finish)


def _na_bias_table(rpb, rows):
    n_tiles = rows // NA_QROWS

    def row_pairs(tile):
        ks = int(np.clip(tile * NA_QROWS - NA_ROWS // 2, 0, rows - NA_KROWS))
        r = tile * NA_QROWS + np.arange(NA_QROWS)
        rs = np.clip(r - NA_ROWS // 2, 0, rows - NA_ROWS)
        kr = ks + np.arange(NA_KROWS)
        ok = (kr[None, :] >= rs[:, None]) & (kr[None, :] < rs[:, None] + NA_ROWS)
        assert (ok.sum(axis=1) == NA_ROWS).all()
        return ok, np.clip(kr[None, :] - r[:, None] + NA_ROWS - 1, 0, 2 * NA_ROWS - 2)

    row_ok, dr_idx = (np.stack(a) for a in zip(*(row_pairs(tile) for tile in (0, 1, n_tiles - 1))))
    for tile in range(1, n_tiles - 1):
        ok, dr = row_pairs(tile)
        assert (ok == row_ok[1]).all() and (np.where(ok, dr, 0) == np.where(ok, dr_idx[1], 0)).all()
    c = np.arange(GRID_W)
    qs = np.clip(c - NA_COLS // 2, 0, GRID_W - NA_COLS)
    col_ok = (c[None, :] >= qs[:, None]) & (c[None, :] < qs[:, None] + NA_COLS)
    n_dr, n_dc = 2 * NA_ROWS - 1, 2 * NA_COLS - 1
    period = 2 * GRID_W
    v = jnp.concatenate([rpb[..., NA_COLS - 1:], jnp.zeros((N_HEADS_NA, n_dr, period - n_dc), F32),
                         rpb[..., :NA_COLS - 1]], axis=-1).astype(F32)
    toe = jnp.tile(v, (1, 1, GRID_W))[..., :GRID_W * (period - 1)]
    toe = toe.reshape(N_HEADS_NA, n_dr, GRID_W, period - 1)[..., :GRID_W]
    slabs = jnp.where(col_ok[None, None], toe, NEG)
    slabs = jnp.concatenate([slabs, jnp.full((N_HEADS_NA, 1, GRID_W, GRID_W), NEG, F32)], axis=1)
    slab_idx = np.where(row_ok, dr_idx, n_dr)
    return pl.pallas_call(
        functools.partial(_na_bias_kernel, slab_idx=slab_idx),
        name="na_bias",
        grid=(N_HEADS_NA,),
        in_specs=[pl.BlockSpec((1, n_dr + 1, GRID_W, GRID_W), lambda h: (h, 0, 0, 0))],
        out_specs=pl.BlockSpec((3, 1, NA_QROWS * GRID_W, NA_KROWS * GRID_W), lambda h: (0, h, 0, 0)),
        out_shape=jax.ShapeDtypeStruct((3, N_HEADS_NA, NA_QROWS * GRID_W, NA_KROWS * GRID_W), F32),
        compiler_params=_params("parallel"),
    )(slabs)


def _na_bias_kernel(slabs_ref, o_ref, *, slab_idx):
    n_cls, n_q, n_k = slab_idx.shape
    for cls in range(n_cls):
        for rq in range(n_q):
            row = jnp.concatenate([slabs_ref[0, int(slab_idx[cls, rq, rk])] for rk in range(n_k)], axis=-1)
            o_ref[cls, 0, rq * GRID_W:(rq + 1) * GRID_W, :] = row


def _na(qkv, bias, head_g, batch, seq):
    rows = seq // GRID_W
    n_tiles = rows // NA_QROWS
    assert n_tiles % NA_STEP_TILES == 0
    tq = NA_QROWS * GRID_W

    return pl.pallas_call(
        functools.partial(_na_kernel, rows=rows),
        name="na",
        grid=(batch, N_HEADS_NA, n_tiles // NA_STEP_TILES),
        in_specs=[
            pl.BlockSpec((1, 1, NA_STEP_TILES * tq, HEAD_DIM), lambda b, h, i: (b, h, i, 0)),
            pl.BlockSpec((1, 1, seq, HEAD_DIM), lambda b, h, i: (b, N_HEADS_NA + h, 0, 0)),
            pl.BlockSpec((1, 1, seq, HEAD_DIM), lambda b, h, i: (b, 2 * N_HEADS_NA + h, 0, 0)),
            pl.BlockSpec((3, 1, tq, NA_KROWS * GRID_W), lambda b, h, i: (0, h, 0, 0)),
            pl.BlockSpec((1, 1, HEAD_DIM), lambda b, h, i: (h, 0, 0)),
        ],
        out_specs=pl.BlockSpec((1, 1, NA_STEP_TILES * tq, HEAD_DIM), lambda b, h, i: (b, h, i, 0)),
        out_shape=jax.ShapeDtypeStruct((batch, N_HEADS_NA, seq, HEAD_DIM), BF16),
        compiler_params=_params("parallel", "parallel", "arbitrary"),
    )(qkv, qkv, qkv, bias, head_g)


_DIL_WIN = DIL_TQ + 2 * DIL_HALF
_BRANCH_FOLD = tuple(max(dil, FOLD) for (_, dil) in DIL_PAIRS)


def _dil_branch(q_ref, k_ref, v_ref, mask_ref, os_ref, ls_ref, slot, part, *, dil, sub, length):
    fold = dil * sub
    qn, kn = DIL_TQ // sub, _DIL_WIN // sub
    n_i = length // qn
    n_p = n_i // DIL_PARTS
    row0 = part * (length // DIL_PARTS)

    def gather(ref, p, start, size):
        parts = [ref[0, 0, p + dil * c, pl.ds(start, size), :] for c in range(sub)]
        return parts[0] if sub == 1 else jnp.concatenate(parts, axis=0)

    def group(g, carry):
        tiles = []
        for n in range(DIL_GROUP):
            t = g * DIL_GROUP + n
            p = t // n_p
            i = part * n_p + t % n_p
            q0 = pl.multiple_of(i * qn, qn)
            k0 = pl.multiple_of(jnp.clip(q0 - DIL_HALF // sub, 0, length - kn), DIL_HALF // sub)
            edge = jnp.where(i == 0, 0, jnp.where(i == n_i - 1, 2, 1))
            tiles.append((p, q0, k0, edge))

        def score(n):
            p, q0, k0, edge = tiles[n]
            return lax.dot_general(gather(q_ref, p, q0, qn), gather(k_ref, p, k0, kn), (((1,), (1,)), ((), ())),
                                   preferred_element_type=F32) + mask_ref[edge]

        def softmax(s):
            m = jnp.max(s, axis=-1, keepdims=True)
            e = jnp.exp2(s - m)
            return m, e.astype(BF16), jnp.sum(e, axis=-1, keepdims=True)

        def finish(n, prob):
            p, q0, k0, edge = tiles[n]
            m, e, den = prob
            o = jnp.dot(e, gather(v_ref, p, k0, kn), preferred_element_type=F32) / den
            lse = jnp.broadcast_to(m + jnp.log2(den), (DIL_TQ, HEAD_DIM))
            for c in range(sub):
                rows = pl.ds(fold * (q0 - row0) + p + dil * c, qn, stride=fold)
                os_ref[slot, rows, :] = o[c * qn:(c + 1) * qn]
                ls_ref[slot, rows, :] = lse[c * qn:(c + 1) * qn]

        _skewed(DIL_GROUP, DIL_LAG, score, softmax, finish)
        return carry

    lax.fori_loop(0, dil * n_p // DIL_GROUP, group, 0)


def _dilated_kernel(qa_ref, ka_ref, va_ref, qb_ref, kb_ref, vb_ref, m0_ref, m1_ref, m2_ref, g_ref, o_ref,
                    os_ref, ls_ref, *, seq):
    part = pl.program_id(2)
    stored = {FOLD: (qa_ref, ka_ref, va_ref), FOLD ** 2: (qb_ref, kb_ref, vb_ref)}
    for slot, ((_, dil), fold, mask_ref) in enumerate(zip(DIL_PAIRS, _BRANCH_FOLD, (m0_ref, m1_ref, m2_ref))):
        _dil_branch(*stored[fold], mask_ref, os_ref, ls_ref, slot, part, dil=dil, sub=fold // dil,
                    length=seq // fold)

    def merge(c, carry):
        rows = pl.ds(pl.multiple_of(c * DIL_MERGE_ROWS, DIL_MERGE_ROWS), DIL_MERGE_ROWS)
        lses = [ls_ref[b, rows, :] for b in range(len(DIL_PAIRS))]
        mx = functools.reduce(jnp.maximum, lses)
        ws = [jnp.exp2(l - mx) for l in lses]
        num = functools.reduce(jnp.add, [w * os_ref[b, rows, :] for b, w in enumerate(ws)])
        o = num / functools.reduce(jnp.add, ws)
        o_ref[0, 0, rows, :] = _rms(o, g_ref[0]).astype(BF16)
        return carry

    lax.fori_loop(0, seq // DIL_PARTS // DIL_MERGE_ROWS, merge, 0)


def _dil_mask_table(sub, length):
    qn, kn = DIL_TQ // sub, _DIL_WIN // sub
    n_i = length // qn
    c = np.arange(sub)[:, None]
    out = []
    for i in (0, 1, n_i - 1):
        k0 = int(np.clip(i * qn - DIL_HALF // sub, 0, length - kn))
        qpos = ((i * qn + np.arange(qn))[None, :] * sub + c).reshape(-1)
        kpos = ((k0 + np.arange(kn))[None, :] * sub + c).reshape(-1)
        out.append(np.where(np.abs(kpos[None, :] - qpos[:, None]) <= DIL_HALF, 0.0, NEG))
    return jnp.asarray(np.stack(out), F32)


def _dilated(fa, fb, head_g, batch, seq):
    masks = []
    for (window, dil), fold in zip(DIL_PAIRS, _BRANCH_FOLD):
        sub, length = fold // dil, seq // fold
        assert window // (2 * dil) == DIL_HALF and fold % dil == 0 and DIL_TQ % sub == 0
        n_p = length // (DIL_TQ // sub) // DIL_PARTS
        assert n_p * DIL_PARTS * (DIL_TQ // sub) == length and (dil * n_p) % DIL_GROUP == 0 and n_p >= 1
        masks.append(_dil_mask_table(sub, length))

    def stored(f, first):
        return pl.BlockSpec((1, 1, f, seq // f, HEAD_DIM), lambda b, h, part: (b, first + h, 0, 0, 0))

    part_rows = seq // DIL_PARTS
    return pl.pallas_call(
        functools.partial(_dilated_kernel, seq=seq),
        name="dilated",
        grid=(batch, N_HEADS_DIL, DIL_PARTS),
        in_specs=[stored(f, first) for f in (FOLD, FOLD ** 2) for first in (0, N_HEADS_DIL, 2 * N_HEADS_DIL)]
        + [pl.BlockSpec((3, DIL_TQ, _DIL_WIN), lambda b, h, part: (0, 0, 0))] * len(DIL_PAIRS)
        + [pl.BlockSpec((1, 1, HEAD_DIM), lambda b, h, part: (N_HEADS_NA + h, 0, 0))],
        out_specs=pl.BlockSpec((1, 1, part_rows, HEAD_DIM), lambda b, h, part: (b, h, part, 0)),
        out_shape=jax.ShapeDtypeStruct((batch, N_HEADS_DIL, seq, HEAD_DIM), BF16),
        scratch_shapes=[pltpu.VMEM((len(DIL_PAIRS), part_rows, HEAD_DIM), F32)] * 2,
        compiler_params=_params("parallel", "parallel", "arbitrary"),
    )(fa, fa, fa, fb, fb, fb, *masks, head_g)


def _cast_once(w_ref, w16_ref):
    @pl.when(pl.program_id(0) == 0)
    def _():
        w16_ref[...] = w_ref[...].astype(BF16)


def _resident(shape):
    return pl.BlockSpec(shape, lambda i: (0,) * len(shape), pipeline_mode=pl.Buffered(1))


def _mix_out_kernel(ona_ref, odil_ref, wo_ref, h_ref, postg_ref, o_ref, wo16_ref):
    _cast_once(wo_ref, wo16_ref)
    heads = [ona_ref[0, h] for h in range(N_HEADS_NA)] + [odil_ref[0, h] for h in range(N_HEADS_DIL)]
    m = jnp.dot(jnp.concatenate(heads, axis=-1), wo16_ref[...], preferred_element_type=F32)
    o_ref[...] = h_ref[...] + _rms(m, postg_ref[...])


def _mix_out(o_na, o_dil, w_o, h, post_g, seq):
    t = h.shape[0]
    tiles_per_seq = seq // MIX_TM
    heads = lambda n: pl.BlockSpec((1, n, MIX_TM, HEAD_DIM), lambda i: (i // tiles_per_seq, 0, i % tiles_per_seq, 0))
    row = pl.BlockSpec((MIX_TM, D_MODEL), lambda i: (i, 0))
    const = lambda shape: pl.BlockSpec(shape, lambda i: (0, 0))
    return pl.pallas_call(
        _mix_out_kernel,
        name="mix_out",
        grid=(t // MIX_TM,),
        in_specs=[heads(N_HEADS_NA), heads(N_HEADS_DIL), _resident((D_MODEL, D_MODEL)), row, const((1, D_MODEL))],
        out_specs=row,
        out_shape=jax.ShapeDtypeStruct((t, D_MODEL), F32),
        scratch_shapes=[pltpu.VMEM((D_MODEL, D_MODEL), BF16)],
        compiler_params=_params("arbitrary"),
    )(o_na, o_dil, w_o, h, post_g)


def _ple_kernel(h_ref, p_ref, preg_ref, wg_ref, wp_ref, postg_ref, o_ref, wg16_ref, wp16_ref):
    _cast_once(wg_ref, wg16_ref)
    _cast_once(wp_ref, wp16_ref)
    h = h_ref[...]
    u = _rms(h, preg_ref[...]).astype(BF16)
    gate = jax.nn.sigmoid(jnp.dot(u, wg16_ref[...], preferred_element_type=F32))
    emb = jnp.dot(p_ref[...].astype(BF16), wp16_ref[...], preferred_element_type=F32)
    o_ref[...] = h + _rms(gate * emb, postg_ref[...])


def _ple(h, p, pre_g, w_gate, w_proj, post_g):
    t = h.shape[0]
    row = lambda width: pl.BlockSpec((PLE_TM, width), lambda i: (i, 0))
    const = lambda shape: pl.BlockSpec(shape, lambda i: (0, 0))
    return pl.pallas_call(
        _ple_kernel,
        name="ple",
        grid=(t // PLE_TM,),
        in_specs=[row(D_MODEL), row(PLE_DIM), const((1, D_MODEL)), _resident((D_MODEL, D_MODEL)),
                  _resident((PLE_DIM, D_MODEL)), const((1, D_MODEL))],
        out_specs=row(D_MODEL),
        out_shape=jax.ShapeDtypeStruct((t, D_MODEL), F32),
        scratch_shapes=[pltpu.VMEM((D_MODEL, D_MODEL), BF16), pltpu.VMEM((PLE_DIM, D_MODEL), BF16)],
        compiler_params=_params("arbitrary"),
    )(h, p, pre_g, w_gate, w_proj, post_g)


def kernel(x, p, ffn1_pre_g, ffn1_w_gate, ffn1_w_up, ffn1_w_down, ffn1_post_g, mix_pre_g, w_qkv, na_rpb, out_g, w_o, mix_post_g, ffn2_pre_g, ffn2_w_gate, ffn2_w_up, ffn2_w_down, ffn2_post_g, ple_pre_g, w_ple_gate, w_ple_proj, ple_post_g):
    batch, seq, d_model = x.shape
    depth = p.shape[0]
    assert d_model == D_MODEL and seq % (GRID_W * NA_QROWS) == 0 and seq // GRID_W >= NA_KROWS
    tokens = batch * seq
    rows = seq // GRID_W
    cos_full, sin_signed = _rope_tables(seq)
    gain = lambda g: g.reshape(1, D_MODEL)

    h = x.reshape(tokens, D_MODEL)
    for i in range(depth):
        h, (w_qkv16,) = _ffn(h, gain(ffn1_pre_g[i]), ffn1_w_gate[i], ffn1_w_up[i], ffn1_w_down[i],
                             gain(ffn1_post_g[i]), side=(w_qkv[i],))
        qkv_na, u_mix = _qkv_na(h, gain(mix_pre_g[i]), w_qkv16, batch, seq)
        fa, fb = _qkv_dil(u_mix, w_qkv16, cos_full, sin_signed, batch, seq)
        head_g = out_g[i].reshape(N_HEADS, 1, HEAD_DIM)
        o_na = _na(qkv_na, _na_bias_table(na_rpb[i], rows), head_g, batch, seq)
        o_dil = _dilated(fa, fb, head_g, batch, seq)
        h = _mix_out(o_na, o_dil, w_o[i], h, gain(mix_post_g[i]), seq)
        h, _ = _ffn(h, gain(ffn2_pre_g[i]), ffn2_w_gate[i], ffn2_w_up[i], ffn2_w_down[i], gain(ffn2_post_g[i]))
        h = _ple(h, p[i].reshape(tokens, PLE_DIM), gain(ple_pre_g[i]), w_ple_gate[i], w_ple_proj[i],
                 gain(ple_post_g[i]))
    return h.reshape(batch, seq, D_MODEL)
```
